```python
import jax, jax.numpy as jnp
from jax import lax
import numpy as np

D_MODEL = 2048
BATCH = 4
SEQ = 2048
DEPTH = 2

HEAD_DIM = 64
BLOCK = 128
ROPE_THETA = 10000.0
NORM_EPS = 1e-6

DIL_PATTERNS = ((128, 1), (512, 4), (2048, 16))
N_DIL = len(DIL_PATTERNS)
A_HEADS = 8
B_Q_HEADS = 8
B_KV_HEADS = 2
B_WINDOW = 128
C_HEADS = 8
D_HEADS = 8
D_NOPE = 64
D_ROPE = 32
D_V = 64
Q_LORA = 384
KV_LORA = 256

N_BRANCH = 4
BRANCH_WIDTH = 512

D_FF = 5632
N_EXPERTS = 8
TOP_K = 2
D_FF_EXPERT = 7168
N_DENSE = (DEPTH + 1) // 2
N_MOE = DEPTH // 2

A_W = N_DIL * A_HEADS * HEAD_DIM
B_QW = B_Q_HEADS * HEAD_DIM
B_KW = B_KV_HEADS * HEAD_DIM
C_W = C_HEADS * HEAD_DIM
IN_SPLITS = (A_W, A_W, A_W, B_QW, B_KW, B_KW, C_W, C_W, C_W, Q_LORA, KV_LORA, D_ROPE, N_BRANCH * D_MODEL)
IN_WIDTH = sum(IN_SPLITS)
SPLIT_POINTS = tuple(int(v) for v in np.cumsum(IN_SPLITS)[:-1])

kernel_name = 'hybrid_dilated_sink_stickbreak_mla_moe'

F32 = jnp.float32


def rmsnorm(x, g):
    xf = x.astype(F32)
    y = xf * lax.rsqrt(jnp.mean(xf * xf, axis=-1, keepdims=True) + NORM_EPS)
    return (y * g.astype(F32)).astype(x.dtype)


def rope(x, pos):
    dh = x.shape[-1]
    half = dh // 2
    freqs = ROPE_THETA ** (-2.0 * jnp.arange(half, dtype=F32) / dh)
    ang = pos.astype(F32)[:, None] * freqs[None, :]
    cos = jnp.cos(ang)[:, None, :]
    sin = jnp.sin(ang)[:, None, :]
    xf = x.astype(F32)
    x1, x2 = xf[..., :half], xf[..., half:]
    return jnp.concatenate([x1 * cos - x2 * sin, x2 * cos + x1 * sin], axis=-1).astype(x.dtype)


def to_strided(t, d):
    B, S = t.shape[:2]
    rest = t.shape[2:]
    return t.reshape(B, S // d, d, *rest).swapaxes(1, 2).reshape(B * d, S // d, *rest)


def from_strided(t, d, B):
    L = t.shape[1]
    rest = t.shape[2:]
    return t.reshape(B, d, L, *rest).swapaxes(1, 2).reshape(B, L * d, *rest)


def banded_attention(q, k, v, max_dist, sinks=None):
    N, L, Hq, dh = q.shape
    Hkv = k.shape[2]
    G = Hq // Hkv
    nb = -(-L // BLOCK)
    pad = nb * BLOCK - L
    padw = ((0, 0), (0, pad), (0, 0), (0, 0))
    q = jnp.pad(q, padw)
    k = jnp.pad(k, padw)
    v = jnp.pad(v, padw)
    qb = q.reshape(N, nb, BLOCK, Hkv, G, dh)

    def window(t):
        tb = t.reshape(N, nb, BLOCK, Hkv, dh)
        prev = jnp.pad(tb, ((0, 0), (1, 0), (0, 0), (0, 0), (0, 0)))[:, :nb]
        return jnp.concatenate([prev, tb], axis=2)

    kw, vw = window(k), window(v)
    s = jnp.einsum('nbqhgd,nbkhd->nbhgqk', qb, kw, preferred_element_type=F32) * (dh ** -0.5)
    qpos = jnp.arange(nb)[:, None, None] * BLOCK + jnp.arange(BLOCK)[None, :, None]
    kpos = jnp.arange(nb)[:, None, None] * BLOCK - BLOCK + jnp.arange(2 * BLOCK)[None, None, :]
    dist = qpos - kpos
    valid = (dist >= 0) & (dist <= max_dist) & (kpos >= 0)
    s = jnp.where(valid[None, :, None, None], s, -jnp.inf)
    m = s.max(axis=-1)
    if sinks is not None:
        sk = sinks.astype(F32).reshape(Hkv, G)[None, None, :, :, None]
        m = jnp.maximum(m, sk)
    p = jnp.exp(s - m[..., None])
    denom = p.sum(axis=-1)
    if sinks is not None:
        denom = denom + jnp.exp(sk - m)
    o = jnp.einsum('nbhgqk,nbkhd->nbqhgd', p, vw.astype(F32)) / jnp.moveaxis(denom, -1, 2)[..., None]
    lse = jnp.moveaxis(m + jnp.log(denom), -1, 2).reshape(N, nb * BLOCK, Hq)[:, :L]
    o = o.reshape(N, nb * BLOCK, Hq, dh)[:, :L]
    return o, lse


def dilated_attention(q, k, v, pos):
    B, S = q.shape[:2]
    q = rope(q, pos)
    k = rope(k, pos)
    outs, lses = [], []
    for g, (win, dil) in enumerate(DIL_PATTERNS):
        hs = slice(g * A_HEADS, (g + 1) * A_HEADS)
        o, lse = banded_attention(to_strided(q[:, :, hs], dil), to_strided(k[:, :, hs], dil),
                                  to_strided(v[:, :, hs], dil), win // dil)
        outs.append(from_strided(o, dil, B))
        lses.append(from_strided(lse, dil, B))
    w = jax.nn.softmax(jnp.stack(lses), axis=0)
    o = jnp.einsum('gbsh,gbshd->bshd', w, jnp.stack(outs))
    return o.reshape(B, S, A_HEADS * HEAD_DIM).astype(q.dtype)


def sliding_sink_attention(q, k, v, sinks, pos):
    B, S = q.shape[:2]
    o, _ = banded_attention(rope(q, pos), rope(k, pos), v, B_WINDOW - 1, sinks)
    return o.reshape(B, S, B_Q_HEADS * HEAD_DIM).astype(q.dtype)


def stick_breaking_attention(q, k, v):
    B, S, H, dh = q.shape
    nb = S // BLOCK
    qb = q.reshape(B, nb, BLOCK, H, dh).swapaxes(0, 1)
    kpos = jnp.arange(S)
    scale = dh ** -0.5
    vf = v.astype(F32)

    def block(args):
        q_blk, b = args
        z = jnp.einsum('bqhd,bkhd->bhqk', q_blk, k, preferred_element_type=F32) * scale
        qpos = b * BLOCK + jnp.arange(BLOCK)
        before = (kpos[None, :] < qpos[:, None])[None, None]
        log_stay = jnp.where(before, jax.nn.log_sigmoid(-z), 0.0)
        log_after = lax.cumsum(log_stay, axis=3, reverse=True) - log_stay
        a = jnp.where(before, jnp.exp(jax.nn.log_sigmoid(z) + log_after), 0.0)
        return jnp.einsum('bhqk,bkhd->bqhd', a, vf)

    o = lax.map(block, (qb, jnp.arange(nb)))
    return o.swapaxes(0, 1).reshape(B, S, H * dh).astype(q.dtype)


def latent_attention(c_q, c_kv, k_rope, g_qa, g_kva, w_uq, w_ukv, pos):
    B, S = c_q.shape[:2]
    q = (rmsnorm(c_q, g_qa) @ w_uq).reshape(B, S, D_HEADS, D_NOPE + D_ROPE)
    q_nope, q_rope = q[..., :D_NOPE], rope(q[..., D_NOPE:], pos)
    kv = (rmsnorm(c_kv, g_kva) @ w_ukv).reshape(B, S, D_HEADS, D_NOPE + D_V)
    k_nope, v = kv[..., :D_NOPE], kv[..., D_NOPE:].astype(F32)
    k_r = rope(k_rope[:, :, None, :], pos)[:, :, 0]
    scale = (D_NOPE + D_ROPE) ** -0.5
    nb = S // BLOCK
    qn = q_nope.reshape(B, nb, BLOCK, D_HEADS, D_NOPE).swapaxes(0, 1)
    qr = q_rope.reshape(B, nb, BLOCK, D_HEADS, D_ROPE).swapaxes(0, 1)
    kpos = jnp.arange(S)

    def block(args):
        qn_b, qr_b, b = args
        s = (jnp.einsum('bqhd,bkhd->bhqk', qn_b, k_nope, preferred_element_type=F32)
             + jnp.einsum('bqhr,bkr->bhqk', qr_b, k_r, preferred_element_type=F32)) * scale
        qpos = b * BLOCK + jnp.arange(BLOCK)
        s = jnp.where(kpos[None, :] <= qpos[:, None], s, -jnp.inf)
        p = jax.nn.softmax(s, axis=-1)
        return jnp.einsum('bhqk,bkhd->bqhd', p, v)

    o = lax.map(block, (qn, qr, jnp.arange(nb)))
    return o.swapaxes(0, 1).reshape(B, S, D_HEADS * D_V).astype(c_q.dtype)


def mixer_block(n, pos, w_in, w_branch, w_out, sinks, g_qa, g_kva, w_uq, w_ukv):
    B, S, _ = n.shape
    proj = n @ w_in
    (aq, ak, av, bq, bk, bv, cq, ck, cv, dcq, dckv, dkr, gates) = jnp.split(proj, SPLIT_POINTS, axis=-1)
    heads = lambda t: t.reshape(B, S, -1, HEAD_DIM)
    o_a = dilated_attention(heads(aq), heads(ak), heads(av), pos)
    o_b = sliding_sink_attention(heads(bq), heads(bk), heads(bv), sinks, pos)
    o_c = stick_breaking_attention(heads(cq), heads(ck), heads(cv))
    o_d = latent_attention(dcq, dckv, dkr, g_qa, g_kva, w_uq, w_ukv, pos)
    branches = jnp.stack([o_a, o_b, o_c, o_d], axis=2)
    g = jax.nn.sigmoid(gates.reshape(B, S, N_BRANCH, D_MODEL).astype(F32))
    merged = (jnp.einsum('bsne,ned->bsnd', branches, w_branch, preferred_element_type=F32) * g).sum(axis=2)
    return merged.astype(n.dtype) @ w_out


def swiglu(x, w_gate, w_up, w_down):
    return (jax.nn.silu(x @ w_gate) * (x @ w_up)) @ w_down


def moe_swiglu(x, w_router, w_gate, w_up, w_down):
    B, S, D = x.shape
    xt = x.reshape(B * S, D)
    logits = (xt @ w_router).astype(F32)
    top_val, top_idx = lax.top_k(logits, TOP_K)
    top_w = jax.nn.softmax(top_val, axis=-1)
    combine = jnp.einsum('tk,tke->te', top_w, jax.nn.one_hot(top_idx, N_EXPERTS, dtype=F32))
    out = sum(combine[:, e, None] * swiglu(xt, w_gate[e], w_up[e], w_down[e]) for e in range(N_EXPERTS))
    return out.reshape(B, S, D).astype(x.dtype)


def setup_inputs(seed: int = 0) -> dict:
    key = jax.random.key(seed)
    ks = jax.random.split(key, 20)
    nrm = lambda k, shape, fan_in: jax.random.normal(k, shape, F32) * (fan_in ** -0.5)
    gain = lambda k, shape: 1.0 + 0.02 * jax.random.normal(k, shape, F32)
    return {
        'x': jax.random.normal(ks[0], (BATCH, SEQ, D_MODEL), F32),
        'w_in': nrm(ks[1], (DEPTH, D_MODEL, IN_WIDTH), D_MODEL),
        'w_branch': nrm(ks[2], (DEPTH, N_BRANCH, BRANCH_WIDTH, D_MODEL), BRANCH_WIDTH),
        'w_out': nrm(ks[3], (DEPTH, D_MODEL, D_MODEL), D_MODEL),
        'norm_mix': gain(ks[4], (DEPTH, D_MODEL)),
        'norm_ffn': gain(ks[5], (DEPTH, D_MODEL)),
        'norm_final': gain(ks[6], (D_MODEL,)),
        'sinks': 0.5 * jax.random.normal(ks[7], (DEPTH, B_Q_HEADS), F32),
        'mla_q_norm': gain(ks[8], (DEPTH, Q_LORA)),
        'mla_kv_norm': gain(ks[9], (DEPTH, KV_LORA)),
        'mla_w_uq': nrm(ks[10], (DEPTH, Q_LORA, D_HEADS * (D_NOPE + D_ROPE)), Q_LORA),
        'mla_w_ukv': nrm(ks[11], (DEPTH, KV_LORA, D_HEADS * (D_NOPE + D_V)), KV_LORA),
        'ffn_w_gate': nrm(ks[12], (N_DENSE, D_MODEL, D_FF), D_MODEL),
        'ffn_w_up': nrm(ks[13], (N_DENSE, D_MODEL, D_FF), D_MODEL),
        'ffn_w_down': nrm(ks[14], (N_DENSE, D_FF, D_MODEL), D_FF),
        'router_w': nrm(ks[15], (N_MOE, D_MODEL, N_EXPERTS), D_MODEL),
        'moe_w_gate': nrm(ks[16], (N_MOE, N_EXPERTS, D_MODEL, D_FF_EXPERT), D_MODEL),
        'moe_w_up': nrm(ks[17], (N_MOE, N_EXPERTS, D_MODEL, D_FF_EXPERT), D_MODEL),
        'moe_w_down': nrm(ks[18], (N_MOE, N_EXPERTS, D_FF_EXPERT, D_MODEL), D_FF_EXPERT),
    }


def reference(x, w_in, w_branch, w_out, norm_mix, norm_ffn, norm_final, sinks, mla_q_norm, mla_kv_norm,
              mla_w_uq, mla_w_ukv, ffn_w_gate, ffn_w_up, ffn_w_down, router_w, moe_w_gate, moe_w_up, moe_w_down):
    pos = jnp.arange(x.shape[1])
    h = x
    for layer in range(DEPTH):
        n = rmsnorm(h, norm_mix[layer])
        h = h + mixer_block(n, pos, w_in[layer], w_branch[layer], w_out[layer], sinks[layer],
                            mla_q_norm[layer], mla_kv_norm[layer], mla_w_uq[layer], mla_w_ukv[layer])
        n = rmsnorm(h, norm_ffn[layer])
        i = layer // 2
        if layer % 2 == 0:
            h = h + swiglu(n, ffn_w_gate[i], ffn_w_up[i], ffn_w_down[i])
        else:
            h = h + moe_swiglu(n, router_w[i], moe_w_gate[i], moe_w_up[i], moe_w_down[i])
    return rmsnorm(h, norm_final)
```

```python
import functools

import numpy as np
import jax
import jax.numpy as jnp
from jax import lax
from jax.experimental import pallas as pl
from jax.experimental.pallas import tpu as pltpu

F32 = jnp.float32
BF16 = jnp.bfloat16

D_MODEL = 2048
SEQ = 2048
HEAD_DIM = 64
BLOCK = 128
LANES = 128
ROPE_THETA = 10000.0
NORM_EPS = 1e-6
DIL_PATTERNS = ((128, 1), (512, 4), (2048, 16))
B_WINDOW = 128
D_NOPE = 64
D_ROPE = 32
Q_LORA = 384
KV_LORA = 256
N_BRANCH = 4
BRANCH_WIDTH = 512
N_EXPERTS = 8
TOP_K = 2

A_Q0, A_K0, A_V0 = 0, 12, 24
B_Q0, B_K0, B_V0 = 36, 40, 41
C_Q0, C_K0, C_V0 = 42, 46, 50
MLA_COL0 = 6912
MLA_BLOCK_W = 768
QKV_WIDTH = 7680
GATE_COL0 = 7584
PROJ_TN = 256

VMEM_LIMIT = 56 * 1024 * 1024
NEG = -1e30

MOE_SUB = 256
MOE_ITEM_SUBS = 4
MOE_TF = 256


def _cparams(n_axes):
    return pltpu.CompilerParams(dimension_semantics=("arbitrary",) * n_axes, vmem_limit_bytes=VMEM_LIMIT)


def _lane_half():
    return lax.broadcasted_iota(jnp.int32, (BLOCK, LANES), 1) < HEAD_DIM


def _rms(x, g):
    return x * lax.rsqrt(jnp.mean(x * x, axis=-1, keepdims=True) + NORM_EPS) * g


def _rmsnorm_kernel(x_ref, g_ref, o_ref):
    o_ref[...] = _rms(x_ref[...], g_ref[...]).astype(o_ref.dtype)


def _rmsnorm(x, g, out_dtype, tm=512):
    T, D = x.shape
    return pl.pallas_call(
        _rmsnorm_kernel,
        out_shape=jax.ShapeDtypeStruct((T, D), out_dtype),
        grid=(T // tm,),
        in_specs=[pl.BlockSpec((tm, D), lambda i: (i, 0)), pl.BlockSpec((1, D), lambda i: (0, 0))],
        out_specs=pl.BlockSpec((tm, D), lambda i: (i, 0)),
        compiler_params=_cparams(1),
        name="rmsnorm",
    )(x, g.reshape(1, D))


def _rope_tables(dh, lane0, period):
    half = dh // 2
    freqs = ROPE_THETA ** (-2.0 * jnp.arange(half, dtype=F32) / dh)
    ang = jnp.arange(SEQ, dtype=F32)[:, None] * freqs[None, :]
    cos, sin = jnp.cos(ang), jnp.sin(ang)
    cos_h = jnp.concatenate([cos, cos], axis=1)
    sin_h = jnp.concatenate([-sin, sin], axis=1)
    ones = jnp.ones((SEQ, period - dh), F32)
    cos_p = jnp.concatenate([cos_h, ones], axis=1)
    sin_p = jnp.concatenate([sin_h, 0.0 * ones], axis=1)
    reps = LANES // period
    cos_t = jnp.roll(jnp.tile(cos_p, (1, reps)), lane0, axis=1)
    sin_t = jnp.roll(jnp.tile(sin_p, (1, reps)), lane0, axis=1)
    return cos_t, sin_t


def _rope_lanes(x, cos, sin, half, first_half_mask):
    fwd = pltpu.roll(x, LANES - half, 1)
    bwd = pltpu.roll(x, half, 1)
    return x * cos + jnp.where(first_half_mask, fwd, bwd) * sin


def _proj_kernel(x_ref, w_ref, cos_ref, sin_ref, o_ref):
    j = pl.program_id(1)
    acc = jnp.dot(x_ref[...], w_ref[...].astype(BF16), preferred_element_type=F32)
    rope_both = (j < 12) | ((j >= 18) & (j < 20))
    rope_first = j == 20

    def roped(t):
        lane = lax.broadcasted_iota(jnp.int32, t.shape, 1)
        return _rope_lanes(t, cos_ref[...], sin_ref[...], HEAD_DIM // 2, (lane % HEAD_DIM) < HEAD_DIM // 2)

    @pl.when(rope_both)
    def _():
        o_ref[:, :LANES] = roped(acc[:, :LANES])
        o_ref[:, LANES:] = roped(acc[:, LANES:])

    @pl.when(rope_first)
    def _():
        o_ref[:, :LANES] = roped(acc[:, :LANES])
        o_ref[:, LANES:] = acc[:, LANES:]

    @pl.when(jnp.logical_not(rope_both | rope_first))
    def _():
        o_ref[...] = acc


def _in_proj(n, w_in, layer, cos, sin):
    T, D = n.shape
    nb = T // SEQ
    return pl.pallas_call(
        _proj_kernel,
        out_shape=jax.ShapeDtypeStruct((T, QKV_WIDTH), F32),
        grid=(nb, QKV_WIDTH // PROJ_TN),
        in_specs=[
            pl.BlockSpec((SEQ, D), lambda b, j: (b, 0)),
            pl.BlockSpec((None, D, PROJ_TN), lambda b, j: (layer, 0, j)),
            pl.BlockSpec((SEQ, LANES), lambda b, j: (0, 0)),
            pl.BlockSpec((SEQ, LANES), lambda b, j: (0, 0)),
        ],
        out_specs=pl.BlockSpec((SEQ, PROJ_TN), lambda b, j: (b, j)),
        compiler_params=_cparams(2),
        name="in_proj_rope",
    )(n, w_in, cos, sin)


def _band_scores(q, kcat, vcat, head, has_prev, strict_prev):
    half = _lane_half()
    qh = jnp.where(half if head == 0 else jnp.logical_not(half), q * (HEAD_DIM ** -0.5), 0.0).astype(BF16)
    s = lax.dot_general(qh, kcat, (((1,), (1,)), ((), ())), preferred_element_type=F32)
    row = lax.broadcasted_iota(jnp.int32, (BLOCK, 2 * BLOCK), 0)
    col = lax.broadcasted_iota(jnp.int32, (BLOCK, 2 * BLOCK), 1)
    dist = row + BLOCK - col
    max_dist = BLOCK - 1 if strict_prev else BLOCK
    first_col = jnp.where(has_prev, 0, BLOCK)
    valid = (dist >= 0) & (dist <= max_dist) & (col >= first_col)
    s = jnp.where(valid, s, NEG)
    m = jnp.max(s, axis=1, keepdims=True)
    p = jnp.exp(s - m)
    l = jnp.sum(p, axis=1, keepdims=True)
    acc = jnp.dot(p.astype(BF16), vcat, preferred_element_type=F32)
    return acc, m, l


def _pair(a0, a1):
    return jnp.where(_lane_half(), a0, a1)


def _dilated_kernel(*refs):
    q_refs, k_refs, v_refs = refs[0:3], refs[3:6], refs[6:9]
    o_ref = refs[9]
    acc_sc, m_sc, l_sc = refs[10:13]
    n_blocks = SEQ // BLOCK

    for g, (win, dil) in enumerate(DIL_PATTERNS):
        assert win // dil == BLOCK and SEQ % (dil * BLOCK) == 0
        q_ref, k_ref, v_ref = q_refs[g], k_refs[g], v_refs[g]

        def rows(start, dil=dil):
            return pl.ds(start, BLOCK) if dil == 1 else pl.ds(start, BLOCK, stride=dil)

        def body(it, carry, g=g, dil=dil, q_ref=q_ref, k_ref=k_ref, v_ref=v_ref, rows=rows):
            r = it % dil
            i = it // dil
            base = i * (BLOCK * dil) + r
            prev = jnp.maximum(base - BLOCK * dil, 0)
            q = q_ref[rows(base), :]
            kcat = jnp.concatenate([k_ref[rows(prev), :], k_ref[rows(base), :]], axis=0).astype(BF16)
            vcat = jnp.concatenate([v_ref[rows(prev), :], v_ref[rows(base), :]], axis=0).astype(BF16)
            a0, m0, l0 = _band_scores(q, kcat, vcat, 0, i > 0, False)
            a1, m1, l1 = _band_scores(q, kcat, vcat, 1, i > 0, False)
            acc_sc[g, rows(base), :] = _pair(a0, a1)
            m_sc[g, rows(base), :] = _pair(m0, m1)
            l_sc[g, rows(base), :] = _pair(l0, l1)
            return carry

        lax.fori_loop(0, n_blocks, body, 0)

    def combine(i, carry):
        rows = pl.ds(pl.multiple_of(i * BLOCK, BLOCK), BLOCK)
        m = [m_sc[g, rows, :] for g in range(len(DIL_PATTERNS))]
        m_all = jnp.maximum(jnp.maximum(m[0], m[1]), m[2])
        num = jnp.zeros((BLOCK, LANES), F32)
        den = jnp.zeros((BLOCK, LANES), F32)
        for g in range(len(DIL_PATTERNS)):
            a = jnp.exp(m[g] - m_all)
            num = num + a * acc_sc[g, rows, :]
            den = den + a * l_sc[g, rows, :]
        o_ref[rows, :] = (num / den).astype(o_ref.dtype)
        return carry

    lax.fori_loop(0, n_blocks, combine, 0)


def _dilated_attention(proj, nb):
    T = proj.shape[0]
    specs = []
    for col0 in (A_Q0, A_K0, A_V0):
        for g in range(len(DIL_PATTERNS)):
            specs.append(pl.BlockSpec((SEQ, LANES), lambda b, hp, c=col0 + 4 * g: (b, c + hp)))
    return pl.pallas_call(
        _dilated_kernel,
        out_shape=jax.ShapeDtypeStruct((T, BRANCH_WIDTH), BF16),
        grid=(nb, BRANCH_WIDTH // LANES),
        in_specs=specs,
        out_specs=pl.BlockSpec((SEQ, LANES), lambda b, hp: (b, hp)),
        scratch_shapes=[pltpu.VMEM((len(DIL_PATTERNS), SEQ, LANES), F32)] * 3,
        compiler_params=_cparams(2),
        name="dilated_attention",
    )(*([proj] * 9))


def _sink_kernel(q_ref, k_ref, v_ref, sink_ref, o_ref, k_sc, v_sc):
    hp = pl.program_id(1)
    kv_head = hp // 2
    k = k_ref[...]
    v = v_ref[...]
    k_sw = pltpu.roll(k, HEAD_DIM, 1)
    v_sw = pltpu.roll(v, HEAD_DIM, 1)
    first = kv_head == 0
    k_sc[0] = jnp.where(first, k, k_sw).astype(BF16)
    k_sc[1] = jnp.where(first, k_sw, k).astype(BF16)
    v_sc[0] = jnp.where(first, v, v_sw).astype(BF16)
    v_sc[1] = jnp.where(first, v_sw, v).astype(BF16)
    sink = _pair(sink_ref[pl.ds(2 * hp, 1), :], sink_ref[pl.ds(2 * hp + 1, 1), :])

    def body(i, carry):
        base = pl.multiple_of(i * BLOCK, BLOCK)
        prev = pl.multiple_of(jnp.maximum(base - BLOCK, 0), BLOCK)
        q = q_ref[pl.ds(base, BLOCK), :]
        out = []
        for a in range(2):
            kcat = jnp.concatenate([k_sc[a, pl.ds(prev, BLOCK), :], k_sc[a, pl.ds(base, BLOCK), :]], axis=0)
            vcat = jnp.concatenate([v_sc[a, pl.ds(prev, BLOCK), :], v_sc[a, pl.ds(base, BLOCK), :]], axis=0)
            out.append(_band_scores(q, kcat, vcat, a, i > 0, True))
        acc = _pair(out[0][0], out[1][0])
        m = _pair(out[0][1], out[1][1])
        l = _pair(out[0][2], out[1][2])
        m_new = jnp.maximum(m, sink)
        scale = jnp.exp(m - m_new)
        den = l * scale + jnp.exp(sink - m_new)
        o_ref[pl.ds(base, BLOCK), :] = (acc * scale / den).astype(o_ref.dtype)
        return carry

    lax.fori_loop(0, SEQ // BLOCK, body, 0)


def _sink_attention(proj, sinks_l, nb):
    T = proj.shape[0]
    sink_tab = jnp.broadcast_to(sinks_l.astype(F32)[:, None], (8, LANES))
    return pl.pallas_call(
        _sink_kernel,
        out_shape=jax.ShapeDtypeStruct((T, BRANCH_WIDTH), BF16),
        grid=(nb, BRANCH_WIDTH // LANES),
        in_specs=[
            pl.BlockSpec((SEQ, LANES), lambda b, hp: (b, B_Q0 + hp)),
            pl.BlockSpec((SEQ, LANES), lambda b, hp: (b, B_K0)),
            pl.BlockSpec((SEQ, LANES), lambda b, hp: (b, B_V0)),
            pl.BlockSpec((8, LANES), lambda b, hp: (0, 0)),
        ],
        out_specs=pl.BlockSpec((SEQ, LANES), lambda b, hp: (b, hp)),
        scratch_shapes=[pltpu.VMEM((2, SEQ, LANES), BF16)] * 2,
        compiler_params=_cparams(2),
        name="sink_attention",
    )(proj, proj, proj, sink_tab)


def _stick_kernel(q_ref, k_ref, v_ref, tri_ref, o_ref, q_sc, k_sc, v_sc, carry_sc, acc_sc):
    half = lax.broadcasted_iota(jnp.int32, (SEQ, LANES), 1) < HEAD_DIM
    q = q_ref[...] * (HEAD_DIM ** -0.5)
    q_sc[0] = jnp.where(half, q, 0.0).astype(BF16)
    q_sc[1] = jnp.where(half, 0.0, q).astype(BF16)
    k_sc[...] = k_ref[...].astype(BF16)
    v_sc[...] = v_ref[...].astype(BF16)
    carry_sc[...] = jnp.zeros_like(carry_sc)
    acc_sc[...] = jnp.zeros_like(acc_sc)
    tri = tri_ref[...]
    n_blocks = SEQ // BLOCK
    lane_half = _lane_half()

    def tile(i, j, diagonal):
        qb = pl.multiple_of(i * BLOCK, BLOCK)
        kb = pl.multiple_of(j * BLOCK, BLOCK)
        kj = k_sc[pl.ds(kb, BLOCK), :]
        vj = v_sc[pl.ds(kb, BLOCK), :]
        outs = []
        for a in range(2):
            z = lax.dot_general(q_sc[a, pl.ds(qb, BLOCK), :], kj, (((1,), (1,)), ((), ())),
                                preferred_element_type=F32)
            t = jnp.log1p(jnp.exp(-jnp.abs(z)))
            log_beta = jnp.minimum(z, 0.0) - t
            log_stay = jnp.minimum(-z, 0.0) - t
            if diagonal:
                row = lax.broadcasted_iota(jnp.int32, (BLOCK, BLOCK), 0)
                col = lax.broadcasted_iota(jnp.int32, (BLOCK, BLOCK), 1)
                before = col < row
                log_stay = jnp.where(before, log_stay, 0.0)
            hi = log_stay.astype(BF16)
            lo = (log_stay - hi.astype(F32)).astype(BF16)
            sums = (jnp.dot(hi, tri, preferred_element_type=F32) + jnp.dot(lo, tri, preferred_element_type=F32))
            carry = carry_sc[a, pl.ds(qb, BLOCK), :]
            log_after = carry + sums[:, :BLOCK] - log_stay
            w = jnp.exp(log_beta + log_after)
            if diagonal:
                w = jnp.where(before, w, 0.0)
            carry_sc[a, pl.ds(qb, BLOCK), :] = carry + sums[:, BLOCK:]
            outs.append(jnp.dot(w.astype(BF16), vj, preferred_element_type=F32))
        acc_sc[pl.ds(qb, BLOCK), :] += jnp.where(lane_half, outs[0], outs[1])

    def key_block(jj, c):
        j = n_blocks - 1 - jj
        tile(j, j, True)

        def q_block(i, c2):
            tile(i, j, False)
            return c2

        lax.fori_loop(j + 1, n_blocks, q_block, 0)
        return c

    lax.fori_loop(0, n_blocks, key_block, 0)
    o_ref[...] = acc_sc[...].astype(o_ref.dtype)


def _stick_attention(proj, nb):
    T = proj.shape[0]
    jp = np.arange(BLOCK)[:, None]
    s = np.arange(BLOCK)[None, :]
    tri = jnp.asarray(np.concatenate([(jp >= s), np.ones((BLOCK, BLOCK), bool)], axis=1), BF16)
    return pl.pallas_call(
        _stick_kernel,
        out_shape=jax.ShapeDtypeStruct((T, BRANCH_WIDTH), BF16),
        grid=(nb, BRANCH_WIDTH // LANES),
        in_specs=[
            pl.BlockSpec((SEQ, LANES), lambda b, hp: (b, C_Q0 + hp)),
            pl.BlockSpec((SEQ, LANES), lambda b, hp: (b, C_K0 + hp)),
            pl.BlockSpec((SEQ, LANES), lambda b, hp: (b, C_V0 + hp)),
            pl.BlockSpec((BLOCK, 2 * BLOCK), lambda b, hp: (0, 0)),
        ],
        out_specs=pl.BlockSpec((SEQ, LANES), lambda b, hp: (b, hp)),
        scratch_shapes=[
            pltpu.VMEM((2, SEQ, LANES), BF16),
            pltpu.VMEM((SEQ, LANES), BF16),
            pltpu.VMEM((SEQ, LANES), BF16),
            pltpu.VMEM((2, SEQ, LANES), F32),
            pltpu.VMEM((SEQ, LANES), F32),
        ],
        compiler_params=_cparams(2),
        name="stick_breaking_attention",
    )(proj, proj, proj, tri)


MLA_HEADS = 8
MLA_TM = 512


def _mla_prep_kernel(c_ref, gq_ref, gkv_ref, wq_ref, wk_ref, wv_ref, cos_ref, sin_ref, q_ref, k_ref, v_ref):
    c = c_ref[...]
    cq = _rms(c[:, :Q_LORA], gq_ref[...]).astype(BF16)
    ckv = _rms(c[:, Q_LORA:Q_LORA + KV_LORA], gkv_ref[...]).astype(BF16)
    q = jnp.dot(cq, wq_ref[...].astype(BF16), preferred_element_type=F32)
    k = jnp.dot(ckv, wk_ref[...].astype(BF16), preferred_element_type=F32)
    v_ref[...] = jnp.dot(ckv, wv_ref[...].astype(BF16), preferred_element_type=F32).astype(BF16)
    cos, sin = cos_ref[...], sin_ref[...]
    lane = lax.broadcasted_iota(jnp.int32, (MLA_TM, LANES), 1)
    rope_lane = (lane >= D_NOPE) & (lane < D_NOPE + D_ROPE)
    first = lane < D_NOPE + D_ROPE // 2
    kr = pltpu.roll(c[:, Q_LORA + KV_LORA:], D_NOPE, 1)
    kr = jnp.where(rope_lane, _rope_lanes(kr, cos, sin, D_ROPE // 2, first), 0.0)
    scale = (D_NOPE + D_ROPE) ** -0.5
    for h in range(MLA_HEADS):
        sl = slice(h * LANES, (h + 1) * LANES)
        q_ref[:, sl] = (_rope_lanes(q[:, sl], cos, sin, D_ROPE // 2, first) * scale).astype(BF16)
        k_ref[:, sl] = (k[:, sl] + kr).astype(BF16)


def _mla_prep(proj, g_qa, g_kva, w_uq, w_ukv, cos, sin):
    T = proj.shape[0]
    wq = jnp.pad(w_uq.reshape(Q_LORA, MLA_HEADS, D_NOPE + D_ROPE), ((0, 0), (0, 0), (0, LANES - D_NOPE - D_ROPE)))
    wq = wq.reshape(Q_LORA, MLA_HEADS * LANES)
    wkv = w_ukv.reshape(KV_LORA, MLA_HEADS, 2 * HEAD_DIM)
    wk = jnp.pad(wkv[:, :, :D_NOPE], ((0, 0), (0, 0), (0, LANES - D_NOPE))).reshape(KV_LORA, MLA_HEADS * LANES)
    wv = wkv[:, :, D_NOPE:].reshape(KV_LORA, MLA_HEADS * HEAD_DIM)
    full = lambda shape: pl.BlockSpec(shape, lambda i: (0, 0))
    per_seq = SEQ // MLA_TM
    return pl.pallas_call(
        _mla_prep_kernel,
        out_shape=[jax.ShapeDtypeStruct((T, MLA_HEADS * LANES), BF16),
                   jax.ShapeDtypeStruct((T, MLA_HEADS * LANES), BF16),
                   jax.ShapeDtypeStruct((T, MLA_HEADS * HEAD_DIM), BF16)],
        grid=(T // MLA_TM,),
        in_specs=[
            pl.BlockSpec((MLA_TM, MLA_BLOCK_W), lambda i: (i, MLA_COL0 // MLA_BLOCK_W)),
            full((1, Q_LORA)), full((1, KV_LORA)),
            full(wq.shape), full(wk.shape), full(wv.shape),
            pl.BlockSpec((MLA_TM, LANES), lambda i: (i % per_seq, 0)),
            pl.BlockSpec((MLA_TM, LANES), lambda i: (i % per_seq, 0)),
        ],
        out_specs=[pl.BlockSpec((MLA_TM, MLA_HEADS * LANES), lambda i: (i, 0)),
                   pl.BlockSpec((MLA_TM, MLA_HEADS * LANES), lambda i: (i, 0)),
                   pl.BlockSpec((MLA_TM, MLA_HEADS * HEAD_DIM), lambda i: (i, 0))],
        compiler_params=_cparams(1),
        name="mla_prep",
    )(proj, g_qa.reshape(1, Q_LORA), g_kva.reshape(1, KV_LORA), wq, wk, wv, cos, sin)


MLA_TQ = 256
MLA_TK = 256


def _mla_attn_kernel(q_ref, k_ref, v_ref, o_ref):
    lane_half = lax.broadcasted_iota(jnp.int32, (MLA_TQ, LANES), 1) < HEAD_DIM

    def q_block(qi, c):
        qb = pl.multiple_of(qi * MLA_TQ, MLA_TQ)
        outs = []
        for a in range(2):
            q = q_ref[pl.ds(qb, MLA_TQ), a * LANES:(a + 1) * LANES]

            def k_block(kj, carry, a=a, q=q):
                m, l, acc = carry
                kb = pl.multiple_of(kj * MLA_TK, MLA_TK)
                k = k_ref[pl.ds(kb, MLA_TK), a * LANES:(a + 1) * LANES]
                s = lax.dot_general(q, k, (((1,), (1,)), ((), ())), preferred_element_type=F32)
                row = lax.broadcasted_iota(jnp.int32, (MLA_TQ, MLA_TK), 0) + qb
                col = lax.broadcasted_iota(jnp.int32, (MLA_TQ, MLA_TK), 1) + kb
                s = jnp.where(col <= row, s, NEG)
                m_new = jnp.maximum(m, jnp.max(s, axis=1, keepdims=True))
                alpha = jnp.exp(m - m_new)
                p = jnp.exp(s - m_new)
                l = alpha * l + jnp.sum(p, axis=1, keepdims=True)
                acc = alpha * acc + jnp.dot(p.astype(BF16), v_ref[pl.ds(kb, MLA_TK), :],
                                            preferred_element_type=F32)
                return m_new, l, acc

            init = (jnp.full((MLA_TQ, 1), NEG, F32), jnp.zeros((MLA_TQ, 1), F32), jnp.zeros((MLA_TQ, LANES), F32))
            m, l, acc = lax.fori_loop(0, qi + 1, k_block, init)
            outs.append(acc / l)
        o_ref[pl.ds(qb, MLA_TQ), :] = jnp.where(lane_half, outs[0], outs[1]).astype(o_ref.dtype)
        return c

    lax.fori_loop(0, SEQ // MLA_TQ, q_block, 0)


def _mla_attention(q, k, v, nb):
    T = q.shape[0]
    return pl.pallas_call(
        _mla_attn_kernel,
        out_shape=jax.ShapeDtypeStruct((T, BRANCH_WIDTH), BF16),
        grid=(nb, BRANCH_WIDTH // LANES),
        in_specs=[
            pl.BlockSpec((SEQ, 2 * LANES), lambda b, hp: (b, hp)),
            pl.BlockSpec((SEQ, 2 * LANES), lambda b, hp: (b, hp)),
            pl.BlockSpec((SEQ, LANES), lambda b, hp: (b, hp)),
        ],
        out_specs=pl.BlockSpec((SEQ, LANES), lambda b, hp: (b, hp)),
        compiler_params=_cparams(2),
        name="mla_attention",
    )(q, k, v)


MERGE_TM = 1024
MERGE_TN = 256


def _merge_kernel(*refs):
    n_ref = refs[0]
    o_refs = refs[1:5]
    wg_refs = refs[5:9]
    wb_refs = refs[9:13]
    out_ref = refs[13]
    n = n_ref[...]
    acc = jnp.zeros((MERGE_TM, MERGE_TN), F32)
    for i in range(N_BRANCH):
        gate = jnp.dot(n, wg_refs[i][...].astype(BF16), preferred_element_type=F32)
        br = jnp.dot(o_refs[i][...], wb_refs[i][...].astype(BF16), preferred_element_type=F32)
        acc = acc + br * jax.nn.sigmoid(gate)
    out_ref[...] = acc.astype(out_ref.dtype)


def _merge(n, branches, w_gates, w_branch, layer):
    T, D = n.shape
    nblk = D // MERGE_TN
    in_specs = [pl.BlockSpec((MERGE_TM, D), lambda m, j: (m, 0))]
    in_specs += [pl.BlockSpec((MERGE_TM, BRANCH_WIDTH), lambda m, j: (m, 0))] * N_BRANCH
    in_specs += [pl.BlockSpec((D, MERGE_TN), lambda m, j, i=i: (0, i * nblk + j)) for i in range(N_BRANCH)]
    in_specs += [pl.BlockSpec((None, None, BRANCH_WIDTH, MERGE_TN), lambda m, j, i=i: (layer, i, 0, j))
                 for i in range(N_BRANCH)]
    return pl.pallas_call(
        _merge_kernel,
        out_shape=jax.ShapeDtypeStruct((T, D), BF16),
        grid=(T // MERGE_TM, nblk),
        in_specs=in_specs,
        out_specs=pl.BlockSpec((MERGE_TM, MERGE_TN), lambda m, j: (m, j)),
        compiler_params=_cparams(2),
        name="gated_merge",
    )(n, *branches, *([w_gates] * N_BRANCH), *([w_branch] * N_BRANCH))


def _matmul_res_kernel(a_ref, w_ref, r_ref, o_ref):
    o_ref[...] = r_ref[...] + jnp.dot(a_ref[...], w_ref[...].astype(BF16), preferred_element_type=F32)


def _matmul_res(a, w, layer, res, tm, tn):
    M, K = a.shape
    N = w.shape[2]
    return pl.pallas_call(
        _matmul_res_kernel,
        out_shape=jax.ShapeDtypeStruct((M, N), F32),
        grid=(M // tm, N // tn),
        in_specs=[pl.BlockSpec((tm, K), lambda m, j: (m, 0)),
                  pl.BlockSpec((None, K, tn), lambda m, j: (layer, 0, j)),
                  pl.BlockSpec((tm, tn), lambda m, j: (m, j))],
        out_specs=pl.BlockSpec((tm, tn), lambda m, j: (m, j)),
        compiler_params=_cparams(2),
        name="matmul_residual",
    )(a, w, res)


def _swiglu_up_kernel(x_ref, wg_ref, wu_ref, o_ref):
    x = x_ref[...]
    g = jnp.dot(x, wg_ref[...].astype(BF16), preferred_element_type=F32)
    u = jnp.dot(x, wu_ref[...].astype(BF16), preferred_element_type=F32)
    o_ref[...] = (jax.nn.silu(g) * u).astype(o_ref.dtype)


def _swiglu_up(x, wg, wu, layer, tm=1024, tn=512):
    M, K = x.shape
    N = wg.shape[2]
    return pl.pallas_call(
        _swiglu_up_kernel,
        out_shape=jax.ShapeDtypeStruct((M, N), BF16),
        grid=(M // tm, N // tn),
        in_specs=[pl.BlockSpec((tm, K), lambda m, j: (m, 0)),
                  pl.BlockSpec((None, K, tn), lambda m, j: (layer, 0, j)),
                  pl.BlockSpec((None, K, tn), lambda m, j: (layer, 0, j))],
        out_specs=pl.BlockSpec((tm, tn), lambda m, j: (m, j)),
        compiler_params=_cparams(2),
        name="swiglu_up",
    )(x, wg, wu)


ROUTER_TM = 512


def _router_kernel(h_ref, g_ref, wr_ref, idx_ref, w_ref):
    n = _rms(h_ref[...], g_ref[...])
    logits = jnp.dot(n, wr_ref[...], preferred_element_type=F32, precision=lax.Precision.HIGHEST)
    lane = lax.broadcasted_iota(jnp.int32, logits.shape, 1)
    v1 = jnp.max(logits, axis=1, keepdims=True)
    i1 = jnp.min(jnp.where(logits == v1, lane, N_EXPERTS), axis=1, keepdims=True)
    rest = jnp.where(lane == i1, -jnp.inf, logits)
    v2 = jnp.max(rest, axis=1, keepdims=True)
    i2 = jnp.min(jnp.where(rest == v2, lane, N_EXPERTS), axis=1, keepdims=True)
    e2 = jnp.exp(v2 - v1)
    den = 1.0 + e2
    two = lax.broadcasted_iota(jnp.int32, (ROUTER_TM, TOP_K), 1)
    idx_ref[...] = jnp.where(two == 0, i1, i2)
    w_ref[...] = jnp.where(two == 0, 1.0 / den, e2 / den)


def _router(h, g, w_router):
    T, D = h.shape
    return pl.pallas_call(
        _router_kernel,
        out_shape=[jax.ShapeDtypeStruct((T, TOP_K), jnp.int32), jax.ShapeDtypeStruct((T, TOP_K), F32)],
        grid=(T // ROUTER_TM,),
        in_specs=[pl.BlockSpec((ROUTER_TM, D), lambda i: (i, 0)),
                  pl.BlockSpec((1, D), lambda i: (0, 0)),
                  pl.BlockSpec((D, N_EXPERTS), lambda i: (0, 0))],
        out_specs=[pl.BlockSpec((ROUTER_TM, TOP_K), lambda i: (i, 0)),
                   pl.BlockSpec((ROUTER_TM, TOP_K), lambda i: (i, 0))],
        compiler_params=_cparams(1),
        name="moe_router",
    )(h, g.reshape(1, D), w_router)


def _row_copy(src_hbm, row, dst_vmem, slot, sem):
    return pltpu.make_async_copy(src_hbm.at[pl.ds(row, 1), :], dst_vmem.at[pl.ds(slot, 1), :], sem)


def _gather_norm_kernel(src_ref, h_hbm, g_ref, o_ref, buf, sem):
    base = pl.program_id(0) * MOE_SUB

    def start(r, c):
        _row_copy(h_hbm, src_ref[base + r], buf, r, sem).start()
        return c

    def wait(r, c):
        _row_copy(h_hbm, 0, buf, r, sem).wait()
        return c

    lax.fori_loop(0, MOE_SUB, start, 0)
    lax.fori_loop(0, MOE_SUB, wait, 0)
    o_ref[...] = _rms(buf[...], g_ref[...]).astype(o_ref.dtype)


def _gather_norm(h, g, src_rows, n_rows):
    T, D = h.shape
    return pl.pallas_call(
        _gather_norm_kernel,
        out_shape=jax.ShapeDtypeStruct((n_rows, D), BF16),
        grid_spec=pltpu.PrefetchScalarGridSpec(
            num_scalar_prefetch=1,
            grid=(n_rows // MOE_SUB,),
            in_specs=[pl.BlockSpec(memory_space=pl.ANY), pl.BlockSpec((1, D), lambda i, s: (0, 0))],
            out_specs=pl.BlockSpec((MOE_SUB, D), lambda i, s: (i, 0)),
            scratch_shapes=[pltpu.VMEM((MOE_SUB, D), F32), pltpu.SemaphoreType.DMA(())],
        ),
        compiler_params=_cparams(1),
        name="moe_gather_norm",
    )(src_rows, h, g.reshape(1, D))


def _expert_kernel(n_items_ref, item_e_ref, item_row_ref, item_sub_ref,
                   x_hbm, wg_ref, wu_ref, wd_ref, y_hbm, x_sc, acc_sc, sem_in, sem_out):
    it = pl.program_id(0)
    f = pl.program_id(1)
    n_f = pl.num_programs(1)
    live = it < n_items_ref[0]
    row0 = pl.multiple_of(item_row_ref[it], MOE_SUB)
    n_sub = item_sub_ref[it]
    item_rows = MOE_SUB * MOE_ITEM_SUBS

    def x_copy():
        return pltpu.make_async_copy(x_hbm.at[pl.ds(row0, item_rows), :], x_sc, sem_in)

    def y_copy(s):
        rows = pl.ds(s * MOE_SUB, MOE_SUB)
        return pltpu.make_async_copy(acc_sc.at[rows, :], y_hbm.at[pl.ds(row0 + s * MOE_SUB, MOE_SUB), :], sem_out)

    @pl.when(live & (f == 0))
    def _():
        cp = x_copy()
        cp.start()
        cp.wait()

    for n in range(1, MOE_ITEM_SUBS + 1):
        @pl.when(live & (n_sub == n))
        def _(n=n):
            rows = n * MOE_SUB
            x = x_sc[:rows, :]
            g = jnp.dot(x, wg_ref[...].astype(BF16), preferred_element_type=F32)
            u = jnp.dot(x, wu_ref[...].astype(BF16), preferred_element_type=F32)
            mid = (jax.nn.silu(g) * u).astype(BF16)
            y = jnp.dot(mid, wd_ref[...].astype(BF16), preferred_element_type=F32)

            @pl.when(f == 0)
            def _():
                acc_sc[:rows, :] = y

            @pl.when(f > 0)
            def _():
                acc_sc[:rows, :] += y

    @pl.when(live & (f == n_f - 1))
    def _():
        for s in range(MOE_ITEM_SUBS):
            @pl.when(s < n_sub)
            def _(s=s):
                y_copy(s).start()
        for s in range(MOE_ITEM_SUBS):
            @pl.when(s < n_sub)
            def _(s=s):
                y_copy(s).wait()

    @pl.when((it == pl.num_programs(0) - 1) & (f == n_f - 1))
    def _():
        acc_sc[:MOE_SUB, :] = jnp.zeros((MOE_SUB, acc_sc.shape[1]), F32)
        first = n_items_ref[1] // MOE_SUB
        n_blocks = y_hbm.shape[0] // MOE_SUB

        def fill(s):
            dst = y_hbm.at[pl.ds(pl.multiple_of(s * MOE_SUB, MOE_SUB), MOE_SUB), :]
            return pltpu.make_async_copy(acc_sc.at[pl.ds(0, MOE_SUB), :], dst, sem_out)

        lax.fori_loop(first, n_blocks, lambda s, c: (fill(s).start(), c)[1], 0)
        lax.fori_loop(first, n_blocks, lambda s, c: (fill(s).wait(), c)[1], 0)


def _expert_ffn(xs, w_gate, w_up, w_down, n_items, item_e, item_row, item_sub, max_rows, max_items):
    D = xs.shape[1]
    FF = w_gate.shape[3]
    n_f = FF // MOE_TF

    def f_of(it, f, n_items_ref):
        return jnp.where(it < n_items_ref[0], f, n_f - 1)

    return pl.pallas_call(
        _expert_kernel,
        out_shape=jax.ShapeDtypeStruct((max_rows, D), F32),
        grid_spec=pltpu.PrefetchScalarGridSpec(
            num_scalar_prefetch=4,
            grid=(max_items, n_f),
            in_specs=[
                pl.BlockSpec(memory_space=pl.ANY),
                pl.BlockSpec((None, None, D, MOE_TF), lambda it, f, n, e, r, s: (0, e[it], 0, f_of(it, f, n))),
                pl.BlockSpec((None, None, D, MOE_TF), lambda it, f, n, e, r, s: (0, e[it], 0, f_of(it, f, n))),
                pl.BlockSpec((None, None, MOE_TF, D), lambda it, f, n, e, r, s: (0, e[it], f_of(it, f, n), 0)),
            ],
            out_specs=pl.BlockSpec(memory_space=pl.ANY),
            scratch_shapes=[pltpu.VMEM((MOE_SUB * MOE_ITEM_SUBS, D), BF16),
                            pltpu.VMEM((MOE_SUB * MOE_ITEM_SUBS, D), F32),
                            pltpu.SemaphoreType.DMA(()), pltpu.SemaphoreType.DMA(())],
        ),
        compiler_params=_cparams(2),
        name="moe_expert_ffn",
    )(n_items, item_e, item_row, item_sub, xs, w_gate, w_up, w_down)


COMBINE_TM = 256


def _combine_kernel(pos_ref, h_ref, w_ref, g_ref, y_hbm, o_ref, buf, sem):
    base = pl.program_id(0) * COMBINE_TM

    def start(r, c):
        for k in range(TOP_K):
            _row_copy(y_hbm, pos_ref[(base + r) * TOP_K + k], buf.at[k], r, sem).start()
        return c

    def wait(r, c):
        for k in range(TOP_K):
            _row_copy(y_hbm, 0, buf.at[k], r, sem).wait()
        return c

    lax.fori_loop(0, COMBINE_TM, start, 0)
    lax.fori_loop(0, COMBINE_TM, wait, 0)
    w = w_ref[...]
    out = h_ref[...] + w[:, 0:1] * buf[0] + w[:, 1:2] * buf[1]
    o_ref[...] = _rms(out, g_ref[...])


def _combine_norm(h, ys, pos, top_w, g):
    T, D = h.shape
    return pl.pallas_call(
        _combine_kernel,
        out_shape=jax.ShapeDtypeStruct((T, D), F32),
        grid_spec=pltpu.PrefetchScalarGridSpec(
            num_scalar_prefetch=1,
            grid=(T // COMBINE_TM,),
            in_specs=[pl.BlockSpec((COMBINE_TM, D), lambda i, p: (i, 0)),
                      pl.BlockSpec((COMBINE_TM, TOP_K), lambda i, p: (i, 0)),
                      pl.BlockSpec((1, D), lambda i, p: (0, 0)),
                      pl.BlockSpec(memory_space=pl.ANY)],
            out_specs=pl.BlockSpec((COMBINE_TM, D), lambda i, p: (i, 0)),
            scratch_shapes=[pltpu.VMEM((TOP_K, COMBINE_TM, D), F32), pltpu.SemaphoreType.DMA(())],
        ),
        compiler_params=_cparams(1),
        name="moe_combine_norm",
    )(pos.reshape(-1), h, top_w, g.reshape(1, D), ys)


def _moe_plan(top_idx):
    T = top_idx.shape[0]
    n_assign = T * TOP_K
    item_rows = MOE_SUB * MOE_ITEM_SUBS
    max_rows = n_assign + N_EXPERTS * MOE_SUB
    max_items = n_assign // item_rows + N_EXPERTS
    e_flat = top_idx.reshape(-1)
    onehot = (e_flat[:, None] == jnp.arange(N_EXPERTS)[None, :]).astype(jnp.int32)
    ranks = jnp.cumsum(onehot, axis=0) - onehot
    counts = jnp.sum(onehot, axis=0)
    padded = ((counts + MOE_SUB - 1) // MOE_SUB) * MOE_SUB
    group_start = jnp.cumsum(padded) - padded
    rank = jnp.sum(ranks * onehot, axis=1)
    pos = group_start[e_flat] + rank
    src_rows = jnp.zeros((max_rows + item_rows,), jnp.int32).at[pos].set(jnp.arange(n_assign, dtype=jnp.int32) // TOP_K)
    items_per_e = (padded + item_rows - 1) // item_rows
    item_start = jnp.cumsum(items_per_e) - items_per_e
    n_items = jnp.sum(items_per_e)
    it = jnp.arange(max_items)
    item_e = jnp.minimum(jnp.sum(it[:, None] >= (item_start + items_per_e)[None, :], axis=1), N_EXPERTS - 1)
    local = it - item_start[item_e]
    item_row = group_start[item_e] + local * item_rows
    item_sub = jnp.clip((padded[item_e] - local * item_rows) // MOE_SUB, 0, MOE_ITEM_SUBS)
    live = it < n_items
    last = jnp.maximum(n_items - 1, 0)
    item_e = jnp.where(live, item_e, item_e[last])
    item_row = jnp.where(live, item_row, 0)
    item_sub = jnp.where(live, item_sub, 0)
    i32 = lambda a: a.astype(jnp.int32)
    plan = jnp.stack([n_items, jnp.sum(padded)])
    return (i32(pos.reshape(T, TOP_K)), src_rows, i32(plan), i32(item_e), i32(item_row),
            i32(item_sub), max_rows, max_items)


def _moe_and_final_norm(h, g_ffn, w_router, w_gate, w_up, w_down, g_final):
    top_idx, top_w = _router(h, g_ffn, w_router)
    pos, src_rows, plan, item_e, item_row, item_sub, max_rows, max_items = _moe_plan(top_idx)
    xs = _gather_norm(h, g_ffn, src_rows, src_rows.shape[0])
    ys = _expert_ffn(xs, w_gate, w_up, w_down, plan, item_e, item_row, item_sub, max_rows, max_items)
    return _combine_norm(h, ys, pos, top_w, g_final)


def kernel(x, w_in, w_branch, w_out, norm_mix, norm_ffn, norm_final, sinks, mla_q_norm, mla_kv_norm, mla_w_uq,
           mla_w_ukv, ffn_w_gate, ffn_w_up, ffn_w_down, router_w, moe_w_gate, moe_w_up, moe_w_down):
    nb, seq, d = x.shape
    assert (seq, d) == (SEQ, D_MODEL)
    depth = w_in.shape[0]
    assert depth == 2, "layer 0 uses the dense FFN, layer 1 the MoE followed by the final norm"
    cos64, sin64 = _rope_tables(HEAD_DIM, 0, HEAD_DIM)
    cos32, sin32 = _rope_tables(D_ROPE, D_NOPE, LANES)
    h = x.reshape(nb * seq, d)
    out = None
    for layer in range(depth):
        n = _rmsnorm(h, norm_mix[layer], BF16)
        proj = _in_proj(n, w_in, layer, cos64, sin64)
        o_a = _dilated_attention(proj, nb)
        o_b = _sink_attention(proj, sinks[layer], nb)
        o_c = _stick_attention(proj, nb)
        q_d, k_d, v_d = _mla_prep(proj, mla_q_norm[layer], mla_kv_norm[layer], mla_w_uq[layer], mla_w_ukv[layer],
                                  cos32, sin32)
        o_d = _mla_attention(q_d, k_d, v_d, nb)
        w_gates = w_in[layer, :, GATE_COL0:]
        merged = _merge(n, (o_a, o_b, o_c, o_d), w_gates, w_branch, layer)
        h = _matmul_res(merged, w_out, layer, h, 1024, 512)
        if layer == 0:
            n2 = _rmsnorm(h, norm_ffn[layer], BF16)
            mid = _swiglu_up(n2, ffn_w_gate, ffn_w_up, 0)
            h = _matmul_res(mid, ffn_w_down, 0, h, 512, 512)
        else:
            out = _moe_and_final_norm(h, norm_ffn[layer], router_w[0], moe_w_gate, moe_w_up, moe_w_down, norm_final)
    return out.reshape(nb, seq, d)
```

```python
import functools

import numpy as np
import jax
import jax.numpy as jnp
from jax import lax
from jax.experimental import pallas as pl
from jax.experimental.pallas import tpu as pltpu

F32 = jnp.float32
BF16 = jnp.bfloat16

D_MODEL = 2048
SEQ = 2048
HEAD_DIM = 64
BLOCK = 128
LANES = 128
ROPE_THETA = 10000.0
NORM_EPS = 1e-6
DIL_PATTERNS = ((128, 1), (512, 4), (2048, 16))
B_WINDOW = 128
D_NOPE = 64
D_ROPE = 32
Q_LORA = 384
KV_LORA = 256
N_BRANCH = 4
BRANCH_WIDTH = 512
N_EXPERTS = 8
TOP_K = 2

A_Q0, A_K0, A_V0 = 0, 12, 24
B_Q0, B_K0, B_V0 = 36, 40, 41
C_Q0, C_K0, C_V0 = 42, 46, 50
MLA_COL0 = 6912
MLA_BLOCK_W = 768
QKV_WIDTH = 7680
GATE_COL0 = 7584
PROJ_TN = 256

VMEM_LIMIT = 56 * 1024 * 1024
NEG = -1e30
LOG2_E = 1.4426950408889634

MOE_SUB = 256
MOE_ITEM_SUBS = 4
MOE_TF = 256


def _cparams(n_axes):
    return pltpu.CompilerParams(dimension_semantics=("arbitrary",) * n_axes, vmem_limit_bytes=VMEM_LIMIT)


def _lane_half():
    return lax.broadcasted_iota(jnp.int32, (BLOCK, LANES), 1) < HEAD_DIM


def _rms(x, g):
    return x * lax.rsqrt(jnp.mean(x * x, axis=-1, keepdims=True) + NORM_EPS) * g


def _rmsnorm_kernel(x_ref, g_ref, o_ref):
    o_ref[...] = _rms(x_ref[...], g_ref[...]).astype(o_ref.dtype)


def _rmsnorm(x, g, out_dtype, tm=512):
    T, D = x.shape
    return pl.pallas_call(
        _rmsnorm_kernel,
        out_shape=jax.ShapeDtypeStruct((T, D), out_dtype),
        grid=(T // tm,),
        in_specs=[pl.BlockSpec((tm, D), lambda i: (i, 0)), pl.BlockSpec((1, D), lambda i: (0, 0))],
        out_specs=pl.BlockSpec((tm, D), lambda i: (i, 0)),
        compiler_params=_cparams(1),
        name="rmsnorm",
    )(x, g.reshape(1, D))


def _rope_tables(dh, lane0, period):
    half = dh // 2
    freqs = ROPE_THETA ** (-2.0 * jnp.arange(half, dtype=F32) / dh)
    ang = jnp.arange(SEQ, dtype=F32)[:, None] * freqs[None, :]
    cos, sin = jnp.cos(ang), jnp.sin(ang)
    cos_h = jnp.concatenate([cos, cos], axis=1)
    sin_h = jnp.concatenate([-sin, sin], axis=1)
    ones = jnp.ones((SEQ, period - dh), F32)
    cos_p = jnp.concatenate([cos_h, ones], axis=1)
    sin_p = jnp.concatenate([sin_h, 0.0 * ones], axis=1)
    reps = LANES // period
    cos_t = jnp.roll(jnp.tile(cos_p, (1, reps)), lane0, axis=1)
    sin_t = jnp.roll(jnp.tile(sin_p, (1, reps)), lane0, axis=1)
    return cos_t, sin_t


def _rope_lanes(x, cos, sin, half, first_half_mask):
    fwd = pltpu.roll(x, LANES - half, 1)
    bwd = pltpu.roll(x, half, 1)
    return x * cos + jnp.where(first_half_mask, fwd, bwd) * sin


def _proj_kernel(x_ref, w_ref, cos_a, sin_a, cos_b, sin_b, o_ref):
    acc = jnp.dot(x_ref[...], w_ref[...].astype(BF16), preferred_element_type=F32)
    lane = lax.broadcasted_iota(jnp.int32, (SEQ, LANES), 1)
    first = (lane % HEAD_DIM) < HEAD_DIM // 2
    o_ref[:, :LANES] = _rope_lanes(acc[:, :LANES], cos_a[...], sin_a[...], HEAD_DIM // 2, first)
    o_ref[:, LANES:] = _rope_lanes(acc[:, LANES:], cos_b[...], sin_b[...], HEAD_DIM // 2, first)


def _in_proj(n, w_in, layer, cos, sin):
    T, D = n.shape
    nb = T // SEQ
    tables = (jnp.stack([cos, jnp.ones_like(cos)]), jnp.stack([sin, jnp.zeros_like(sin)]))

    def variant(j, half):
        rope_both = (j < 12) | ((j >= 18) & (j < 20))
        rope = rope_both | (j == 20) if half == 0 else rope_both
        return jnp.where(rope, 0, 1)

    table_specs = [pl.BlockSpec((None, SEQ, LANES), lambda b, j, half=half: (variant(j, half), 0, 0))
                   for half in (0, 0, 1, 1)]
    return pl.pallas_call(
        _proj_kernel,
        out_shape=jax.ShapeDtypeStruct((T, QKV_WIDTH), F32),
        grid=(nb, QKV_WIDTH // PROJ_TN),
        in_specs=[
            pl.BlockSpec((SEQ, D), lambda b, j: (b, 0)),
            pl.BlockSpec((None, D, PROJ_TN), lambda b, j: (layer, 0, j)),
        ] + table_specs,
        out_specs=pl.BlockSpec((SEQ, PROJ_TN), lambda b, j: (b, j)),
        compiler_params=_cparams(2),
        name="in_proj_rope",
    )(n, w_in, tables[0], tables[1], tables[0], tables[1])


BAND_UNROLL = 8


def _band_scores(q, kcat, vcat, head, has_prev, strict_prev):
    half = _lane_half()
    qh = jnp.where(half if head == 0 else jnp.logical_not(half), q * (HEAD_DIM ** -0.5), 0.0).astype(BF16)
    s = lax.dot_general(qh, kcat, (((1,), (1,)), ((), ())), preferred_element_type=F32)
    row = lax.broadcasted_iota(jnp.int32, (BLOCK, 2 * BLOCK), 0)
    col = lax.broadcasted_iota(jnp.int32, (BLOCK, 2 * BLOCK), 1)
    dist = row + BLOCK - col
    max_dist = BLOCK - 1 if strict_prev else BLOCK
    first_col = jnp.where(has_prev, 0, BLOCK)
    valid = (dist >= 0) & (dist <= max_dist) & (col >= first_col)
    s = jnp.where(valid, s, NEG)
    m = jnp.max(s, axis=1, keepdims=True)
    p = jnp.exp(s - m)
    l = jnp.sum(p, axis=1, keepdims=True)
    acc = jnp.dot(p.astype(BF16), vcat, preferred_element_type=F32)
    return acc, m, l


def _pair(a0, a1):
    return jnp.where(_lane_half(), a0, a1)


def _dilated_kernel(*refs):
    q_refs, k_refs, v_refs = refs[0:3], refs[3:6], refs[6:9]
    o_ref = refs[9]
    acc_sc, m_sc, l_sc = refs[10:13]
    n_blocks = SEQ // BLOCK

    for g, (win, dil) in enumerate(DIL_PATTERNS):
        assert win // dil == BLOCK and SEQ % (dil * BLOCK) == 0
        q_ref, k_ref, v_ref = q_refs[g], k_refs[g], v_refs[g]

        def rows(start, dil=dil):
            return pl.ds(start, BLOCK) if dil == 1 else pl.ds(start, BLOCK, stride=dil)

        def block(it, g=g, dil=dil, q_ref=q_ref, k_ref=k_ref, v_ref=v_ref, rows=rows):
            r = it % dil
            i = it // dil
            base = i * (BLOCK * dil) + r
            prev = jnp.maximum(base - BLOCK * dil, 0)
            q = q_ref[rows(base), :]
            kcat = jnp.concatenate([k_ref[rows(prev), :], k_ref[rows(base), :]], axis=0).astype(BF16)
            vcat = jnp.concatenate([v_ref[rows(prev), :], v_ref[rows(base), :]], axis=0).astype(BF16)
            a0, m0, l0 = _band_scores(q, kcat, vcat, 0, i > 0, False)
            a1, m1, l1 = _band_scores(q, kcat, vcat, 1, i > 0, False)
            acc_sc[g, rows(base), :] = _pair(a0, a1)
            m_sc[g, rows(base), :] = _pair(m0, m1)
            l_sc[g, rows(base), :] = _pair(l0, l1)

        def body(it, carry, block=block):
            for u in range(BAND_UNROLL):
                block(it + u * (n_blocks // BAND_UNROLL))
            return carry

        lax.fori_loop(0, n_blocks // BAND_UNROLL, body, 0)

    def combine(i, carry):
        rows = pl.ds(pl.multiple_of(i * BLOCK, BLOCK), BLOCK)
        m = [m_sc[g, rows, :] for g in range(len(DIL_PATTERNS))]
        m_all = jnp.maximum(jnp.maximum(m[0], m[1]), m[2])
        num = jnp.zeros((BLOCK, LANES), F32)
        den = jnp.zeros((BLOCK, LANES), F32)
        for g in range(len(DIL_PATTERNS)):
            a = jnp.exp(m[g] - m_all)
            num = num + a * acc_sc[g, rows, :]
            den = den + a * l_sc[g, rows, :]
        o_ref[rows, :] = (num / den).astype(o_ref.dtype)
        return carry

    lax.fori_loop(0, n_blocks, combine, 0)


def _dilated_attention(proj, nb):
    T = proj.shape[0]
    specs = []
    for col0 in (A_Q0, A_K0, A_V0):
        for g in range(len(DIL_PATTERNS)):
            specs.append(pl.BlockSpec((SEQ, LANES), lambda b, hp, c=col0 + 4 * g: (b, c + hp)))
    return pl.pallas_call(
        _dilated_kernel,
        out_shape=jax.ShapeDtypeStruct((T, BRANCH_WIDTH), BF16),
        grid=(nb, BRANCH_WIDTH // LANES),
        in_specs=specs,
        out_specs=pl.BlockSpec((SEQ, LANES), lambda b, hp: (b, hp)),
        scratch_shapes=[pltpu.VMEM((len(DIL_PATTERNS), SEQ, LANES), F32)] * 3,
        compiler_params=_cparams(2),
        name="dilated_attention",
    )(*([proj] * 9))


def _sink_kernel(q_ref, k_ref, v_ref, sink_ref, o_ref, k_sc, v_sc):
    hp = pl.program_id(1)
    kv_head = hp // 2
    k = k_ref[...]
    v = v_ref[...]
    k_sw = pltpu.roll(k, HEAD_DIM, 1)
    v_sw = pltpu.roll(v, HEAD_DIM, 1)
    first = kv_head == 0
    k_sc[0] = jnp.where(first, k, k_sw).astype(BF16)
    k_sc[1] = jnp.where(first, k_sw, k).astype(BF16)
    v_sc[0] = jnp.where(first, v, v_sw).astype(BF16)
    v_sc[1] = jnp.where(first, v_sw, v).astype(BF16)
    sink = _pair(sink_ref[pl.ds(2 * hp, 1), :], sink_ref[pl.ds(2 * hp + 1, 1), :])

    def block(i):
        base = pl.multiple_of(i * BLOCK, BLOCK)
        prev = pl.multiple_of(jnp.maximum(base - BLOCK, 0), BLOCK)
        q = q_ref[pl.ds(base, BLOCK), :]
        out = []
        for a in range(2):
            kcat = jnp.concatenate([k_sc[a, pl.ds(prev, BLOCK), :], k_sc[a, pl.ds(base, BLOCK), :]], axis=0)
            vcat = jnp.concatenate([v_sc[a, pl.ds(prev, BLOCK), :], v_sc[a, pl.ds(base, BLOCK), :]], axis=0)
            out.append(_band_scores(q, kcat, vcat, a, i > 0, True))
        acc = _pair(out[0][0], out[1][0])
        m = _pair(out[0][1], out[1][1])
        l = _pair(out[0][2], out[1][2])
        m_new = jnp.maximum(m, sink)
        scale = jnp.exp(m - m_new)
        den = l * scale + jnp.exp(sink - m_new)
        o_ref[pl.ds(base, BLOCK), :] = (acc * scale / den).astype(o_ref.dtype)

    n_blocks = SEQ // BLOCK

    def body(it, carry):
        for u in range(BAND_UNROLL):
            block(it + u * (n_blocks // BAND_UNROLL))
        return carry

    lax.fori_loop(0, n_blocks // BAND_UNROLL, body, 0)


def _sink_attention(proj, sinks_l, nb):
    T = proj.shape[0]
    sink_tab = jnp.broadcast_to(sinks_l.astype(F32)[:, None], (8, LANES))
    return pl.pallas_call(
        _sink_kernel,
        out_shape=jax.ShapeDtypeStruct((T, BRANCH_WIDTH), BF16),
        grid=(nb, BRANCH_WIDTH // LANES),
        in_specs=[
            pl.BlockSpec((SEQ, LANES), lambda b, hp: (b, B_Q0 + hp)),
            pl.BlockSpec((SEQ, LANES), lambda b, hp: (b, B_K0)),
            pl.BlockSpec((SEQ, LANES), lambda b, hp: (b, B_V0)),
            pl.BlockSpec((8, LANES), lambda b, hp: (0, 0)),
        ],
        out_specs=pl.BlockSpec((SEQ, LANES), lambda b, hp: (b, hp)),
        scratch_shapes=[pltpu.VMEM((2, SEQ, LANES), BF16)] * 2,
        compiler_params=_cparams(2),
        name="sink_attention",
    )(proj, proj, proj, sink_tab)


STICK_TQ = 1024


def _stick_kernel(q_ref, k_ref, v_ref, tri_ref, o_ref, q_sc, k_sc, v_sc, carry_sc, acc_sc):
    half = lax.broadcasted_iota(jnp.int32, (SEQ, LANES), 1) < HEAD_DIM
    q = q_ref[...] * (HEAD_DIM ** -0.5)
    q_sc[0] = jnp.where(half, q, 0.0).astype(BF16)
    q_sc[1] = jnp.where(half, 0.0, q).astype(BF16)
    k_sc[...] = k_ref[...].astype(BF16)
    v_sc[...] = v_ref[...].astype(BF16)
    carry_sc[...] = jnp.zeros_like(carry_sc)
    acc_sc[...] = jnp.zeros_like(acc_sc)
    tri = tri_ref[...]
    n_key_blocks = SEQ // BLOCK
    n_q_tiles = SEQ // STICK_TQ
    lane_half = lax.broadcasted_iota(jnp.int32, (STICK_TQ, LANES), 1) < HEAD_DIM

    def tile(qi, j, overlaps):
        qb = pl.multiple_of(qi * STICK_TQ, STICK_TQ)
        kb = pl.multiple_of(j * BLOCK, BLOCK)
        kj = k_sc[pl.ds(kb, BLOCK), :]
        vj = v_sc[pl.ds(kb, BLOCK), :]
        outs = []
        for a in range(2):
            z = lax.dot_general(q_sc[a, pl.ds(qb, STICK_TQ), :], kj, (((1,), (1,)), ((), ())),
                                preferred_element_type=F32)
            sp = jnp.maximum(z, 0.0) + jnp.log(1.0 + jnp.exp(-jnp.abs(z)))
            if overlaps:
                row = lax.broadcasted_iota(jnp.int32, (STICK_TQ, BLOCK), 0)
                col = lax.broadcasted_iota(jnp.int32, (STICK_TQ, BLOCK), 1)
                before = col + (kb - qb) < row
                sp = jnp.where(before, sp, 0.0)
            hi = sp.astype(BF16)
            lo = (sp - hi.astype(F32)).astype(BF16)
            sums = jnp.dot(jnp.concatenate([hi, lo], axis=1), tri, preferred_element_type=F32)
            carry = carry_sc[a, pl.ds(qb, STICK_TQ), :]
            w = jnp.exp(z + carry + sums[:, :BLOCK])
            if overlaps:
                w = jnp.where(before, w, 0.0)
            carry_sc[a, pl.ds(qb, STICK_TQ), :] = carry + sums[:, BLOCK:]
            outs.append(jnp.dot(w.astype(BF16), vj, preferred_element_type=F32))
        acc_sc[pl.ds(qb, STICK_TQ), :] += jnp.where(lane_half, outs[0], outs[1])

    def key_block(jj, c):
        j = n_key_blocks - 1 - jj
        first = j // (STICK_TQ // BLOCK)
        tile(first, j, True)

        def q_tile(qi, c2):
            tile(qi, j, False)
            return c2

        lax.fori_loop(first + 1, n_q_tiles, q_tile, 0)
        return c

    lax.fori_loop(0, n_key_blocks, key_block, 0)
    o_ref[...] = acc_sc[...].astype(o_ref.dtype)


def _stick_attention(proj, nb):
    T = proj.shape[0]
    jp = np.arange(BLOCK)[:, None]
    s = np.arange(BLOCK)[None, :]
    tri = -np.concatenate([(jp >= s), np.ones((BLOCK, BLOCK), bool)], axis=1).astype(np.float32)
    tri = jnp.asarray(np.concatenate([tri, tri], axis=0), BF16)
    return pl.pallas_call(
        _stick_kernel,
        out_shape=jax.ShapeDtypeStruct((T, BRANCH_WIDTH), BF16),
        grid=(nb, BRANCH_WIDTH // LANES),
        in_specs=[
            pl.BlockSpec((SEQ, LANES), lambda b, hp: (b, C_Q0 + hp)),
            pl.BlockSpec((SEQ, LANES), lambda b, hp: (b, C_K0 + hp)),
            pl.BlockSpec((SEQ, LANES), lambda b, hp: (b, C_V0 + hp)),
            pl.BlockSpec((2 * BLOCK, 2 * BLOCK), lambda b, hp: (0, 0)),
        ],
        out_specs=pl.BlockSpec((SEQ, LANES), lambda b, hp: (b, hp)),
        scratch_shapes=[
            pltpu.VMEM((2, SEQ, LANES), BF16),
            pltpu.VMEM((SEQ, LANES), BF16),
            pltpu.VMEM((SEQ, LANES), BF16),
            pltpu.VMEM((2, SEQ, LANES), F32),
            pltpu.VMEM((SEQ, LANES), F32),
        ],
        compiler_params=_cparams(2),
        name="stick_breaking_attention",
    )(proj, proj, proj, tri)


MLA_HEADS = 8
MLA_TM = 512


def _mla_prep_kernel(c_ref, gq_ref, gkv_ref, wq_ref, wk_ref, wv_ref, cos_ref, sin_ref, q_ref, k_ref, v_ref):
    c = c_ref[...]
    cq = _rms(c[:, :Q_LORA], gq_ref[...]).astype(BF16)
    ckv = _rms(c[:, Q_LORA:Q_LORA + KV_LORA], gkv_ref[...]).astype(BF16)
    q = jnp.dot(cq, wq_ref[...].astype(BF16), preferred_element_type=F32)
    k = jnp.dot(ckv, wk_ref[...].astype(BF16), preferred_element_type=F32)
    v_ref[...] = jnp.dot(ckv, wv_ref[...].astype(BF16), preferred_element_type=F32).astype(BF16)
    cos, sin = cos_ref[...], sin_ref[...]
    lane = lax.broadcasted_iota(jnp.int32, (MLA_TM, LANES), 1)
    rope_lane = (lane >= D_NOPE) & (lane < D_NOPE + D_ROPE)
    first = lane < D_NOPE + D_ROPE // 2
    kr = pltpu.roll(c[:, Q_LORA + KV_LORA:], D_NOPE, 1)
    kr = jnp.where(rope_lane, _rope_lanes(kr, cos, sin, D_ROPE // 2, first), 0.0)
    scale = (D_NOPE + D_ROPE) ** -0.5 * LOG2_E
    for h in range(MLA_HEADS):
        sl = slice(h * LANES, (h + 1) * LANES)
        q_ref[:, sl] = (_rope_lanes(q[:, sl], cos, sin, D_ROPE // 2, first) * scale).astype(BF16)
        k_ref[:, sl] = (k[:, sl] + kr).astype(BF16)


def _mla_prep(proj, g_qa, g_kva, w_uq, w_ukv, cos, sin):
    T = proj.shape[0]
    wq = jnp.pad(w_uq.reshape(Q_LORA, MLA_HEADS, D_NOPE + D_ROPE), ((0, 0), (0, 0), (0, LANES - D_NOPE - D_ROPE)))
    wq = wq.reshape(Q_LORA, MLA_HEADS * LANES)
    wkv = w_ukv.reshape(KV_LORA, MLA_HEADS, 2 * HEAD_DIM)
    wk = jnp.pad(wkv[:, :, :D_NOPE], ((0, 0), (0, 0), (0, LANES - D_NOPE))).reshape(KV_LORA, MLA_HEADS * LANES)
    wv = wkv[:, :, D_NOPE:].reshape(KV_LORA, MLA_HEADS * HEAD_DIM)
    full = lambda shape: pl.BlockSpec(shape, lambda i: (0, 0))
    per_seq = SEQ // MLA_TM
    return pl.pallas_call(
        _mla_prep_kernel,
        out_shape=[jax.ShapeDtypeStruct((T, MLA_HEADS * LANES), BF16),
                   jax.ShapeDtypeStruct((T, MLA_HEADS * LANES), BF16),
                   jax.ShapeDtypeStruct((T, MLA_HEADS * HEAD_DIM), BF16)],
        grid=(T // MLA_TM,),
        in_specs=[
            pl.BlockSpec((MLA_TM, MLA_BLOCK_W), lambda i: (i, MLA_COL0 // MLA_BLOCK_W)),
            full((1, Q_LORA)), full((1, KV_LORA)),
            full(wq.shape), full(wk.shape), full(wv.shape),
            pl.BlockSpec((MLA_TM, LANES), lambda i: (i % per_seq, 0)),
            pl.BlockSpec((MLA_TM, LANES), lambda i: (i % per_seq, 0)),
        ],
        out_specs=[pl.BlockSpec((MLA_TM, MLA_HEADS * LANES), lambda i: (i, 0)),
                   pl.BlockSpec((MLA_TM, MLA_HEADS * LANES), lambda i: (i, 0)),
                   pl.BlockSpec((MLA_TM, MLA_HEADS * HEAD_DIM), lambda i: (i, 0))],
        compiler_params=_cparams(1),
        name="mla_prep",
    )(proj, g_qa.reshape(1, Q_LORA), g_kva.reshape(1, KV_LORA), wq, wk, wv, cos, sin)


MLA_TQ = 1024
MLA_TK = 256


def _mla_attn_kernel(q_ref, k_ref, v_ref, o_ref, v_sc, m_sc, acc_sc):
    half = lax.broadcasted_iota(jnp.int32, (SEQ, LANES), 1) < HEAD_DIM
    v = v_ref[...]
    one = jnp.ones_like(v)
    v_sc[0] = jnp.where(half, v, one)
    v_sc[1] = jnp.where(half, one, v)
    lane_half = lax.broadcasted_iota(jnp.int32, (MLA_TQ, LANES), 1) < HEAD_DIM

    def step(qb, kb, diagonal):
        for a in range(2):
            q = q_ref[pl.ds(qb, MLA_TQ), a * LANES:(a + 1) * LANES]
            k = k_ref[pl.ds(kb, MLA_TK), a * LANES:(a + 1) * LANES]
            s = lax.dot_general(q, k, (((1,), (1,)), ((), ())), preferred_element_type=F32)
            if diagonal:
                row = lax.broadcasted_iota(jnp.int32, (MLA_TQ, MLA_TK), 0)
                col = lax.broadcasted_iota(jnp.int32, (MLA_TQ, MLA_TK), 1)
                s = jnp.where(col + (kb - qb) <= row, s, NEG)
            m = m_sc[a]
            m_new = jnp.maximum(m, jnp.max(s, axis=1, keepdims=True))
            alpha = jnp.exp2(m - m_new)
            p = jnp.exp2(s - jnp.concatenate([m_new] * (MLA_TK // LANES), axis=1))
            pv = jnp.dot(p.astype(BF16), v_sc[a, pl.ds(kb, MLA_TK), :], preferred_element_type=F32)
            acc_sc[a] = alpha * acc_sc[a] + pv
            m_sc[a] = m_new

    def q_block(qi, c):
        qb = pl.multiple_of(qi * MLA_TQ, MLA_TQ)
        m_sc[...] = jnp.full(m_sc.shape, NEG, F32)
        acc_sc[...] = jnp.zeros(acc_sc.shape, F32)

        def k_block(kj, c2):
            step(qb, pl.multiple_of(kj * MLA_TK, MLA_TK), False)
            return c2

        lax.fori_loop(0, qi * (MLA_TQ // MLA_TK), k_block, 0)
        for d in range(MLA_TQ // MLA_TK):
            step(qb, qb + d * MLA_TK, True)
        outs = [acc_sc[a] / pltpu.roll(acc_sc[a], HEAD_DIM, 1) for a in range(2)]
        o_ref[pl.ds(qb, MLA_TQ), :] = jnp.where(lane_half, outs[0], outs[1]).astype(o_ref.dtype)
        return c

    lax.fori_loop(0, SEQ // MLA_TQ, q_block, 0)


def _mla_attention(q, k, v, nb):
    T = q.shape[0]
    return pl.pallas_call(
        _mla_attn_kernel,
        out_shape=jax.ShapeDtypeStruct((T, BRANCH_WIDTH), BF16),
        grid=(nb, BRANCH_WIDTH // LANES),
        in_specs=[
            pl.BlockSpec((SEQ, 2 * LANES), lambda b, hp: (b, hp)),
            pl.BlockSpec((SEQ, 2 * LANES), lambda b, hp: (b, hp)),
            pl.BlockSpec((SEQ, LANES), lambda b, hp: (b, hp)),
        ],
        out_specs=pl.BlockSpec((SEQ, LANES), lambda b, hp: (b, hp)),
        scratch_shapes=[pltpu.VMEM((2, SEQ, LANES), BF16),
                        pltpu.VMEM((2, MLA_TQ, LANES), F32),
                        pltpu.VMEM((2, MLA_TQ, LANES), F32)],
        compiler_params=_cparams(2),
        name="mla_attention",
    )(q, k, v)


MERGE_TM = 1024
MERGE_TN = 256


def _merge_kernel(*refs):
    n_ref = refs[0]
    o_refs = refs[1:5]
    wg_refs = refs[5:9]
    wb_refs = refs[9:13]
    out_ref = refs[13]
    n = n_ref[...]
    acc = jnp.zeros((MERGE_TM, MERGE_TN), F32)
    for i in range(N_BRANCH):
        gate = jnp.dot(n, wg_refs[i][...].astype(BF16), preferred_element_type=F32)
        br = jnp.dot(o_refs[i][...], wb_refs[i][...].astype(BF16), preferred_element_type=F32)
        acc = acc + br * jax.nn.sigmoid(gate)
    out_ref[...] = acc.astype(out_ref.dtype)


def _merge(n, branches, w_gates, w_branch, layer):
    T, D = n.shape
    nblk = D // MERGE_TN
    in_specs = [pl.BlockSpec((MERGE_TM, D), lambda m, j: (m, 0))]
    in_specs += [pl.BlockSpec((MERGE_TM, BRANCH_WIDTH), lambda m, j: (m, 0))] * N_BRANCH
    in_specs += [pl.BlockSpec((D, MERGE_TN), lambda m, j, i=i: (0, i * nblk + j)) for i in range(N_BRANCH)]
    in_specs += [pl.BlockSpec((None, None, BRANCH_WIDTH, MERGE_TN), lambda m, j, i=i: (layer, i, 0, j))
                 for i in range(N_BRANCH)]
    return pl.pallas_call(
        _merge_kernel,
        out_shape=jax.ShapeDtypeStruct((T, D), BF16),
        grid=(T // MERGE_TM, nblk),
        in_specs=in_specs,
        out_specs=pl.BlockSpec((MERGE_TM, MERGE_TN), lambda m, j: (m, j)),
        compiler_params=_cparams(2),
        name="gated_merge",
    )(n, *branches, *([w_gates] * N_BRANCH), *([w_branch] * N_BRANCH))


def _matmul_res_kernel(a_ref, w_ref, r_ref, o_ref):
    o_ref[...] = r_ref[...] + jnp.dot(a_ref[...], w_ref[...].astype(BF16), preferred_element_type=F32)


def _matmul_res(a, w, layer, res, tm, tn):
    M, K = a.shape
    N = w.shape[2]
    return pl.pallas_call(
        _matmul_res_kernel,
        out_shape=jax.ShapeDtypeStruct((M, N), F32),
        grid=(M // tm, N // tn),
        in_specs=[pl.BlockSpec((tm, K), lambda m, j: (m, 0)),
                  pl.BlockSpec((None, K, tn), lambda m, j: (layer, 0, j)),
                  pl.BlockSpec((tm, tn), lambda m, j: (m, j))],
        out_specs=pl.BlockSpec((tm, tn), lambda m, j: (m, j)),
        compiler_params=_cparams(2),
        name="matmul_residual",
    )(a, w, res)


def _swiglu_up_kernel(x_ref, wg_ref, wu_ref, o_ref):
    x = x_ref[...]
    g = jnp.dot(x, wg_ref[...].astype(BF16), preferred_element_type=F32)
    u = jnp.dot(x, wu_ref[...].astype(BF16), preferred_element_type=F32)
    o_ref[...] = (jax.nn.silu(g) * u).astype(o_ref.dtype)


def _swiglu_up(x, wg, wu, layer, tm=1024, tn=512):
    M, K = x.shape
    N = wg.shape[2]
    return pl.pallas_call(
        _swiglu_up_kernel,
        out_shape=jax.ShapeDtypeStruct((M, N), BF16),
        grid=(M // tm, N // tn),
        in_specs=[pl.BlockSpec((tm, K), lambda m, j: (m, 0)),
                  pl.BlockSpec((None, K, tn), lambda m, j: (layer, 0, j)),
                  pl.BlockSpec((None, K, tn), lambda m, j: (layer, 0, j))],
        out_specs=pl.BlockSpec((tm, tn), lambda m, j: (m, j)),
        compiler_params=_cparams(2),
        name="swiglu_up",
    )(x, wg, wu)


ROUTER_TM = 512


def _router_kernel(h_ref, g_ref, wr_ref, idx_ref, w_ref):
    n = _rms(h_ref[...], g_ref[...])
    logits = jnp.dot(n, wr_ref[...], preferred_element_type=F32, precision=lax.Precision.HIGHEST)
    lane = lax.broadcasted_iota(jnp.int32, logits.shape, 1)
    v1 = jnp.max(logits, axis=1, keepdims=True)
    i1 = jnp.min(jnp.where(logits == v1, lane, N_EXPERTS), axis=1, keepdims=True)
    rest = jnp.where(lane == i1, -jnp.inf, logits)
    v2 = jnp.max(rest, axis=1, keepdims=True)
    i2 = jnp.min(jnp.where(rest == v2, lane, N_EXPERTS), axis=1, keepdims=True)
    e2 = jnp.exp(v2 - v1)
    den = 1.0 + e2
    two = lax.broadcasted_iota(jnp.int32, (ROUTER_TM, TOP_K), 1)
    idx_ref[...] = jnp.where(two == 0, i1, i2)
    w_ref[...] = jnp.where(two == 0, 1.0 / den, e2 / den)


def _router(h, g, w_router):
    T, D = h.shape
    return pl.pallas_call(
        _router_kernel,
        out_shape=[jax.ShapeDtypeStruct((T, TOP_K), jnp.int32), jax.ShapeDtypeStruct((T, TOP_K), F32)],
        grid=(T // ROUTER_TM,),
        in_specs=[pl.BlockSpec((ROUTER_TM, D), lambda i: (i, 0)),
                  pl.BlockSpec((1, D), lambda i: (0, 0)),
                  pl.BlockSpec((D, N_EXPERTS), lambda i: (0, 0))],
        out_specs=[pl.BlockSpec((ROUTER_TM, TOP_K), lambda i: (i, 0)),
                   pl.BlockSpec((ROUTER_TM, TOP_K), lambda i: (i, 0))],
        compiler_params=_cparams(1),
        name="moe_router",
    )(h, g.reshape(1, D), w_router)


def _row_copy(src_hbm, row, dst_vmem, slot, sem):
    return pltpu.make_async_copy(src_hbm.at[pl.ds(row, 1), :], dst_vmem.at[pl.ds(slot, 1), :], sem)


def _gather_norm_kernel(src_ref, h_hbm, g_ref, o_ref, buf, sem):
    base = pl.program_id(0) * MOE_SUB

    def start(r, c):
        _row_copy(h_hbm, src_ref[base + r], buf, r, sem).start()
        return c

    def wait(r, c):
        _row_copy(h_hbm, 0, buf, r, sem).wait()
        return c

    lax.fori_loop(0, MOE_SUB, start, 0)
    lax.fori_loop(0, MOE_SUB, wait, 0)
    o_ref[...] = _rms(buf[...], g_ref[...]).astype(o_ref.dtype)


def _gather_norm(h, g, src_rows, n_rows):
    T, D = h.shape
    return pl.pallas_call(
        _gather_norm_kernel,
        out_shape=jax.ShapeDtypeStruct((n_rows, D), BF16),
        grid_spec=pltpu.PrefetchScalarGridSpec(
            num_scalar_prefetch=1,
            grid=(n_rows // MOE_SUB,),
            in_specs=[pl.BlockSpec(memory_space=pl.ANY), pl.BlockSpec((1, D), lambda i, s: (0, 0))],
            out_specs=pl.BlockSpec((MOE_SUB, D), lambda i, s: (i, 0)),
            scratch_shapes=[pltpu.VMEM((MOE_SUB, D), F32), pltpu.SemaphoreType.DMA(())],
        ),
        compiler_params=_cparams(1),
        name="moe_gather_norm",
    )(src_rows, h, g.reshape(1, D))


def _expert_kernel(n_items_ref, item_e_ref, item_row_ref, item_sub_ref,
                   x_hbm, wg_ref, wu_ref, wd_ref, y_hbm, x_sc, acc_sc, sem_in, sem_out):
    it = pl.program_id(0)
    f = pl.program_id(1)
    n_f = pl.num_programs(1)
    live = it < n_items_ref[0]
    row0 = pl.multiple_of(item_row_ref[it], MOE_SUB)
    n_sub = item_sub_ref[it]
    item_rows = MOE_SUB * MOE_ITEM_SUBS

    def x_copy():
        return pltpu.make_async_copy(x_hbm.at[pl.ds(row0, item_rows), :], x_sc, sem_in)

    def y_copy(s):
        rows = pl.ds(s * MOE_SUB, MOE_SUB)
        return pltpu.make_async_copy(acc_sc.at[rows, :], y_hbm.at[pl.ds(row0 + s * MOE_SUB, MOE_SUB), :], sem_out)

    @pl.when(live & (f == 0))
    def _():
        cp = x_copy()
        cp.start()
        acc_sc[...] = jnp.zeros(acc_sc.shape, F32)
        cp.wait()

    for n in range(1, MOE_ITEM_SUBS + 1):
        @pl.when(live & (n_sub == n))
        def _(n=n):
            rows = n * MOE_SUB
            x = x_sc[:rows, :]
            g = jnp.dot(x, wg_ref[...].astype(BF16), preferred_element_type=F32)
            u = jnp.dot(x, wu_ref[...].astype(BF16), preferred_element_type=F32)
            mid = (jax.nn.silu(g) * u).astype(BF16)
            acc_sc[:rows, :] += jnp.dot(mid, wd_ref[...].astype(BF16), preferred_element_type=F32)

    @pl.when(live & (f == n_f - 1))
    def _():
        for s in range(MOE_ITEM_SUBS):
            @pl.when(s < n_sub)
            def _(s=s):
                y_copy(s).start()
        for s in range(MOE_ITEM_SUBS):
            @pl.when(s < n_sub)
            def _(s=s):
                y_copy(s).wait()

    @pl.when((it == pl.num_programs(0) - 1) & (f == n_f - 1))
    def _():
        acc_sc[:MOE_SUB, :] = jnp.zeros((MOE_SUB, acc_sc.shape[1]), F32)
        first = n_items_ref[1] // MOE_SUB
        n_blocks = y_hbm.shape[0] // MOE_SUB

        def fill(s):
            dst = y_hbm.at[pl.ds(pl.multiple_of(s * MOE_SUB, MOE_SUB), MOE_SUB), :]
            return pltpu.make_async_copy(acc_sc.at[pl.ds(0, MOE_SUB), :], dst, sem_out)

        lax.fori_loop(first, n_blocks, lambda s, c: (fill(s).start(), c)[1], 0)
        lax.fori_loop(first, n_blocks, lambda s, c: (fill(s).wait(), c)[1], 0)


def _expert_ffn(xs, w_gate, w_up, w_down, n_items, item_e, item_row, item_sub, max_rows, max_items):
    D = xs.shape[1]
    FF = w_gate.shape[3]
    n_f = FF // MOE_TF

    def f_of(it, f, n_items_ref):
        return jnp.where(it < n_items_ref[0], f, n_f - 1)

    return pl.pallas_call(
        _expert_kernel,
        out_shape=jax.ShapeDtypeStruct((max_rows, D), F32),
        grid_spec=pltpu.PrefetchScalarGridSpec(
            num_scalar_prefetch=4,
            grid=(max_items, n_f),
            in_specs=[
                pl.BlockSpec(memory_space=pl.ANY),
                pl.BlockSpec((None, None, D, MOE_TF), lambda it, f, n, e, r, s: (0, e[it], 0, f_of(it, f, n))),
                pl.BlockSpec((None, None, D, MOE_TF), lambda it, f, n, e, r, s: (0, e[it], 0, f_of(it, f, n))),
                pl.BlockSpec((None, None, MOE_TF, D), lambda it, f, n, e, r, s: (0, e[it], f_of(it, f, n), 0)),
            ],
            out_specs=pl.BlockSpec(memory_space=pl.ANY),
            scratch_shapes=[pltpu.VMEM((MOE_SUB * MOE_ITEM_SUBS, D), BF16),
                            pltpu.VMEM((MOE_SUB * MOE_ITEM_SUBS, D), F32),
                            pltpu.SemaphoreType.DMA(()), pltpu.SemaphoreType.DMA(())],
        ),
        compiler_params=_cparams(2),
        name="moe_expert_ffn",
    )(n_items, item_e, item_row, item_sub, xs, w_gate, w_up, w_down)


COMBINE_TM = 256


def _combine_kernel(pos_ref, h_ref, w_ref, g_ref, y_hbm, o_ref, buf, sem):
    base = pl.program_id(0) * COMBINE_TM

    def start(r, c):
        for k in range(TOP_K):
            _row_copy(y_hbm, pos_ref[(base + r) * TOP_K + k], buf.at[k], r, sem).start()
        return c

    def wait(r, c):
        for k in range(TOP_K):
            _row_copy(y_hbm, 0, buf.at[k], r, sem).wait()
        return c

    lax.fori_loop(0, COMBINE_TM, start, 0)
    lax.fori_loop(0, COMBINE_TM, wait, 0)
    w = w_ref[...]
    out = h_ref[...] + w[:, 0:1] * buf[0] + w[:, 1:2] * buf[1]
    o_ref[...] = _rms(out, g_ref[...])


def _combine_norm(h, ys, pos, top_w, g):
    T, D = h.shape
    return pl.pallas_call(
        _combine_kernel,
        out_shape=jax.ShapeDtypeStruct((T, D), F32),
        grid_spec=pltpu.PrefetchScalarGridSpec(
            num_scalar_prefetch=1,
            grid=(T // COMBINE_TM,),
            in_specs=[pl.BlockSpec((COMBINE_TM, D), lambda i, p: (i, 0)),
                      pl.BlockSpec((COMBINE_TM, TOP_K), lambda i, p: (i, 0)),
                      pl.BlockSpec((1, D), lambda i, p: (0, 0)),
                      pl.BlockSpec(memory_space=pl.ANY)],
            out_specs=pl.BlockSpec((COMBINE_TM, D), lambda i, p: (i, 0)),
            scratch_shapes=[pltpu.VMEM((TOP_K, COMBINE_TM, D), F32), pltpu.SemaphoreType.DMA(())],
        ),
        compiler_params=_cparams(1),
        name="moe_combine_norm",
    )(pos.reshape(-1), h, top_w, g.reshape(1, D), ys)


def _moe_plan(top_idx):
    T = top_idx.shape[0]
    n_assign = T * TOP_K
    item_rows = MOE_SUB * MOE_ITEM_SUBS
    max_rows = n_assign + N_EXPERTS * MOE_SUB
    max_items = n_assign // item_rows + N_EXPERTS
    e_flat = top_idx.reshape(-1)
    onehot = (e_flat[:, None] == jnp.arange(N_EXPERTS)[None, :]).astype(jnp.int32)
    ranks = jnp.cumsum(onehot, axis=0) - onehot
    counts = jnp.sum(onehot, axis=0)
    padded = ((counts + MOE_SUB - 1) // MOE_SUB) * MOE_SUB
    group_start = jnp.cumsum(padded) - padded
    rank = jnp.sum(ranks * onehot, axis=1)
    pos = group_start[e_flat] + rank
    src_rows = jnp.zeros((max_rows + item_rows,), jnp.int32).at[pos].set(jnp.arange(n_assign, dtype=jnp.int32) // TOP_K)
    items_per_e = (padded + item_rows - 1) // item_rows
    item_start = jnp.cumsum(items_per_e) - items_per_e
    n_items = jnp.sum(items_per_e)
    it = jnp.arange(max_items)
    item_e = jnp.minimum(jnp.sum(it[:, None] >= (item_start + items_per_e)[None, :], axis=1), N_EXPERTS - 1)
    local = it - item_start[item_e]
    item_row = group_start[item_e] + local * item_rows
    item_sub = jnp.clip((padded[item_e] - local * item_rows) // MOE_SUB, 0, MOE_ITEM_SUBS)
    live = it < n_items
    last = jnp.maximum(n_items - 1, 0)
    item_e = jnp.where(live, item_e, item_e[last])
    item_row = jnp.where(live, item_row, 0)
    item_sub = jnp.where(live, item_sub, 0)
    i32 = lambda a: a.astype(jnp.int32)
    plan = jnp.stack([n_items, jnp.sum(padded)])
    return (i32(pos.reshape(T, TOP_K)), src_rows, i32(plan), i32(item_e), i32(item_row),
            i32(item_sub), max_rows, max_items)


def _moe_and_final_norm(h, g_ffn, w_router, w_gate, w_up, w_down, g_final):
    top_idx, top_w = _router(h, g_ffn, w_router)
    pos, src_rows, plan, item_e, item_row, item_sub, max_rows, max_items = _moe_plan(top_idx)
    xs = _gather_norm(h, g_ffn, src_rows, src_rows.shape[0])
    ys = _expert_ffn(xs, w_gate, w_up, w_down, plan, item_e, item_row, item_sub, max_rows, max_items)
    return _combine_norm(h, ys, pos, top_w, g_final)


def kernel(x, w_in, w_branch, w_out, norm_mix, norm_ffn, norm_final, sinks, mla_q_norm, mla_kv_norm, mla_w_uq,
           mla_w_ukv, ffn_w_gate, ffn_w_up, ffn_w_down, router_w, moe_w_gate, moe_w_up, moe_w_down):
    nb, seq, d = x.shape
    assert (seq, d) == (SEQ, D_MODEL)
    depth = w_in.shape[0]
    assert depth == 2, "layer 0 uses the dense FFN, layer 1 the MoE followed by the final norm"
    cos64, sin64 = _rope_tables(HEAD_DIM, 0, HEAD_DIM)
    cos32, sin32 = _rope_tables(D_ROPE, D_NOPE, LANES)
    h = x.reshape(nb * seq, d)
    out = None
    for layer in range(depth):
        n = _rmsnorm(h, norm_mix[layer], BF16)
        proj = _in_proj(n, w_in, layer, cos64, sin64)
        o_a = _dilated_attention(proj, nb)
        o_b = _sink_attention(proj, sinks[layer], nb)
        o_c = _stick_attention(proj, nb)
        q_d, k_d, v_d = _mla_prep(proj, mla_q_norm[layer], mla_kv_norm[layer], mla_w_uq[layer], mla_w_ukv[layer],
                                  cos32, sin32)
        o_d = _mla_attention(q_d, k_d, v_d, nb)
        w_gates = w_in[layer, :, GATE_COL0:]
        merged = _merge(n, (o_a, o_b, o_c, o_d), w_gates, w_branch, layer)
        h = _matmul_res(merged, w_out, layer, h, 1024, 512)
        if layer == 0:
            n2 = _rmsnorm(h, norm_ffn[layer], BF16)
            mid = _swiglu_up(n2, ffn_w_gate, ffn_w_up, 0)
            h = _matmul_res(mid, ffn_w_down, 0, h, 1024, 256)
        else:
            out = _moe_and_final_norm(h, norm_ffn[layer], router_w[0], moe_w_gate, moe_w_up, moe_w_down, norm_final)
    return out.reshape(nb, seq, d)
```

```python
import functools

import numpy as np
import jax
import jax.numpy as jnp
from jax import lax
from jax.experimental import pallas as pl
from jax.experimental.pallas import tpu as pltpu

F32 = jnp.float32
BF16 = jnp.bfloat16

D_MODEL = 2048
SEQ = 2048
HEAD_DIM = 64
BLOCK = 128
LANES = 128
ROPE_THETA = 10000.0
NORM_EPS = 1e-6
DIL_PATTERNS = ((128, 1), (512, 4), (2048, 16))
B_WINDOW = 128
D_NOPE = 64
D_ROPE = 32
Q_LORA = 384
KV_LORA = 256
N_BRANCH = 4
BRANCH_WIDTH = 512
N_EXPERTS = 8
TOP_K = 2

A_Q0, A_K0, A_V0 = 0, 12, 24
B_Q0, B_K0, B_V0 = 36, 40, 41
C_Q0, C_K0, C_V0 = 42, 46, 50
MLA_COL0 = 6912
MLA_BLOCK_W = 768
QKV_WIDTH = 7680
GATE_COL0 = 7584
PROJ_TN = 256

VMEM_LIMIT = 56 * 1024 * 1024
NEG = -1e30
LOG2_E = 1.4426950408889634

MOE_SUB = 256
MOE_ITEM_SUBS = 4
MOE_TF = 256


def _cparams(n_axes):
    return pltpu.CompilerParams(dimension_semantics=("arbitrary",) * n_axes, vmem_limit_bytes=VMEM_LIMIT)


def _lane_half():
    return lax.broadcasted_iota(jnp.int32, (BLOCK, LANES), 1) < HEAD_DIM


def _rms(x, g):
    return x * lax.rsqrt(jnp.mean(x * x, axis=-1, keepdims=True) + NORM_EPS) * g


def _rmsnorm_kernel(x_ref, g_ref, o_ref):
    o_ref[...] = _rms(x_ref[...], g_ref[...]).astype(o_ref.dtype)


def _rmsnorm(x, g, out_dtype, tm=512):
    T, D = x.shape
    return pl.pallas_call(
        _rmsnorm_kernel,
        out_shape=jax.ShapeDtypeStruct((T, D), out_dtype),
        grid=(T // tm,),
        in_specs=[pl.BlockSpec((tm, D), lambda i: (i, 0)), pl.BlockSpec((1, D), lambda i: (0, 0))],
        out_specs=pl.BlockSpec((tm, D), lambda i: (i, 0)),
        compiler_params=_cparams(1),
        name="rmsnorm",
    )(x, g.reshape(1, D))


def _rope_tables(dh, lane0, period):
    half = dh // 2
    freqs = ROPE_THETA ** (-2.0 * jnp.arange(half, dtype=F32) / dh)
    ang = jnp.arange(SEQ, dtype=F32)[:, None] * freqs[None, :]
    cos, sin = jnp.cos(ang), jnp.sin(ang)
    cos_h = jnp.concatenate([cos, cos], axis=1)
    sin_h = jnp.concatenate([-sin, sin], axis=1)
    ones = jnp.ones((SEQ, period - dh), F32)
    cos_p = jnp.concatenate([cos_h, ones], axis=1)
    sin_p = jnp.concatenate([sin_h, 0.0 * ones], axis=1)
    reps = LANES // period
    cos_t = jnp.roll(jnp.tile(cos_p, (1, reps)), lane0, axis=1)
    sin_t = jnp.roll(jnp.tile(sin_p, (1, reps)), lane0, axis=1)
    return cos_t, sin_t


def _rope_lanes(x, cos, sin, half, first_half_mask):
    fwd = pltpu.roll(x, LANES - half, 1)
    bwd = pltpu.roll(x, half, 1)
    return x * cos + jnp.where(first_half_mask, fwd, bwd) * sin


def _proj_kernel(x_ref, w_ref, cos_a, sin_a, cos_b, sin_b, o_ref):
    acc = jnp.dot(x_ref[...], w_ref[...].astype(BF16), preferred_element_type=F32)
    lane = lax.broadcasted_iota(jnp.int32, (SEQ, LANES), 1)
    first = (lane % HEAD_DIM) < HEAD_DIM // 2
    o_ref[:, :LANES] = _rope_lanes(acc[:, :LANES], cos_a[...], sin_a[...], HEAD_DIM // 2, first)
    o_ref[:, LANES:] = _rope_lanes(acc[:, LANES:], cos_b[...], sin_b[...], HEAD_DIM // 2, first)


def _in_proj(n, w_in, layer, cos, sin):
    T, D = n.shape
    nb = T // SEQ
    tables = (jnp.stack([cos, jnp.ones_like(cos)]), jnp.stack([sin, jnp.zeros_like(sin)]))

    def variant(j, half):
        rope_both = (j < 12) | ((j >= 18) & (j < 20))
        rope = rope_both | (j == 20) if half == 0 else rope_both
        return jnp.where(rope, 0, 1)

    table_specs = [pl.BlockSpec((None, SEQ, LANES), lambda b, j, half=half: (variant(j, half), 0, 0))
                   for half in (0, 0, 1, 1)]
    return pl.pallas_call(
        _proj_kernel,
        out_shape=jax.ShapeDtypeStruct((T, QKV_WIDTH), F32),
        grid=(nb, QKV_WIDTH // PROJ_TN),
        in_specs=[
            pl.BlockSpec((SEQ, D), lambda b, j: (b, 0)),
            pl.BlockSpec((None, D, PROJ_TN), lambda b, j: (layer, 0, j)),
        ] + table_specs,
        out_specs=pl.BlockSpec((SEQ, PROJ_TN), lambda b, j: (b, j)),
        compiler_params=_cparams(2),
        name="in_proj_rope",
    )(n, w_in, tables[0], tables[1], tables[0], tables[1])


BAND_UNROLL = 8


def _band_scores(q, kcat, vcat, head, has_prev, strict_prev):
    half = _lane_half()
    qh = jnp.where(half if head == 0 else jnp.logical_not(half), q * (HEAD_DIM ** -0.5), 0.0).astype(BF16)
    s = lax.dot_general(qh, kcat, (((1,), (1,)), ((), ())), preferred_element_type=F32)
    row = lax.broadcasted_iota(jnp.int32, (BLOCK, 2 * BLOCK), 0)
    col = lax.broadcasted_iota(jnp.int32, (BLOCK, 2 * BLOCK), 1)
    dist = row + BLOCK - col
    max_dist = BLOCK - 1 if strict_prev else BLOCK
    first_col = jnp.where(has_prev, 0, BLOCK)
    valid = (dist >= 0) & (dist <= max_dist) & (col >= first_col)
    s = jnp.where(valid, s, NEG)
    m = jnp.max(s, axis=1, keepdims=True)
    p = jnp.exp(s - m)
    l = jnp.sum(p, axis=1, keepdims=True)
    acc = jnp.dot(p.astype(BF16), vcat, preferred_element_type=F32)
    return acc, m, l


def _pair(a0, a1):
    return jnp.where(_lane_half(), a0, a1)


def _dilated_kernel(*refs):
    q_refs, k_refs, v_refs = refs[0:3], refs[3:6], refs[6:9]
    o_ref = refs[9]
    acc_sc, m_sc, l_sc = refs[10:13]
    n_blocks = SEQ // BLOCK

    for g, (win, dil) in enumerate(DIL_PATTERNS):
        assert win // dil == BLOCK and SEQ % (dil * BLOCK) == 0
        q_ref, k_ref, v_ref = q_refs[g], k_refs[g], v_refs[g]

        def rows(start, dil=dil):
            return pl.ds(start, BLOCK) if dil == 1 else pl.ds(start, BLOCK, stride=dil)

        def block(it, g=g, dil=dil, q_ref=q_ref, k_ref=k_ref, v_ref=v_ref, rows=rows):
            r = it % dil
            i = it // dil
            base = i * (BLOCK * dil) + r
            prev = jnp.maximum(base - BLOCK * dil, 0)
            q = q_ref[rows(base), :]
            kcat = jnp.concatenate([k_ref[rows(prev), :], k_ref[rows(base), :]], axis=0).astype(BF16)
            vcat = jnp.concatenate([v_ref[rows(prev), :], v_ref[rows(base), :]], axis=0).astype(BF16)
            a0, m0, l0 = _band_scores(q, kcat, vcat, 0, i > 0, False)
            a1, m1, l1 = _band_scores(q, kcat, vcat, 1, i > 0, False)
            acc_sc[g, rows(base), :] = _pair(a0, a1)
            m_sc[g, rows(base), :] = _pair(m0, m1)
            l_sc[g, rows(base), :] = _pair(l0, l1)

        def body(it, carry, block=block):
            for u in range(BAND_UNROLL):
                block(it + u * (n_blocks // BAND_UNROLL))
            return carry

        lax.fori_loop(0, n_blocks // BAND_UNROLL, body, 0)

    def combine(i, carry):
        rows = pl.ds(pl.multiple_of(i * BLOCK, BLOCK), BLOCK)
        m = [m_sc[g, rows, :] for g in range(len(DIL_PATTERNS))]
        m_all = jnp.maximum(jnp.maximum(m[0], m[1]), m[2])
        num = jnp.zeros((BLOCK, LANES), F32)
        den = jnp.zeros((BLOCK, LANES), F32)
        for g in range(len(DIL_PATTERNS)):
            a = jnp.exp(m[g] - m_all)
            num = num + a * acc_sc[g, rows, :]
            den = den + a * l_sc[g, rows, :]
        o_ref[rows, :] = (num / den).astype(o_ref.dtype)
        return carry

    lax.fori_loop(0, n_blocks, combine, 0)


def _dilated_attention(proj, nb):
    T = proj.shape[0]
    specs = []
    for col0 in (A_Q0, A_K0, A_V0):
        for g in range(len(DIL_PATTERNS)):
            specs.append(pl.BlockSpec((SEQ, LANES), lambda b, hp, c=col0 + 4 * g: (b, c + hp)))
    return pl.pallas_call(
        _dilated_kernel,
        out_shape=jax.ShapeDtypeStruct((T, BRANCH_WIDTH), BF16),
        grid=(nb, BRANCH_WIDTH // LANES),
        in_specs=specs,
        out_specs=pl.BlockSpec((SEQ, LANES), lambda b, hp: (b, hp)),
        scratch_shapes=[pltpu.VMEM((len(DIL_PATTERNS), SEQ, LANES), F32)] * 3,
        compiler_params=_cparams(2),
        name="dilated_attention",
    )(*([proj] * 9))


def _sink_kernel(q_ref, k_ref, v_ref, sink_ref, o_ref, k_sc, v_sc):
    hp = pl.program_id(1)
    kv_head = hp // 2
    k = k_ref[...]
    v = v_ref[...]
    k_sw = pltpu.roll(k, HEAD_DIM, 1)
    v_sw = pltpu.roll(v, HEAD_DIM, 1)
    first = kv_head == 0
    k_sc[0] = jnp.where(first, k, k_sw).astype(BF16)
    k_sc[1] = jnp.where(first, k_sw, k).astype(BF16)
    v_sc[0] = jnp.where(first, v, v_sw).astype(BF16)
    v_sc[1] = jnp.where(first, v_sw, v).astype(BF16)
    sink = _pair(sink_ref[pl.ds(2 * hp, 1), :], sink_ref[pl.ds(2 * hp + 1, 1), :])

    def block(i):
        base = pl.multiple_of(i * BLOCK, BLOCK)
        prev = pl.multiple_of(jnp.maximum(base - BLOCK, 0), BLOCK)
        q = q_ref[pl.ds(base, BLOCK), :]
        out = []
        for a in range(2):
            kcat = jnp.concatenate([k_sc[a, pl.ds(prev, BLOCK), :], k_sc[a, pl.ds(base, BLOCK), :]], axis=0)
            vcat = jnp.concatenate([v_sc[a, pl.ds(prev, BLOCK), :], v_sc[a, pl.ds(base, BLOCK), :]], axis=0)
            out.append(_band_scores(q, kcat, vcat, a, i > 0, True))
        acc = _pair(out[0][0], out[1][0])
        m = _pair(out[0][1], out[1][1])
        l = _pair(out[0][2], out[1][2])
        m_new = jnp.maximum(m, sink)
        scale = jnp.exp(m - m_new)
        den = l * scale + jnp.exp(sink - m_new)
        o_ref[pl.ds(base, BLOCK), :] = (acc * scale / den).astype(o_ref.dtype)

    n_blocks = SEQ // BLOCK

    def body(it, carry):
        for u in range(BAND_UNROLL):
            block(it + u * (n_blocks // BAND_UNROLL))
        return carry

    lax.fori_loop(0, n_blocks // BAND_UNROLL, body, 0)


def _sink_attention(proj, sinks_l, nb):
    T = proj.shape[0]
    sink_tab = jnp.broadcast_to(sinks_l.astype(F32)[:, None], (8, LANES))
    return pl.pallas_call(
        _sink_kernel,
        out_shape=jax.ShapeDtypeStruct((T, BRANCH_WIDTH), BF16),
        grid=(nb, BRANCH_WIDTH // LANES),
        in_specs=[
            pl.BlockSpec((SEQ, LANES), lambda b, hp: (b, B_Q0 + hp)),
            pl.BlockSpec((SEQ, LANES), lambda b, hp: (b, B_K0)),
            pl.BlockSpec((SEQ, LANES), lambda b, hp: (b, B_V0)),
            pl.BlockSpec((8, LANES), lambda b, hp: (0, 0)),
        ],
        out_specs=pl.BlockSpec((SEQ, LANES), lambda b, hp: (b, hp)),
        scratch_shapes=[pltpu.VMEM((2, SEQ, LANES), BF16)] * 2,
        compiler_params=_cparams(2),
        name="sink_attention",
    )(proj, proj, proj, sink_tab)


STICK_TQ = 1024


def _stick_kernel(q_ref, k_ref, v_ref, tri_ref, o_ref, q_sc, k_sc, v_sc, carry_sc, acc_sc):
    half = lax.broadcasted_iota(jnp.int32, (SEQ, LANES), 1) < HEAD_DIM
    q = q_ref[...] * (HEAD_DIM ** -0.5)
    q_sc[0] = jnp.where(half, q, 0.0).astype(BF16)
    q_sc[1] = jnp.where(half, 0.0, q).astype(BF16)
    k_sc[...] = k_ref[...].astype(BF16)
    v_sc[...] = v_ref[...].astype(BF16)
    carry_sc[...] = jnp.zeros_like(carry_sc)
    acc_sc[...] = jnp.zeros_like(acc_sc)
    tri = tri_ref[...]
    n_key_blocks = SEQ // BLOCK
    n_q_tiles = SEQ // STICK_TQ
    lane_half = lax.broadcasted_iota(jnp.int32, (STICK_TQ, LANES), 1) < HEAD_DIM

    def tile(qi, j, overlaps):
        qb = pl.multiple_of(qi * STICK_TQ, STICK_TQ)
        kb = pl.multiple_of(j * BLOCK, BLOCK)
        kj = k_sc[pl.ds(kb, BLOCK), :]
        vj = v_sc[pl.ds(kb, BLOCK), :]
        outs = []
        for a in range(2):
            z = lax.dot_general(q_sc[a, pl.ds(qb, STICK_TQ), :], kj, (((1,), (1,)), ((), ())),
                                preferred_element_type=F32)
            sp = jnp.maximum(z, 0.0) + jnp.log(1.0 + jnp.exp(-jnp.abs(z)))
            if overlaps:
                row = lax.broadcasted_iota(jnp.int32, (STICK_TQ, BLOCK), 0)
                col = lax.broadcasted_iota(jnp.int32, (STICK_TQ, BLOCK), 1)
                before = col + (kb - qb) < row
                sp = jnp.where(before, sp, 0.0)
            hi = sp.astype(BF16)
            lo = (sp - hi.astype(F32)).astype(BF16)
            sums = jnp.dot(jnp.concatenate([hi, lo], axis=1), tri, preferred_element_type=F32)
            carry = carry_sc[a, pl.ds(qb, STICK_TQ), :]
            w = jnp.exp(z + carry + sums[:, :BLOCK])
            if overlaps:
                w = jnp.where(before, w, 0.0)
            carry_sc[a, pl.ds(qb, STICK_TQ), :] = carry + sums[:, BLOCK:]
            outs.append(jnp.dot(w.astype(BF16), vj, preferred_element_type=F32))
        acc_sc[pl.ds(qb, STICK_TQ), :] += jnp.where(lane_half, outs[0], outs[1])

    def key_block(jj, c):
        j = n_key_blocks - 1 - jj
        first = j // (STICK_TQ // BLOCK)
        tile(first, j, True)

        def q_tile(qi, c2):
            tile(qi, j, False)
            return c2

        lax.fori_loop(first + 1, n_q_tiles, q_tile, 0)
        return c

    lax.fori_loop(0, n_key_blocks, key_block, 0)
    o_ref[...] = acc_sc[...].astype(o_ref.dtype)


def _stick_attention(proj, nb):
    T = proj.shape[0]
    jp = np.arange(BLOCK)[:, None]
    s = np.arange(BLOCK)[None, :]
    tri = -np.concatenate([(jp >= s), np.ones((BLOCK, BLOCK), bool)], axis=1).astype(np.float32)
    tri = jnp.asarray(np.concatenate([tri, tri], axis=0), BF16)
    return pl.pallas_call(
        _stick_kernel,
        out_shape=jax.ShapeDtypeStruct((T, BRANCH_WIDTH), BF16),
        grid=(nb, BRANCH_WIDTH // LANES),
        in_specs=[
            pl.BlockSpec((SEQ, LANES), lambda b, hp: (b, C_Q0 + hp)),
            pl.BlockSpec((SEQ, LANES), lambda b, hp: (b, C_K0 + hp)),
            pl.BlockSpec((SEQ, LANES), lambda b, hp: (b, C_V0 + hp)),
            pl.BlockSpec((2 * BLOCK, 2 * BLOCK), lambda b, hp: (0, 0)),
        ],
        out_specs=pl.BlockSpec((SEQ, LANES), lambda b, hp: (b, hp)),
        scratch_shapes=[
            pltpu.VMEM((2, SEQ, LANES), BF16),
            pltpu.VMEM((SEQ, LANES), BF16),
            pltpu.VMEM((SEQ, LANES), BF16),
            pltpu.VMEM((2, SEQ, LANES), F32),
            pltpu.VMEM((SEQ, LANES), F32),
        ],
        compiler_params=_cparams(2),
        name="stick_breaking_attention",
    )(proj, proj, proj, tri)


MLA_HEADS = 8
MLA_TM = 512


def _mla_prep_kernel(c_ref, gq_ref, gkv_ref, wq_ref, wk_ref, wv_ref, cos_ref, sin_ref, q_ref, k_ref, v_ref):
    c = c_ref[...]
    cq = _rms(c[:, :Q_LORA], gq_ref[...]).astype(BF16)
    ckv = _rms(c[:, Q_LORA:Q_LORA + KV_LORA], gkv_ref[...]).astype(BF16)
    q = jnp.dot(cq, wq_ref[...].astype(BF16), preferred_element_type=F32)
    k = jnp.dot(ckv, wk_ref[...].astype(BF16), preferred_element_type=F32)
    v_ref[...] = jnp.dot(ckv, wv_ref[...].astype(BF16), preferred_element_type=F32).astype(BF16)
    cos, sin = cos_ref[...], sin_ref[...]
    lane = lax.broadcasted_iota(jnp.int32, (MLA_TM, LANES), 1)
    rope_lane = (lane >= D_NOPE) & (lane < D_NOPE + D_ROPE)
    first = lane < D_NOPE + D_ROPE // 2
    kr = pltpu.roll(c[:, Q_LORA + KV_LORA:], D_NOPE, 1)
    kr = jnp.where(rope_lane, _rope_lanes(kr, cos, sin, D_ROPE // 2, first), 0.0)
    scale = (D_NOPE + D_ROPE) ** -0.5 * LOG2_E
    for h in range(MLA_HEADS):
        sl = slice(h * LANES, (h + 1) * LANES)
        q_ref[:, sl] = (_rope_lanes(q[:, sl], cos, sin, D_ROPE // 2, first) * scale).astype(BF16)
        k_ref[:, sl] = (k[:, sl] + kr).astype(BF16)


def _mla_prep(proj, g_qa, g_kva, w_uq, w_ukv, cos, sin):
    T = proj.shape[0]
    wq = jnp.pad(w_uq.reshape(Q_LORA, MLA_HEADS, D_NOPE + D_ROPE), ((0, 0), (0, 0), (0, LANES - D_NOPE - D_ROPE)))
    wq = wq.reshape(Q_LORA, MLA_HEADS * LANES)
    wkv = w_ukv.reshape(KV_LORA, MLA_HEADS, 2 * HEAD_DIM)
    wk = jnp.pad(wkv[:, :, :D_NOPE], ((0, 0), (0, 0), (0, LANES - D_NOPE))).reshape(KV_LORA, MLA_HEADS * LANES)
    wv = wkv[:, :, D_NOPE:].reshape(KV_LORA, MLA_HEADS * HEAD_DIM)
    full = lambda shape: pl.BlockSpec(shape, lambda i: (0, 0))
    per_seq = SEQ // MLA_TM
    return pl.pallas_call(
        _mla_prep_kernel,
        out_shape=[jax.ShapeDtypeStruct((T, MLA_HEADS * LANES), BF16),
                   jax.ShapeDtypeStruct((T, MLA_HEADS * LANES), BF16),
                   jax.ShapeDtypeStruct((T, MLA_HEADS * HEAD_DIM), BF16)],
        grid=(T // MLA_TM,),
        in_specs=[
            pl.BlockSpec((MLA_TM, MLA_BLOCK_W), lambda i: (i, MLA_COL0 // MLA_BLOCK_W)),
            full((1, Q_LORA)), full((1, KV_LORA)),
            full(wq.shape), full(wk.shape), full(wv.shape),
            pl.BlockSpec((MLA_TM, LANES), lambda i: (i % per_seq, 0)),
            pl.BlockSpec((MLA_TM, LANES), lambda i: (i % per_seq, 0)),
        ],
        out_specs=[pl.BlockSpec((MLA_TM, MLA_HEADS * LANES), lambda i: (i, 0)),
                   pl.BlockSpec((MLA_TM, MLA_HEADS * LANES), lambda i: (i, 0)),
                   pl.BlockSpec((MLA_TM, MLA_HEADS * HEAD_DIM), lambda i: (i, 0))],
        compiler_params=_cparams(1),
        name="mla_prep",
    )(proj, g_qa.reshape(1, Q_LORA), g_kva.reshape(1, KV_LORA), wq, wk, wv, cos, sin)


MLA_TQ = 1024
MLA_TK = 256


def _mla_attn_kernel(q_ref, k_ref, v_ref, o_ref, v_sc, m_sc, acc_sc):
    half = lax.broadcasted_iota(jnp.int32, (SEQ, LANES), 1) < HEAD_DIM
    v = v_ref[...]
    one = jnp.ones_like(v)
    v_sc[0] = jnp.where(half, v, one)
    v_sc[1] = jnp.where(half, one, v)
    lane_half = lax.broadcasted_iota(jnp.int32, (MLA_TQ, LANES), 1) < HEAD_DIM

    def step(qb, kb, diagonal):
        for a in range(2):
            q = q_ref[pl.ds(qb, MLA_TQ), a * LANES:(a + 1) * LANES]
            k = k_ref[pl.ds(kb, MLA_TK), a * LANES:(a + 1) * LANES]
            s = lax.dot_general(q, k, (((1,), (1,)), ((), ())), preferred_element_type=F32)
            if diagonal:
                row = lax.broadcasted_iota(jnp.int32, (MLA_TQ, MLA_TK), 0)
                col = lax.broadcasted_iota(jnp.int32, (MLA_TQ, MLA_TK), 1)
                s = jnp.where(col + (kb - qb) <= row, s, NEG)
            m = m_sc[a]
            m_new = jnp.maximum(m, jnp.max(s, axis=1, keepdims=True))
            alpha = jnp.exp2(m - m_new)
            p = jnp.exp2(s - jnp.concatenate([m_new] * (MLA_TK // LANES), axis=1))
            pv = jnp.dot(p.astype(BF16), v_sc[a, pl.ds(kb, MLA_TK), :], preferred_element_type=F32)
            acc_sc[a] = alpha * acc_sc[a] + pv
            m_sc[a] = m_new

    def q_block(qi, c):
        qb = pl.multiple_of(qi * MLA_TQ, MLA_TQ)
        m_sc[...] = jnp.full(m_sc.shape, NEG, F32)
        acc_sc[...] = jnp.zeros(acc_sc.shape, F32)

        def k_block(kj, c2):
            step(qb, pl.multiple_of(kj * MLA_TK, MLA_TK), False)
            return c2

        lax.fori_loop(0, qi * (MLA_TQ // MLA_TK), k_block, 0)
        for d in range(MLA_TQ // MLA_TK):
            step(qb, qb + d * MLA_TK, True)
        outs = [acc_sc[a] / pltpu.roll(acc_sc[a], HEAD_DIM, 1) for a in range(2)]
        o_ref[pl.ds(qb, MLA_TQ), :] = jnp.where(lane_half, outs[0], outs[1]).astype(o_ref.dtype)
        return c

    lax.fori_loop(0, SEQ // MLA_TQ, q_block, 0)


def _mla_attention(q, k, v, nb):
    T = q.shape[0]
    return pl.pallas_call(
        _mla_attn_kernel,
        out_shape=jax.ShapeDtypeStruct((T, BRANCH_WIDTH), BF16),
        grid=(nb, BRANCH_WIDTH // LANES),
        in_specs=[
            pl.BlockSpec((SEQ, 2 * LANES), lambda b, hp: (b, hp)),
            pl.BlockSpec((SEQ, 2 * LANES), lambda b, hp: (b, hp)),
            pl.BlockSpec((SEQ, LANES), lambda b, hp: (b, hp)),
        ],
        out_specs=pl.BlockSpec((SEQ, LANES), lambda b, hp: (b, hp)),
        scratch_shapes=[pltpu.VMEM((2, SEQ, LANES), BF16),
                        pltpu.VMEM((2, MLA_TQ, LANES), F32),
                        pltpu.VMEM((2, MLA_TQ, LANES), F32)],
        compiler_params=_cparams(2),
        name="mla_attention",
    )(q, k, v)


MERGE_TM = 1024
MERGE_TN = 256


def _merge_kernel(*refs):
    n_ref = refs[0]
    o_refs = refs[1:5]
    wg_refs = refs[5:9]
    wb_refs = refs[9:13]
    out_ref = refs[13]
    n = n_ref[...]
    acc = jnp.zeros((MERGE_TM, MERGE_TN), F32)
    for i in range(N_BRANCH):
        gate = jnp.dot(n, wg_refs[i][...], preferred_element_type=F32)
        br = jnp.dot(o_refs[i][...], wb_refs[i][...].astype(BF16), preferred_element_type=F32)
        acc = acc + br * jax.nn.sigmoid(gate)
    out_ref[...] = acc.astype(out_ref.dtype)


def _gate_weight_kernel(t0_ref, t1_ref, t2_ref, o_ref):
    shift = LANES - GATE_COL0 % LANES
    keep = lax.broadcasted_iota(jnp.int32, t0_ref.shape, 1) < shift
    r0, r1, r2 = (pltpu.roll(t[...], shift, 1) for t in (t0_ref, t1_ref, t2_ref))
    o_ref[:, :LANES] = jnp.where(keep, r0, r1).astype(o_ref.dtype)
    o_ref[:, LANES:] = jnp.where(keep, r1, r2).astype(o_ref.dtype)


def _gate_weights(w_in, layer):
    D = w_in.shape[1]
    n_gate = N_BRANCH * D_MODEL
    tile0 = GATE_COL0 // LANES
    assert GATE_COL0 + n_gate == w_in.shape[2]
    return pl.pallas_call(
        _gate_weight_kernel,
        out_shape=jax.ShapeDtypeStruct((D, n_gate), BF16),
        grid=(n_gate // (2 * LANES),),
        in_specs=[pl.BlockSpec((None, D, LANES), lambda j, c=c: (layer, 0, tile0 + 2 * j + c)) for c in range(3)],
        out_specs=pl.BlockSpec((D, 2 * LANES), lambda j: (0, j)),
        compiler_params=_cparams(1),
        name="gate_weight_align",
    )(w_in, w_in, w_in)


def _merge(n, branches, w_gates, w_branch, layer):
    T, D = n.shape
    nblk = D // MERGE_TN
    in_specs = [pl.BlockSpec((MERGE_TM, D), lambda m, j: (m, 0))]
    in_specs += [pl.BlockSpec((MERGE_TM, BRANCH_WIDTH), lambda m, j: (m, 0))] * N_BRANCH
    in_specs += [pl.BlockSpec((D, MERGE_TN), lambda m, j, i=i: (0, i * nblk + j)) for i in range(N_BRANCH)]
    in_specs += [pl.BlockSpec((None, None, BRANCH_WIDTH, MERGE_TN), lambda m, j, i=i: (layer, i, 0, j))
                 for i in range(N_BRANCH)]
    return pl.pallas_call(
        _merge_kernel,
        out_shape=jax.ShapeDtypeStruct((T, D), BF16),
        grid=(T // MERGE_TM, nblk),
        in_specs=in_specs,
        out_specs=pl.BlockSpec((MERGE_TM, MERGE_TN), lambda m, j: (m, j)),
        compiler_params=_cparams(2),
        name="gated_merge",
    )(n, *branches, *([w_gates] * N_BRANCH), *([w_branch] * N_BRANCH))


def _matmul_res_kernel(a_ref, w_ref, r_ref, o_ref):
    o_ref[...] = r_ref[...] + jnp.dot(a_ref[...], w_ref[...].astype(BF16), preferred_element_type=F32)


def _matmul_res(a, w, layer, res, tm, tn):
    M, K = a.shape
    N = w.shape[2]
    return pl.pallas_call(
        _matmul_res_kernel,
        out_shape=jax.ShapeDtypeStruct((M, N), F32),
        grid=(M // tm, N // tn),
        in_specs=[pl.BlockSpec((tm, K), lambda m, j: (m, 0)),
                  pl.BlockSpec((None, K, tn), lambda m, j: (layer, 0, j)),
                  pl.BlockSpec((tm, tn), lambda m, j: (m, j))],
        out_specs=pl.BlockSpec((tm, tn), lambda m, j: (m, j)),
        compiler_params=_cparams(2),
        name="matmul_residual",
    )(a, w, res)


def _swiglu_up_kernel(x_ref, wg_ref, wu_ref, o_ref):
    x = x_ref[...]
    g = jnp.dot(x, wg_ref[...].astype(BF16), preferred_element_type=F32)
    u = jnp.dot(x, wu_ref[...].astype(BF16), preferred_element_type=F32)
    o_ref[...] = (jax.nn.silu(g) * u).astype(o_ref.dtype)


def _swiglu_up(x, wg, wu, layer, tm=1024, tn=512):
    M, K = x.shape
    N = wg.shape[2]
    return pl.pallas_call(
        _swiglu_up_kernel,
        out_shape=jax.ShapeDtypeStruct((M, N), BF16),
        grid=(M // tm, N // tn),
        in_specs=[pl.BlockSpec((tm, K), lambda m, j: (m, 0)),
                  pl.BlockSpec((None, K, tn), lambda m, j: (layer, 0, j)),
                  pl.BlockSpec((None, K, tn), lambda m, j: (layer, 0, j))],
        out_specs=pl.BlockSpec((tm, tn), lambda m, j: (m, j)),
        compiler_params=_cparams(2),
        name="swiglu_up",
    )(x, wg, wu)


ROUTER_TM = 512


def _router_kernel(h_ref, g_ref, wr_ref, idx_ref, w_ref):
    n = _rms(h_ref[...], g_ref[...])
    logits = jnp.dot(n, wr_ref[...], preferred_element_type=F32, precision=lax.Precision.HIGHEST)
    lane = lax.broadcasted_iota(jnp.int32, logits.shape, 1)
    v1 = jnp.max(logits, axis=1, keepdims=True)
    i1 = jnp.min(jnp.where(logits == v1, lane, N_EXPERTS), axis=1, keepdims=True)
    rest = jnp.where(lane == i1, -jnp.inf, logits)
    v2 = jnp.max(rest, axis=1, keepdims=True)
    i2 = jnp.min(jnp.where(rest == v2, lane, N_EXPERTS), axis=1, keepdims=True)
    e2 = jnp.exp(v2 - v1)
    den = 1.0 + e2
    two = lax.broadcasted_iota(jnp.int32, (ROUTER_TM, TOP_K), 1)
    idx_ref[...] = jnp.where(two == 0, i1, i2)
    w_ref[...] = jnp.where(two == 0, 1.0 / den, e2 / den)


def _router(h, g, w_router):
    T, D = h.shape
    return pl.pallas_call(
        _router_kernel,
        out_shape=[jax.ShapeDtypeStruct((T, TOP_K), jnp.int32), jax.ShapeDtypeStruct((T, TOP_K), F32)],
        grid=(T // ROUTER_TM,),
        in_specs=[pl.BlockSpec((ROUTER_TM, D), lambda i: (i, 0)),
                  pl.BlockSpec((1, D), lambda i: (0, 0)),
                  pl.BlockSpec((D, N_EXPERTS), lambda i: (0, 0))],
        out_specs=[pl.BlockSpec((ROUTER_TM, TOP_K), lambda i: (i, 0)),
                   pl.BlockSpec((ROUTER_TM, TOP_K), lambda i: (i, 0))],
        compiler_params=_cparams(1),
        name="moe_router",
    )(h, g.reshape(1, D), w_router)


def _row_copy(src_hbm, row, dst_vmem, slot, sem):
    return pltpu.make_async_copy(src_hbm.at[pl.ds(row, 1), :], dst_vmem.at[pl.ds(slot, 1), :], sem)


def _expert_kernel(n_f, plan_ref, item_e_ref, item_row_ref, item_sub_ref, src_ref,
                   h_hbm, g_ref, wg_ref, wu_ref, wd_ref, y_hbm, gbuf, x_sc, acc_sc, sem_in, sem_out):
    it = pl.program_id(0)
    f = pl.program_id(1)
    n_it = pl.num_programs(0)
    n_items = plan_ref[0]
    live = it < n_items
    row0 = pl.multiple_of(item_row_ref[it], MOE_SUB)
    n_sub = item_sub_ref[it]
    item_rows = x_sc.shape[0]
    g_rows = gbuf.shape[0]
    chunk = g_rows // n_f

    def gather_wait():
        pltpu.make_async_copy(h_hbm.at[pl.ds(0, g_rows), :], gbuf, sem_in).wait()

    def y_copy(s):
        rows = pl.ds(s * MOE_SUB, MOE_SUB)
        return pltpu.make_async_copy(acc_sc.at[rows, :], y_hbm.at[pl.ds(row0 + s * MOE_SUB, MOE_SUB), :], sem_out)

    @pl.when(live & (it == 0) & (f == 0))
    def _():
        def start(r, c):
            _row_copy(h_hbm, src_ref[row0 + r], gbuf, r, sem_in).start()
            return c
        lax.fori_loop(0, g_rows, start, 0)

    @pl.when(live & (f == 0))
    def _():
        gather_wait()

        def norm(i, c):
            rows = pl.ds(pl.multiple_of(i * MOE_SUB, MOE_SUB), MOE_SUB)
            x_sc[rows, :] = _rms(gbuf[rows, :], g_ref[...]).astype(BF16)
            return c
        lax.fori_loop(0, item_rows // MOE_SUB, norm, 0)
        acc_sc[...] = jnp.zeros(acc_sc.shape, F32)

    @pl.when((it == n_items) & (it > 0) & (f == 0))
    def _():
        gather_wait()

    next_row0 = item_row_ref[jnp.minimum(it + 1, n_it - 1)]

    for n in range(1, MOE_ITEM_SUBS + 1):
        @pl.when(live & (n_sub == n))
        def _(n=n):
            lo = f * chunk
            for r in range(chunk):
                _row_copy(h_hbm, src_ref[next_row0 + lo + r], gbuf, lo + r, sem_in).start()
            rows = n * MOE_SUB
            x = x_sc[:rows, :]
            g = jnp.dot(x, wg_ref[...].astype(BF16), preferred_element_type=F32)
            u = jnp.dot(x, wu_ref[...].astype(BF16), preferred_element_type=F32)
            mid = (jax.nn.silu(g) * u).astype(BF16)
            acc_sc[:rows, :] += jnp.dot(mid, wd_ref[...].astype(BF16), preferred_element_type=F32)

    @pl.when(live & (f == n_f - 1))
    def _():
        for s in range(MOE_ITEM_SUBS):
            @pl.when(s < n_sub)
            def _(s=s):
                y_copy(s).start()
        for s in range(MOE_ITEM_SUBS):
            @pl.when(s < n_sub)
            def _(s=s):
                y_copy(s).wait()

    @pl.when((it == n_it - 1) & (f == n_f - 1))
    def _():
        @pl.when(live)
        def _():
            gather_wait()

        acc_sc[:MOE_SUB, :] = jnp.zeros((MOE_SUB, acc_sc.shape[1]), F32)
        first = plan_ref[1] // MOE_SUB
        n_blocks = y_hbm.shape[0] // MOE_SUB

        def fill(s):
            dst = y_hbm.at[pl.ds(pl.multiple_of(s * MOE_SUB, MOE_SUB), MOE_SUB), :]
            return pltpu.make_async_copy(acc_sc.at[pl.ds(0, MOE_SUB), :], dst, sem_out)

        lax.fori_loop(first, n_blocks, lambda s, c: (fill(s).start(), c)[1], 0)
        lax.fori_loop(first, n_blocks, lambda s, c: (fill(s).wait(), c)[1], 0)


def _gather_rows_per_item(n_f):
    item_rows = MOE_SUB * MOE_ITEM_SUBS
    chunk = -(-item_rows // n_f)
    chunk += -chunk % 8
    return chunk * n_f


def _expert_ffn(h, g, w_gate, w_up, w_down, plan, item_e, item_row, item_sub, src_rows, max_rows, max_items):
    D = h.shape[1]
    n_f = w_gate.shape[3] // MOE_TF
    item_rows = MOE_SUB * MOE_ITEM_SUBS

    def f_of(it, f, plan_ref):
        return jnp.where(it < plan_ref[0], f, n_f - 1)

    return pl.pallas_call(
        functools.partial(_expert_kernel, n_f),
        out_shape=jax.ShapeDtypeStruct((max_rows, D), F32),
        grid_spec=pltpu.PrefetchScalarGridSpec(
            num_scalar_prefetch=5,
            grid=(max_items, n_f),
            in_specs=[
                pl.BlockSpec(memory_space=pl.ANY),
                pl.BlockSpec((1, D), lambda it, f, n, e, r, s, src: (0, 0)),
                pl.BlockSpec((None, None, D, MOE_TF), lambda it, f, n, e, r, s, src: (0, e[it], 0, f_of(it, f, n))),
                pl.BlockSpec((None, None, D, MOE_TF), lambda it, f, n, e, r, s, src: (0, e[it], 0, f_of(it, f, n))),
                pl.BlockSpec((None, None, MOE_TF, D), lambda it, f, n, e, r, s, src: (0, e[it], f_of(it, f, n), 0)),
            ],
            out_specs=pl.BlockSpec(memory_space=pl.ANY),
            scratch_shapes=[pltpu.VMEM((_gather_rows_per_item(n_f), D), F32),
                            pltpu.VMEM((item_rows, D), BF16),
                            pltpu.VMEM((item_rows, D), F32),
                            pltpu.SemaphoreType.DMA(()), pltpu.SemaphoreType.DMA(())],
        ),
        compiler_params=_cparams(2),
        name="moe_expert_ffn",
    )(plan, item_e, item_row, item_sub, src_rows, h, g.reshape(1, D), w_gate, w_up, w_down)


COMBINE_TM = 256


def _combine_kernel(pos_ref, h_ref, w_ref, g_ref, y_hbm, o_ref, buf, sem):
    base = pl.program_id(0) * COMBINE_TM

    def start(r, c):
        for k in range(TOP_K):
            _row_copy(y_hbm, pos_ref[(base + r) * TOP_K + k], buf.at[k], r, sem).start()
        return c

    lax.fori_loop(0, COMBINE_TM, start, 0, unroll=8)
    for k in range(TOP_K):
        pltpu.make_async_copy(y_hbm.at[pl.ds(0, COMBINE_TM), :], buf.at[k], sem).wait()
    w = w_ref[...]
    out = h_ref[...] + w[:, 0:1] * buf[0] + w[:, 1:2] * buf[1]
    o_ref[...] = _rms(out, g_ref[...])


def _combine_norm(h, ys, pos, top_w, g):
    T, D = h.shape
    return pl.pallas_call(
        _combine_kernel,
        out_shape=jax.ShapeDtypeStruct((T, D), F32),
        grid_spec=pltpu.PrefetchScalarGridSpec(
            num_scalar_prefetch=1,
            grid=(T // COMBINE_TM,),
            in_specs=[pl.BlockSpec((COMBINE_TM, D), lambda i, p: (i, 0)),
                      pl.BlockSpec((COMBINE_TM, TOP_K), lambda i, p: (i, 0)),
                      pl.BlockSpec((1, D), lambda i, p: (0, 0)),
                      pl.BlockSpec(memory_space=pl.ANY)],
            out_specs=pl.BlockSpec((COMBINE_TM, D), lambda i, p: (i, 0)),
            scratch_shapes=[pltpu.VMEM((TOP_K, COMBINE_TM, D), F32), pltpu.SemaphoreType.DMA(())],
        ),
        compiler_params=_cparams(1),
        name="moe_combine_norm",
    )(pos.reshape(-1), h, top_w, g.reshape(1, D), ys)


def _moe_plan(top_idx, gather_rows):
    T = top_idx.shape[0]
    n_assign = T * TOP_K
    item_rows = MOE_SUB * MOE_ITEM_SUBS
    max_rows = n_assign + N_EXPERTS * MOE_SUB
    max_items = n_assign // item_rows + N_EXPERTS
    e_flat = top_idx.reshape(-1)
    onehot = (e_flat[:, None] == jnp.arange(N_EXPERTS)[None, :]).astype(jnp.int32)
    ranks = jnp.cumsum(onehot, axis=0) - onehot
    counts = jnp.sum(onehot, axis=0)
    padded = ((counts + MOE_SUB - 1) // MOE_SUB) * MOE_SUB
    group_start = jnp.cumsum(padded) - padded
    rank = jnp.sum(ranks * onehot, axis=1)
    pos = group_start[e_flat] + rank
    src_rows = jnp.zeros((max_rows + gather_rows,), jnp.int32).at[pos].set(
        jnp.arange(n_assign, dtype=jnp.int32) // TOP_K)
    items_per_e = (padded + item_rows - 1) // item_rows
    item_start = jnp.cumsum(items_per_e) - items_per_e
    n_items = jnp.sum(items_per_e)
    it = jnp.arange(max_items)
    item_e = jnp.minimum(jnp.sum(it[:, None] >= (item_start + items_per_e)[None, :], axis=1), N_EXPERTS - 1)
    local = it - item_start[item_e]
    item_row = group_start[item_e] + local * item_rows
    item_sub = jnp.clip((padded[item_e] - local * item_rows) // MOE_SUB, 0, MOE_ITEM_SUBS)
    live = it < n_items
    last = jnp.maximum(n_items - 1, 0)
    item_e = jnp.where(live, item_e, item_e[last])
    item_row = jnp.where(live, item_row, 0)
    item_sub = jnp.where(live, item_sub, 0)
    i32 = lambda a: a.astype(jnp.int32)
    plan = jnp.stack([n_items, jnp.sum(padded)])
    return (i32(pos.reshape(T, TOP_K)), src_rows, i32(plan), i32(item_e), i32(item_row),
            i32(item_sub), max_rows, max_items)


def _moe_and_final_norm(h, g_ffn, w_router, w_gate, w_up, w_down, g_final):
    top_idx, top_w = _router(h, g_ffn, w_router)
    gather_rows = _gather_rows_per_item(w_gate.shape[3] // MOE_TF)
    pos, src_rows, plan, item_e, item_row, item_sub, max_rows, max_items = _moe_plan(top_idx, gather_rows)
    ys = _expert_ffn(h, g_ffn, w_gate, w_up, w_down, plan, item_e, item_row, item_sub, src_rows, max_rows,
                     max_items)
    return _combine_norm(h, ys, pos, top_w, g_final)


def kernel(x, w_in, w_branch, w_out, norm_mix, norm_ffn, norm_final, sinks, mla_q_norm, mla_kv_norm, mla_w_uq,
           mla_w_ukv, ffn_w_gate, ffn_w_up, ffn_w_down, router_w, moe_w_gate, moe_w_up, moe_w_down):
    nb, seq, d = x.shape
    assert (seq, d) == (SEQ, D_MODEL)
    depth = w_in.shape[0]
    assert depth == 2, "layer 0 uses the dense FFN, layer 1 the MoE followed by the final norm"
    cos64, sin64 = _rope_tables(HEAD_DIM, 0, HEAD_DIM)
    cos32, sin32 = _rope_tables(D_ROPE, D_NOPE, LANES)
    h = x.reshape(nb * seq, d)
    out = None
    for layer in range(depth):
        n = _rmsnorm(h, norm_mix[layer], BF16)
        proj = _in_proj(n, w_in, layer, cos64, sin64)
        o_a = _dilated_attention(proj, nb)
        o_b = _sink_attention(proj, sinks[layer], nb)
        o_c = _stick_attention(proj, nb)
        q_d, k_d, v_d = _mla_prep(proj, mla_q_norm[layer], mla_kv_norm[layer], mla_w_uq[layer], mla_w_ukv[layer],
                                  cos32, sin32)
        o_d = _mla_attention(q_d, k_d, v_d, nb)
        w_gates = _gate_weights(w_in, layer)
        merged = _merge(n, (o_a, o_b, o_c, o_d), w_gates, w_branch, layer)
        h = _matmul_res(merged, w_out, layer, h, 1024, 512)
        if layer == 0:
            n2 = _rmsnorm(h, norm_ffn[layer], BF16)
            mid = _swiglu_up(n2, ffn_w_gate, ffn_w_up, 0)
            h = _matmul_res(mid, ffn_w_down, 0, h, 1024, 256)
        else:
            out = _moe_and_final_norm(h, norm_ffn[layer], router_w[0], moe_w_gate, moe_w_up, moe_w_down, norm_final)
    return out.reshape(nb, seq, d)
```

```python
import functools

import numpy as np
import jax
import jax.numpy as jnp
from jax import lax
from jax.experimental import pallas as pl
from jax.experimental.pallas import tpu as pltpu

F32 = jnp.float32
BF16 = jnp.bfloat16

D_MODEL = 2048
SEQ = 2048
HEAD_DIM = 64
BLOCK = 128
LANES = 128
ROPE_THETA = 10000.0
NORM_EPS = 1e-6
DIL_PATTERNS = ((128, 1), (512, 4), (2048, 16))
B_WINDOW = 128
D_NOPE = 64
D_ROPE = 32
Q_LORA = 384
KV_LORA = 256
N_BRANCH = 4
BRANCH_WIDTH = 512
N_EXPERTS = 8
TOP_K = 2

A_Q0, A_K0, A_V0 = 0, 12, 24
B_Q0, B_K0, B_V0 = 36, 40, 41
C_Q0, C_K0, C_V0 = 42, 46, 50
MLA_COL0 = 6912
MLA_BLOCK_W = 768
QKV_WIDTH = 7680
GATE_COL0 = 7584
PROJ_TN = 256

VMEM_LIMIT = 56 * 1024 * 1024
NEG = -1e30
NT_DIMS = (((1,), (1,)), ((), ()))
LOG2_E = 1.4426950408889634

MOE_SUB = 128
MOE_ITEM_SUBS = 8
MOE_TF = 256


def _cparams(n_axes):
    return pltpu.CompilerParams(dimension_semantics=("arbitrary",) * n_axes, vmem_limit_bytes=VMEM_LIMIT)


def _lane_half():
    return lax.broadcasted_iota(jnp.int32, (BLOCK, LANES), 1) < HEAD_DIM


def _rms(x, g):
    return x * lax.rsqrt(jnp.mean(x * x, axis=-1, keepdims=True) + NORM_EPS) * g


def _rmsnorm_kernel(x_ref, g_ref, o_ref):
    o_ref[...] = _rms(x_ref[...], g_ref[...]).astype(o_ref.dtype)


def _rmsnorm(x, g, out_dtype, tm=512):
    T, D = x.shape
    return pl.pallas_call(
        _rmsnorm_kernel,
        out_shape=jax.ShapeDtypeStruct((T, D), out_dtype),
        grid=(T // tm,),
        in_specs=[pl.BlockSpec((tm, D), lambda i: (i, 0)), pl.BlockSpec((1, D), lambda i: (0, 0))],
        out_specs=pl.BlockSpec((tm, D), lambda i: (i, 0)),
        compiler_params=_cparams(1),
        name="rmsnorm",
    )(x, g.reshape(1, D))


def _rope_tables(dh, lane0, period):
    half = dh // 2
    freqs = ROPE_THETA ** (-2.0 * jnp.arange(half, dtype=F32) / dh)
    ang = jnp.arange(SEQ, dtype=F32)[:, None] * freqs[None, :]
    cos, sin = jnp.cos(ang), jnp.sin(ang)
    cos_h = jnp.concatenate([cos, cos], axis=1)
    sin_h = jnp.concatenate([-sin, sin], axis=1)
    ones = jnp.ones((SEQ, period - dh), F32)
    cos_p = jnp.concatenate([cos_h, ones], axis=1)
    sin_p = jnp.concatenate([sin_h, 0.0 * ones], axis=1)
    reps = LANES // period
    cos_t = jnp.roll(jnp.tile(cos_p, (1, reps)), lane0, axis=1)
    sin_t = jnp.roll(jnp.tile(sin_p, (1, reps)), lane0, axis=1)
    return cos_t, sin_t


def _rope_lanes(x, cos, sin, half, first_half_mask):
    fwd = pltpu.roll(x, LANES - half, 1)
    bwd = pltpu.roll(x, half, 1)
    return x * cos + jnp.where(first_half_mask, fwd, bwd) * sin


def _proj_block_has_rope(j, half):
    rope_both = (j < 12) | ((j >= 18) & (j < 20))
    return rope_both | (j == 20) if half == 0 else rope_both


def _proj_kernel(x_ref, w_ref, cos_a, sin_a, cos_b, sin_b, o_ref):
    j = pl.program_id(1)
    has_rope = _proj_block_has_rope(j, 0)

    @pl.when(has_rope)
    def _():
        acc = lax.dot_general(x_ref[...], w_ref[...].astype(BF16), NT_DIMS, preferred_element_type=F32)
        lane = lax.broadcasted_iota(jnp.int32, (SEQ, LANES), 1)
        first = (lane % HEAD_DIM) < HEAD_DIM // 2
        o_ref[:, :LANES] = _rope_lanes(acc[:, :LANES], cos_a[...], sin_a[...], HEAD_DIM // 2, first)
        o_ref[:, LANES:] = _rope_lanes(acc[:, LANES:], cos_b[...], sin_b[...], HEAD_DIM // 2, first)

    @pl.when(jnp.logical_not(has_rope))
    def _():
        o_ref[...] = lax.dot_general(x_ref[...], w_ref[...].astype(BF16), NT_DIMS, preferred_element_type=F32)


def _in_proj(n, w_in_t, layer, cos, sin):
    T, D = n.shape
    nb = T // SEQ
    tables = (jnp.stack([cos, jnp.ones_like(cos)]), jnp.stack([sin, jnp.zeros_like(sin)]))

    def variant(j, half):
        return jnp.where(_proj_block_has_rope(j, half), 0, 1)

    table_specs = [pl.BlockSpec((None, SEQ, LANES), lambda b, j, half=half: (variant(j, half), 0, 0))
                   for half in (0, 0, 1, 1)]
    return pl.pallas_call(
        _proj_kernel,
        out_shape=jax.ShapeDtypeStruct((T, QKV_WIDTH), F32),
        grid=(nb, QKV_WIDTH // PROJ_TN),
        in_specs=[
            pl.BlockSpec((SEQ, D), lambda b, j: (b, 0)),
            pl.BlockSpec((None, PROJ_TN, D), lambda b, j: (layer, j, 0)),
        ] + table_specs,
        out_specs=pl.BlockSpec((SEQ, PROJ_TN), lambda b, j: (b, j)),
        compiler_params=_cparams(2),
        name="in_proj_rope",
    )(n, w_in_t, tables[0], tables[1], tables[0], tables[1])


BAND_UNROLL = 8


def _band_scores(q, kcat, vcat, head, has_prev, strict_prev):
    half = _lane_half()
    qh = jnp.where(half if head == 0 else jnp.logical_not(half), q * (HEAD_DIM ** -0.5), 0.0).astype(BF16)
    s = lax.dot_general(qh, kcat, (((1,), (1,)), ((), ())), preferred_element_type=F32)
    row = lax.broadcasted_iota(jnp.int32, (BLOCK, 2 * BLOCK), 0)
    col = lax.broadcasted_iota(jnp.int32, (BLOCK, 2 * BLOCK), 1)
    dist = row + BLOCK - col
    max_dist = BLOCK - 1 if strict_prev else BLOCK
    first_col = jnp.where(has_prev, 0, BLOCK)
    valid = (dist >= 0) & (dist <= max_dist) & (col >= first_col)
    s = jnp.where(valid, s, NEG)
    m = jnp.max(s, axis=1, keepdims=True)
    p = jnp.exp(s - m)
    l = jnp.sum(p, axis=1, keepdims=True)
    acc = jnp.dot(p.astype(BF16), vcat, preferred_element_type=F32)
    return acc, m, l


def _pair(a0, a1):
    return jnp.where(_lane_half(), a0, a1)


def _dilated_kernel(*refs):
    q_refs, k_refs, v_refs = refs[0:3], refs[3:6], refs[6:9]
    o_ref = refs[9]
    acc_sc, m_sc, l_sc = refs[10:13]
    n_blocks = SEQ // BLOCK

    for g, (win, dil) in enumerate(DIL_PATTERNS):
        assert win // dil == BLOCK and SEQ % (dil * BLOCK) == 0
        q_ref, k_ref, v_ref = q_refs[g], k_refs[g], v_refs[g]

        def rows(start, dil=dil):
            return pl.ds(start, BLOCK) if dil == 1 else pl.ds(start, BLOCK, stride=dil)

        def block(it, g=g, dil=dil, q_ref=q_ref, k_ref=k_ref, v_ref=v_ref, rows=rows):
            r = it % dil
            i = it // dil
            base = i * (BLOCK * dil) + r
            prev = jnp.maximum(base - BLOCK * dil, 0)
            q = q_ref[rows(base), :]
            kcat = jnp.concatenate([k_ref[rows(prev), :], k_ref[rows(base), :]], axis=0).astype(BF16)
            vcat = jnp.concatenate([v_ref[rows(prev), :], v_ref[rows(base), :]], axis=0).astype(BF16)
            a0, m0, l0 = _band_scores(q, kcat, vcat, 0, i > 0, False)
            a1, m1, l1 = _band_scores(q, kcat, vcat, 1, i > 0, False)
            acc_sc[g, rows(base), :] = _pair(a0, a1)
            m_sc[g, rows(base), :] = _pair(m0, m1)
            l_sc[g, rows(base), :] = _pair(l0, l1)

        def body(it, carry, block=block):
            for u in range(BAND_UNROLL):
                block(it + u * (n_blocks // BAND_UNROLL))
            return carry

        lax.fori_loop(0, n_blocks // BAND_UNROLL, body, 0)

    def combine(i, carry):
        rows = pl.ds(pl.multiple_of(i * BLOCK, BLOCK), BLOCK)
        m = [m_sc[g, rows, :] for g in range(len(DIL_PATTERNS))]
        m_all = jnp.maximum(jnp.maximum(m[0], m[1]), m[2])
        num = jnp.zeros((BLOCK, LANES), F32)
        den = jnp.zeros((BLOCK, LANES), F32)
        for g in range(len(DIL_PATTERNS)):
            a = jnp.exp(m[g] - m_all)
            num = num + a * acc_sc[g, rows, :]
            den = den + a * l_sc[g, rows, :]
        o_ref[rows, :] = (num / den).astype(o_ref.dtype)
        return carry

    lax.fori_loop(0, n_blocks, combine, 0)


def _dilated_attention(proj, nb):
    T = proj.shape[0]
    specs = []
    for col0 in (A_Q0, A_K0, A_V0):
        for g in range(len(DIL_PATTERNS)):
            specs.append(pl.BlockSpec((SEQ, LANES), lambda b, hp, c=col0 + 4 * g: (b, c + hp)))
    return pl.pallas_call(
        _dilated_kernel,
        out_shape=jax.ShapeDtypeStruct((T, BRANCH_WIDTH), BF16),
        grid=(nb, BRANCH_WIDTH // LANES),
        in_specs=specs,
        out_specs=pl.BlockSpec((SEQ, LANES), lambda b, hp: (b, hp)),
        scratch_shapes=[pltpu.VMEM((len(DIL_PATTERNS), SEQ, LANES), F32)] * 3,
        compiler_params=_cparams(2),
        name="dilated_attention",
    )(*([proj] * 9))


def _sink_kernel(q_ref, k_ref, v_ref, sink_ref, o_ref, k_sc, v_sc):
    hp = pl.program_id(1)
    kv_head = hp // 2
    k = k_ref[...]
    v = v_ref[...]
    k_sw = pltpu.roll(k, HEAD_DIM, 1)
    v_sw = pltpu.roll(v, HEAD_DIM, 1)
    first = kv_head == 0
    k_sc[0] = jnp.where(first, k, k_sw).astype(BF16)
    k_sc[1] = jnp.where(first, k_sw, k).astype(BF16)
    v_sc[0] = jnp.where(first, v, v_sw).astype(BF16)
    v_sc[1] = jnp.where(first, v_sw, v).astype(BF16)
    sink = _pair(sink_ref[pl.ds(2 * hp, 1), :], sink_ref[pl.ds(2 * hp + 1, 1), :])

    def block(i):
        base = pl.multiple_of(i * BLOCK, BLOCK)
        prev = pl.multiple_of(jnp.maximum(base - BLOCK, 0), BLOCK)
        q = q_ref[pl.ds(base, BLOCK), :]
        out = []
        for a in range(2):
            kcat = jnp.concatenate([k_sc[a, pl.ds(prev, BLOCK), :], k_sc[a, pl.ds(base, BLOCK), :]], axis=0)
            vcat = jnp.concatenate([v_sc[a, pl.ds(prev, BLOCK), :], v_sc[a, pl.ds(base, BLOCK), :]], axis=0)
            out.append(_band_scores(q, kcat, vcat, a, i > 0, True))
        acc = _pair(out[0][0], out[1][0])
        m = _pair(out[0][1], out[1][1])
        l = _pair(out[0][2], out[1][2])
        m_new = jnp.maximum(m, sink)
        scale = jnp.exp(m - m_new)
        den = l * scale + jnp.exp(sink - m_new)
        o_ref[pl.ds(base, BLOCK), :] = (acc * scale / den).astype(o_ref.dtype)

    n_blocks = SEQ // BLOCK

    def body(it, carry):
        for u in range(BAND_UNROLL):
            block(it + u * (n_blocks // BAND_UNROLL))
        return carry

    lax.fori_loop(0, n_blocks // BAND_UNROLL, body, 0)


def _sink_attention(proj, sinks_l, nb):
    T = proj.shape[0]
    sink_tab = jnp.broadcast_to(sinks_l.astype(F32)[:, None], (8, LANES))
    return pl.pallas_call(
        _sink_kernel,
        out_shape=jax.ShapeDtypeStruct((T, BRANCH_WIDTH), BF16),
        grid=(nb, BRANCH_WIDTH // LANES),
        in_specs=[
            pl.BlockSpec((SEQ, LANES), lambda b, hp: (b, B_Q0 + hp)),
            pl.BlockSpec((SEQ, LANES), lambda b, hp: (b, B_K0)),
            pl.BlockSpec((SEQ, LANES), lambda b, hp: (b, B_V0)),
            pl.BlockSpec((8, LANES), lambda b, hp: (0, 0)),
        ],
        out_specs=pl.BlockSpec((SEQ, LANES), lambda b, hp: (b, hp)),
        scratch_shapes=[pltpu.VMEM((2, SEQ, LANES), BF16)] * 2,
        compiler_params=_cparams(2),
        name="sink_attention",
    )(proj, proj, proj, sink_tab)


STICK_TQ = 256
STICK_SUBS = STICK_TQ // BLOCK
STICK_STAGES = 5


def _stick_tiles():
    n_key_blocks, n_q_tiles = SEQ // BLOCK, SEQ // STICK_TQ
    tiles = []
    for j in reversed(range(n_key_blocks)):
        first = j // STICK_SUBS
        tiles.append((first, j, 1 + j % STICK_SUBS))
        tiles += [(qi, j, 0) for qi in range(first + 1, n_q_tiles)]
    idle = (0, n_key_blocks - 1, STICK_SUBS + 1)
    lag = STICK_STAGES - 1
    stages = [[idle] * s + tiles + [idle] * (lag - s) for s in range(STICK_STAGES)]
    table = np.asarray(stages, np.int32)
    return table.shape[1], [np.ascontiguousarray(table[:, :, c]).reshape(-1) for c in range(3)]


def _stick_masks():
    row = np.arange(STICK_TQ)[:, None]
    col = np.arange(BLOCK)[None, :]
    valid = [np.ones((STICK_TQ, BLOCK), bool)]
    valid += [col + r * BLOCK < row for r in range(STICK_SUBS)]
    valid += [np.zeros((STICK_TQ, BLOCK), bool)]
    valid = np.stack(valid)
    masks = np.stack([valid.astype(np.float32), np.where(valid, 0.0, NEG).astype(np.float32)], axis=1)
    return masks.reshape(-1, STICK_TQ, BLOCK)


def _stick_kernel(n_slots, tq_ref, tk_ref, tm_ref, q_ref, k_ref, v_ref, tri_ref, mask_ref, o_ref,
                  q_sc, kd_sc, vd_sc, carry_sc, acc_sc, z_buf, hl_buf, sums_buf, w_buf):
    half = _lane_half()
    q_sc[...] = (q_ref[...] * (HEAD_DIM ** -0.5)).astype(BF16)
    n_key_blocks = SEQ // BLOCK
    k = k_ref[...]
    v = v_ref[...]
    for j in range(n_key_blocks):
        rows = slice(j * BLOCK, (j + 1) * BLOCK)
        kd_sc[j, :BLOCK] = jnp.where(half, k[rows], 0.0).astype(BF16)
        kd_sc[j, BLOCK:] = jnp.where(half, 0.0, k[rows]).astype(BF16)
        vd_sc[j, :BLOCK] = jnp.where(half, v[rows], 0.0).astype(BF16)
        vd_sc[j, BLOCK:] = jnp.where(half, 0.0, v[rows]).astype(BF16)
    carry_sc[...] = jnp.zeros_like(carry_sc)
    acc_sc[...] = jnp.zeros_like(acc_sc)
    z_buf[...] = jnp.zeros_like(z_buf)
    hl_buf[...] = jnp.zeros_like(hl_buf)
    sums_buf[...] = jnp.zeros_like(sums_buf)
    w_buf[...] = jnp.zeros_like(w_buf)
    tri = tri_ref[...]

    def q_rows(qi):
        return pl.ds(pl.multiple_of(qi * STICK_TQ, STICK_TQ), STICK_TQ)

    def trip(t, phase):
        cur, other = phase % 2, (phase + 1) % 2
        i = 4 * n_slots + t
        acc_sc[q_rows(tq_ref[i]), :] += jnp.dot(w_buf[cur], vd_sc[tk_ref[i]], preferred_element_type=F32)
        for a in range(2):
            sums_buf[2 * cur + a] = jnp.dot(hl_buf[2 * cur + a], tri, preferred_element_type=F32)
        z_buf[phase] = lax.dot_general(q_sc[q_rows(tq_ref[t]), :], kd_sc[tk_ref[t]], NT_DIMS,
                                       preferred_element_type=F32)
        i = 3 * n_slots + t
        rows = q_rows(tq_ref[i])
        bias = mask_ref[2 * tm_ref[i] + 1]
        z = z_buf[(phase + 1) % 4]
        w = []
        for a in range(2):
            sums = sums_buf[2 * other + a]
            carry = carry_sc[a, rows, :]
            w.append(jnp.exp(z[:, a * BLOCK:(a + 1) * BLOCK] + carry + sums[:, :BLOCK] + bias))
            carry_sc[a, rows, :] = carry + sums[:, BLOCK:]
        w_buf[other] = jnp.concatenate(w, axis=1).astype(BF16)
        keep = mask_ref[2 * tm_ref[n_slots + t]]
        z = z_buf[(phase + 3) % 4]
        for a in range(2):
            za = z[:, a * BLOCK:(a + 1) * BLOCK]
            sp = (jnp.maximum(za, 0.0) + jnp.log(1.0 + jnp.exp(-jnp.abs(za)))) * keep
            hi = sp.astype(BF16)
            lo = (sp - hi.astype(F32)).astype(BF16)
            hl_buf[2 * other + a] = jnp.concatenate([hi, lo], axis=1)

    def body(i, c):
        for phase in range(4):
            trip(4 * i + phase, phase)
        return c

    assert n_slots % 4 == 0
    lax.fori_loop(0, n_slots // 4, body, 0)
    o_ref[...] = acc_sc[...].astype(o_ref.dtype)


def _stick_attention(proj, nb):
    T = proj.shape[0]
    jp = np.arange(BLOCK)[:, None]
    s = np.arange(BLOCK)[None, :]
    tri = -np.concatenate([(jp >= s), np.ones((BLOCK, BLOCK), bool)], axis=1).astype(np.float32)
    tri = jnp.asarray(np.concatenate([tri, tri], axis=0), BF16)
    n_slots, tables = _stick_tiles()
    masks = _stick_masks()
    n_key_blocks = SEQ // BLOCK
    return pl.pallas_call(
        functools.partial(_stick_kernel, n_slots),
        out_shape=jax.ShapeDtypeStruct((T, BRANCH_WIDTH), BF16),
        grid_spec=pltpu.PrefetchScalarGridSpec(
            num_scalar_prefetch=3,
            grid=(nb, BRANCH_WIDTH // LANES),
            in_specs=[
                pl.BlockSpec((SEQ, LANES), lambda b, hp, *_: (b, C_Q0 + hp)),
                pl.BlockSpec((SEQ, LANES), lambda b, hp, *_: (b, C_K0 + hp)),
                pl.BlockSpec((SEQ, LANES), lambda b, hp, *_: (b, C_V0 + hp)),
                pl.BlockSpec((2 * BLOCK, 2 * BLOCK), lambda b, hp, *_: (0, 0)),
                pl.BlockSpec(masks.shape, lambda b, hp, *_: (0, 0, 0)),
            ],
            out_specs=pl.BlockSpec((SEQ, LANES), lambda b, hp, *_: (b, hp)),
            scratch_shapes=[
                pltpu.VMEM((SEQ, LANES), BF16),
                pltpu.VMEM((n_key_blocks, 2 * BLOCK, LANES), BF16),
                pltpu.VMEM((n_key_blocks, 2 * BLOCK, LANES), BF16),
                pltpu.VMEM((2, SEQ, LANES), F32),
                pltpu.VMEM((SEQ, LANES), F32),
                pltpu.VMEM((4, STICK_TQ, 2 * BLOCK), F32),
                pltpu.VMEM((4, STICK_TQ, 2 * BLOCK), BF16),
                pltpu.VMEM((4, STICK_TQ, 2 * BLOCK), F32),
                pltpu.VMEM((2, STICK_TQ, 2 * BLOCK), BF16),
            ],
        ),
        compiler_params=_cparams(2),
        name="stick_breaking_attention",
    )(*[jnp.asarray(t) for t in tables], proj, proj, proj, tri, jnp.asarray(masks))


MLA_HEADS = 8
MLA_TM = 512


def _mla_prep_kernel(c_ref, gq_ref, gkv_ref, wq_ref, wk_ref, wv_ref, cos_ref, sin_ref, q_ref, k_ref, v_ref):
    c = c_ref[...]
    cq = _rms(c[:, :Q_LORA], gq_ref[...]).astype(BF16)
    ckv = _rms(c[:, Q_LORA:Q_LORA + KV_LORA], gkv_ref[...]).astype(BF16)
    q = jnp.dot(cq, wq_ref[...].astype(BF16), preferred_element_type=F32)
    k = jnp.dot(ckv, wk_ref[...].astype(BF16), preferred_element_type=F32)
    v_ref[...] = jnp.dot(ckv, wv_ref[...].astype(BF16), preferred_element_type=F32).astype(BF16)
    cos, sin = cos_ref[...], sin_ref[...]
    lane = lax.broadcasted_iota(jnp.int32, (MLA_TM, LANES), 1)
    rope_lane = (lane >= D_NOPE) & (lane < D_NOPE + D_ROPE)
    first = lane < D_NOPE + D_ROPE // 2
    kr = pltpu.roll(c[:, Q_LORA + KV_LORA:], D_NOPE, 1)
    kr = jnp.where(rope_lane, _rope_lanes(kr, cos, sin, D_ROPE // 2, first), 0.0)
    scale = (D_NOPE + D_ROPE) ** -0.5 * LOG2_E
    for h in range(MLA_HEADS):
        sl = slice(h * LANES, (h + 1) * LANES)
        q_ref[:, sl] = (_rope_lanes(q[:, sl], cos, sin, D_ROPE // 2, first) * scale).astype(BF16)
        k_ref[:, sl] = (k[:, sl] + kr).astype(BF16)


def _mla_prep(proj, g_qa, g_kva, w_uq, w_ukv, cos, sin):
    T = proj.shape[0]
    wq = jnp.pad(w_uq.reshape(Q_LORA, MLA_HEADS, D_NOPE + D_ROPE), ((0, 0), (0, 0), (0, LANES - D_NOPE - D_ROPE)))
    wq = wq.reshape(Q_LORA, MLA_HEADS * LANES)
    wkv = w_ukv.reshape(KV_LORA, MLA_HEADS, 2 * HEAD_DIM)
    wk = jnp.pad(wkv[:, :, :D_NOPE], ((0, 0), (0, 0), (0, LANES - D_NOPE))).reshape(KV_LORA, MLA_HEADS * LANES)
    wv = wkv[:, :, D_NOPE:].reshape(KV_LORA, MLA_HEADS * HEAD_DIM)
    full = lambda shape: pl.BlockSpec(shape, lambda i: (0, 0))
    per_seq = SEQ // MLA_TM
    return pl.pallas_call(
        _mla_prep_kernel,
        out_shape=[jax.ShapeDtypeStruct((T, MLA_HEADS * LANES), BF16),
                   jax.ShapeDtypeStruct((T, MLA_HEADS * LANES), BF16),
                   jax.ShapeDtypeStruct((T, MLA_HEADS * HEAD_DIM), BF16)],
        grid=(T // MLA_TM,),
        in_specs=[
            pl.BlockSpec((MLA_TM, MLA_BLOCK_W), lambda i: (i, MLA_COL0 // MLA_BLOCK_W)),
            full((1, Q_LORA)), full((1, KV_LORA)),
            full(wq.shape), full(wk.shape), full(wv.shape),
            pl.BlockSpec((MLA_TM, LANES), lambda i: (i % per_seq, 0)),
            pl.BlockSpec((MLA_TM, LANES), lambda i: (i % per_seq, 0)),
        ],
        out_specs=[pl.BlockSpec((MLA_TM, MLA_HEADS * LANES), lambda i: (i, 0)),
                   pl.BlockSpec((MLA_TM, MLA_HEADS * LANES), lambda i: (i, 0)),
                   pl.BlockSpec((MLA_TM, MLA_HEADS * HEAD_DIM), lambda i: (i, 0))],
        compiler_params=_cparams(1),
        name="mla_prep",
    )(proj, g_qa.reshape(1, Q_LORA), g_kva.reshape(1, KV_LORA), wq, wk, wv, cos, sin)


MLA_TQ = 1024
MLA_TK = 256


def _mla_attn_kernel(q_ref, k_ref, v_ref, o_ref, v_sc, m_sc, acc_sc):
    half = lax.broadcasted_iota(jnp.int32, (SEQ, LANES), 1) < HEAD_DIM
    v = v_ref[...]
    one = jnp.ones_like(v)
    v_sc[0] = jnp.where(half, v, one)
    v_sc[1] = jnp.where(half, one, v)
    lane_half = lax.broadcasted_iota(jnp.int32, (MLA_TQ, LANES), 1) < HEAD_DIM

    def step(qb, kb, diagonal):
        for a in range(2):
            q = q_ref[pl.ds(qb, MLA_TQ), a * LANES:(a + 1) * LANES]
            k = k_ref[pl.ds(kb, MLA_TK), a * LANES:(a + 1) * LANES]
            s = lax.dot_general(q, k, (((1,), (1,)), ((), ())), preferred_element_type=F32)
            if diagonal:
                row = lax.broadcasted_iota(jnp.int32, (MLA_TQ, MLA_TK), 0)
                col = lax.broadcasted_iota(jnp.int32, (MLA_TQ, MLA_TK), 1)
                s = jnp.where(col + (kb - qb) <= row, s, NEG)
            m = m_sc[a]
            m_new = jnp.maximum(m, jnp.max(s, axis=1, keepdims=True))
            alpha = jnp.exp2(m - m_new)
            p = jnp.exp2(s - jnp.concatenate([m_new] * (MLA_TK // LANES), axis=1))
            pv = jnp.dot(p.astype(BF16), v_sc[a, pl.ds(kb, MLA_TK), :], preferred_element_type=F32)
            acc_sc[a] = alpha * acc_sc[a] + pv
            m_sc[a] = m_new

    def q_block(qi, c):
        qb = pl.multiple_of(qi * MLA_TQ, MLA_TQ)
        m_sc[...] = jnp.full(m_sc.shape, NEG, F32)
        acc_sc[...] = jnp.zeros(acc_sc.shape, F32)

        def k_block(kj, c2):
            step(qb, pl.multiple_of(kj * MLA_TK, MLA_TK), False)
            return c2

        lax.fori_loop(0, qi * (MLA_TQ // MLA_TK), k_block, 0)
        for d in range(MLA_TQ // MLA_TK):
            step(qb, qb + d * MLA_TK, True)
        outs = [acc_sc[a] / pltpu.roll(acc_sc[a], HEAD_DIM, 1) for a in range(2)]
        o_ref[pl.ds(qb, MLA_TQ), :] = jnp.where(lane_half, outs[0], outs[1]).astype(o_ref.dtype)
        return c

    lax.fori_loop(0, SEQ // MLA_TQ, q_block, 0)


def _mla_attention(q, k, v, nb):
    T = q.shape[0]
    return pl.pallas_call(
        _mla_attn_kernel,
        out_shape=jax.ShapeDtypeStruct((T, BRANCH_WIDTH), BF16),
        grid=(nb, BRANCH_WIDTH // LANES),
        in_specs=[
            pl.BlockSpec((SEQ, 2 * LANES), lambda b, hp: (b, hp)),
            pl.BlockSpec((SEQ, 2 * LANES), lambda b, hp: (b, hp)),
            pl.BlockSpec((SEQ, LANES), lambda b, hp: (b, hp)),
        ],
        out_specs=pl.BlockSpec((SEQ, LANES), lambda b, hp: (b, hp)),
        scratch_shapes=[pltpu.VMEM((2, SEQ, LANES), BF16),
                        pltpu.VMEM((2, MLA_TQ, LANES), F32),
                        pltpu.VMEM((2, MLA_TQ, LANES), F32)],
        compiler_params=_cparams(2),
        name="mla_attention",
    )(q, k, v)


MERGE_TM = 1024
MERGE_TN = 256


def _merge_kernel(*refs):
    n_ref = refs[0]
    o_refs = refs[1:5]
    wg_refs = refs[5:9]
    wb_refs = refs[9:13]
    out_ref = refs[13]
    n = n_ref[...]
    acc = jnp.zeros((MERGE_TM, MERGE_TN), F32)
    for i in range(N_BRANCH):
        gate = lax.dot_general(n, wg_refs[i][...], NT_DIMS, preferred_element_type=F32)
        br = jnp.dot(o_refs[i][...], wb_refs[i][...].astype(BF16), preferred_element_type=F32)
        acc = acc + br * jax.nn.sigmoid(gate)
    out_ref[...] = acc.astype(out_ref.dtype)


def _merge(n, branches, w_gates_t, w_branch, layer):
    T, D = n.shape
    nblk = D // MERGE_TN
    in_specs = [pl.BlockSpec((MERGE_TM, D), lambda m, j: (m, 0))]
    in_specs += [pl.BlockSpec((MERGE_TM, BRANCH_WIDTH), lambda m, j: (m, 0))] * N_BRANCH
    in_specs += [pl.BlockSpec((MERGE_TN, D), lambda m, j, i=i: (i * nblk + j, 0)) for i in range(N_BRANCH)]
    in_specs += [pl.BlockSpec((None, None, BRANCH_WIDTH, MERGE_TN), lambda m, j, i=i: (layer, i, 0, j))
                 for i in range(N_BRANCH)]
    return pl.pallas_call(
        _merge_kernel,
        out_shape=jax.ShapeDtypeStruct((T, D), BF16),
        grid=(T // MERGE_TM, nblk),
        in_specs=in_specs,
        out_specs=pl.BlockSpec((MERGE_TM, MERGE_TN), lambda m, j: (m, j)),
        compiler_params=_cparams(2),
        name="gated_merge",
    )(n, *branches, *([w_gates_t] * N_BRANCH), *([w_branch] * N_BRANCH))


def _matmul_res_kernel(a_ref, w_ref, r_ref, o_ref):
    o_ref[...] = r_ref[...] + jnp.dot(a_ref[...], w_ref[...].astype(BF16), preferred_element_type=F32)


def _matmul_res(a, w, layer, res, tm, tn):
    M, K = a.shape
    N = w.shape[2]
    return pl.pallas_call(
        _matmul_res_kernel,
        out_shape=jax.ShapeDtypeStruct((M, N), F32),
        grid=(M // tm, N // tn),
        in_specs=[pl.BlockSpec((tm, K), lambda m, j: (m, 0)),
                  pl.BlockSpec((None, K, tn), lambda m, j: (layer, 0, j)),
                  pl.BlockSpec((tm, tn), lambda m, j: (m, j))],
        out_specs=pl.BlockSpec((tm, tn), lambda m, j: (m, j)),
        compiler_params=_cparams(2),
        name="matmul_residual",
    )(a, w, res)


def _swiglu_up_kernel(x_ref, wg_ref, wu_ref, o_ref):
    x = x_ref[...]
    g = jnp.dot(x, wg_ref[...].astype(BF16), preferred_element_type=F32)
    u = jnp.dot(x, wu_ref[...].astype(BF16), preferred_element_type=F32)
    o_ref[...] = (jax.nn.silu(g) * u).astype(o_ref.dtype)


def _swiglu_up(x, wg, wu, layer, tm=1024, tn=512):
    M, K = x.shape
    N = wg.shape[2]
    return pl.pallas_call(
        _swiglu_up_kernel,
        out_shape=jax.ShapeDtypeStruct((M, N), BF16),
        grid=(M // tm, N // tn),
        in_specs=[pl.BlockSpec((tm, K), lambda m, j: (m, 0)),
                  pl.BlockSpec((None, K, tn), lambda m, j: (layer, 0, j)),
                  pl.BlockSpec((None, K, tn), lambda m, j: (layer, 0, j))],
        out_specs=pl.BlockSpec((tm, tn), lambda m, j: (m, j)),
        compiler_params=_cparams(2),
        name="swiglu_up",
    )(x, wg, wu)


ROUTER_TM = 512


def _router_kernel(h_ref, g_ref, wr_ref, idx_ref, w_ref):
    n = _rms(h_ref[...], g_ref[...])
    logits = jnp.dot(n, wr_ref[...], preferred_element_type=F32, precision=lax.Precision.HIGHEST)
    lane = lax.broadcasted_iota(jnp.int32, logits.shape, 1)
    v1 = jnp.max(logits, axis=1, keepdims=True)
    i1 = jnp.min(jnp.where(logits == v1, lane, N_EXPERTS), axis=1, keepdims=True)
    rest = jnp.where(lane == i1, -jnp.inf, logits)
    v2 = jnp.max(rest, axis=1, keepdims=True)
    i2 = jnp.min(jnp.where(rest == v2, lane, N_EXPERTS), axis=1, keepdims=True)
    e2 = jnp.exp(v2 - v1)
    den = 1.0 + e2
    two = lax.broadcasted_iota(jnp.int32, (ROUTER_TM, TOP_K), 1)
    idx_ref[...] = jnp.where(two == 0, i1, i2)
    w_ref[...] = jnp.where(two == 0, 1.0 / den, e2 / den)


def _router(h, g, w_router):
    T, D = h.shape
    return pl.pallas_call(
        _router_kernel,
        out_shape=[jax.ShapeDtypeStruct((T, TOP_K), jnp.int32), jax.ShapeDtypeStruct((T, TOP_K), F32)],
        grid=(T // ROUTER_TM,),
        in_specs=[pl.BlockSpec((ROUTER_TM, D), lambda i: (i, 0)),
                  pl.BlockSpec((1, D), lambda i: (0, 0)),
                  pl.BlockSpec((D, N_EXPERTS), lambda i: (0, 0))],
        out_specs=[pl.BlockSpec((ROUTER_TM, TOP_K), lambda i: (i, 0)),
                   pl.BlockSpec((ROUTER_TM, TOP_K), lambda i: (i, 0))],
        compiler_params=_cparams(1),
        name="moe_router",
    )(h, g.reshape(1, D), w_router)


def _row_copy(src_hbm, row, dst_vmem, slot, sem):
    return pltpu.make_async_copy(src_hbm.at[pl.ds(row, 1), :], dst_vmem.at[pl.ds(slot, 1), :], sem)


def _expert_kernel(n_f, plan_ref, item_e_ref, item_row_ref, item_sub_ref, src_ref,
                   h_hbm, g_ref, wg_ref, wu_ref, wd_ref, y_hbm, gbuf, x_sc, acc_sc, sem_in, sem_out):
    it = pl.program_id(0)
    f = pl.program_id(1)
    n_it = pl.num_programs(0)
    n_items = plan_ref[0]
    live = it < n_items
    row0 = pl.multiple_of(item_row_ref[it], MOE_SUB)
    n_sub = item_sub_ref[it]
    item_rows = x_sc.shape[0]
    g_rows = gbuf.shape[0]
    chunk = g_rows // n_f

    def gather_wait():
        pltpu.make_async_copy(h_hbm.at[pl.ds(0, g_rows), :], gbuf, sem_in).wait()

    def y_copy(s):
        rows = pl.ds(s * MOE_SUB, MOE_SUB)
        return pltpu.make_async_copy(acc_sc.at[rows, :], y_hbm.at[pl.ds(row0 + s * MOE_SUB, MOE_SUB), :], sem_out)

    @pl.when(live & (it == 0) & (f == 0))
    def _():
        def start(r, c):
            _row_copy(h_hbm, src_ref[row0 + r], gbuf, r, sem_in).start()
            return c
        lax.fori_loop(0, g_rows, start, 0)

    @pl.when(live & (f == 0))
    def _():
        gather_wait()

        def norm(i, c):
            rows = pl.ds(pl.multiple_of(i * MOE_SUB, MOE_SUB), MOE_SUB)
            x_sc[rows, :] = _rms(gbuf[rows, :], g_ref[...]).astype(BF16)
            return c
        lax.fori_loop(0, item_rows // MOE_SUB, norm, 0)
        acc_sc[...] = jnp.zeros(acc_sc.shape, F32)

    @pl.when((it == n_items) & (it > 0) & (f == 0))
    def _():
        gather_wait()

    next_row0 = item_row_ref[jnp.minimum(it + 1, n_it - 1)]

    for n in range(1, MOE_ITEM_SUBS + 1):
        @pl.when(live & (n_sub == n))
        def _(n=n):
            lo = f * chunk
            for r in range(chunk):
                _row_copy(h_hbm, src_ref[next_row0 + lo + r], gbuf, lo + r, sem_in).start()
            rows = n * MOE_SUB
            x = x_sc[:rows, :]
            g = jnp.dot(x, wg_ref[...].astype(BF16), preferred_element_type=F32)
            u = jnp.dot(x, wu_ref[...].astype(BF16), preferred_element_type=F32)
            mid = (jax.nn.silu(g) * u).astype(BF16)
            acc_sc[:rows, :] += jnp.dot(mid, wd_ref[...].astype(BF16), preferred_element_type=F32)

    @pl.when(live & (f == n_f - 1))
    def _():
        for s in range(MOE_ITEM_SUBS):
            @pl.when(s < n_sub)
            def _(s=s):
                y_copy(s).start()
        for s in range(MOE_ITEM_SUBS):
            @pl.when(s < n_sub)
            def _(s=s):
                y_copy(s).wait()

    @pl.when((it == n_it - 1) & (f == n_f - 1))
    def _():
        @pl.when(live)
        def _():
            gather_wait()

        acc_sc[:MOE_SUB, :] = jnp.zeros((MOE_SUB, acc_sc.shape[1]), F32)
        first = plan_ref[1] // MOE_SUB
        n_blocks = y_hbm.shape[0] // MOE_SUB

        def fill(s):
            dst = y_hbm.at[pl.ds(pl.multiple_of(s * MOE_SUB, MOE_SUB), MOE_SUB), :]
            return pltpu.make_async_copy(acc_sc.at[pl.ds(0, MOE_SUB), :], dst, sem_out)

        lax.fori_loop(first, n_blocks, lambda s, c: (fill(s).start(), c)[1], 0)
        lax.fori_loop(first, n_blocks, lambda s, c: (fill(s).wait(), c)[1], 0)


def _gather_rows_per_item(n_f):
    item_rows = MOE_SUB * MOE_ITEM_SUBS
    chunk = -(-item_rows // n_f)
    chunk += -chunk % 8
    return chunk * n_f


def _expert_ffn(h, g, w_gate, w_up, w_down, plan, item_e, item_row, item_sub, src_rows, max_rows, max_items):
    D = h.shape[1]
    n_f = w_gate.shape[3] // MOE_TF
    item_rows = MOE_SUB * MOE_ITEM_SUBS

    def f_of(it, f, plan_ref):
        return jnp.where(it < plan_ref[0], f, n_f - 1)

    return pl.pallas_call(
        functools.partial(_expert_kernel, n_f),
        out_shape=jax.ShapeDtypeStruct((max_rows, D), F32),
        grid_spec=pltpu.PrefetchScalarGridSpec(
            num_scalar_prefetch=5,
            grid=(max_items, n_f),
            in_specs=[
                pl.BlockSpec(memory_space=pl.ANY),
                pl.BlockSpec((1, D), lambda it, f, n, e, r, s, src: (0, 0)),
                pl.BlockSpec((None, None, D, MOE_TF), lambda it, f, n, e, r, s, src: (0, e[it], 0, f_of(it, f, n))),
                pl.BlockSpec((None, None, D, MOE_TF), lambda it, f, n, e, r, s, src: (0, e[it], 0, f_of(it, f, n))),
                pl.BlockSpec((None, None, MOE_TF, D), lambda it, f, n, e, r, s, src: (0, e[it], f_of(it, f, n), 0)),
            ],
            out_specs=pl.BlockSpec(memory_space=pl.ANY),
            scratch_shapes=[pltpu.VMEM((_gather_rows_per_item(n_f), D), F32),
                            pltpu.VMEM((item_rows, D), BF16),
                            pltpu.VMEM((item_rows, D), F32),
                            pltpu.SemaphoreType.DMA(()), pltpu.SemaphoreType.DMA(())],
        ),
        compiler_params=_cparams(2),
        name="moe_expert_ffn",
    )(plan, item_e, item_row, item_sub, src_rows, h, g.reshape(1, D), w_gate, w_up, w_down)


COMBINE_TM = 256


def _combine_kernel(pos_ref, h_ref, w_ref, g_ref, y_hbm, o_ref, buf, sem):
    base = pl.program_id(0) * COMBINE_TM

    def start(r, c):
        for k in range(TOP_K):
            _row_copy(y_hbm, pos_ref[(base + r) * TOP_K + k], buf.at[k], r, sem).start()
        return c

    lax.fori_loop(0, COMBINE_TM, start, 0, unroll=8)
    for k in range(TOP_K):
        pltpu.make_async_copy(y_hbm.at[pl.ds(0, COMBINE_TM), :], buf.at[k], sem).wait()
    w = w_ref[...]
    out = h_ref[...] + w[:, 0:1] * buf[0] + w[:, 1:2] * buf[1]
    o_ref[...] = _rms(out, g_ref[...])


def _combine_norm(h, ys, pos, top_w, g):
    T, D = h.shape
    return pl.pallas_call(
        _combine_kernel,
        out_shape=jax.ShapeDtypeStruct((T, D), F32),
        grid_spec=pltpu.PrefetchScalarGridSpec(
            num_scalar_prefetch=1,
            grid=(T // COMBINE_TM,),
            in_specs=[pl.BlockSpec((COMBINE_TM, D), lambda i, p: (i, 0)),
                      pl.BlockSpec((COMBINE_TM, TOP_K), lambda i, p: (i, 0)),
                      pl.BlockSpec((1, D), lambda i, p: (0, 0)),
                      pl.BlockSpec(memory_space=pl.ANY)],
            out_specs=pl.BlockSpec((COMBINE_TM, D), lambda i, p: (i, 0)),
            scratch_shapes=[pltpu.VMEM((TOP_K, COMBINE_TM, D), F32), pltpu.SemaphoreType.DMA(())],
        ),
        compiler_params=_cparams(1),
        name="moe_combine_norm",
    )(pos.reshape(-1), h, top_w, g.reshape(1, D), ys)


def _moe_plan(top_idx, gather_rows):
    T = top_idx.shape[0]
    n_assign = T * TOP_K
    item_rows = MOE_SUB * MOE_ITEM_SUBS
    max_rows = n_assign + N_EXPERTS * MOE_SUB
    max_items = n_assign // item_rows + N_EXPERTS
    e_flat = top_idx.reshape(-1)
    onehot = (e_flat[:, None] == jnp.arange(N_EXPERTS)[None, :]).astype(jnp.int32)
    ranks = jnp.cumsum(onehot, axis=0) - onehot
    counts = jnp.sum(onehot, axis=0)
    padded = ((counts + MOE_SUB - 1) // MOE_SUB) * MOE_SUB
    group_start = jnp.cumsum(padded) - padded
    rank = jnp.sum(ranks * onehot, axis=1)
    pos = group_start[e_flat] + rank
    src_rows = jnp.zeros((max_rows + gather_rows,), jnp.int32).at[pos].set(
        jnp.arange(n_assign, dtype=jnp.int32) // TOP_K)
    items_per_e = (padded + item_rows - 1) // item_rows
    item_start = jnp.cumsum(items_per_e) - items_per_e
    n_items = jnp.sum(items_per_e)
    it = jnp.arange(max_items)
    item_e = jnp.minimum(jnp.sum(it[:, None] >= (item_start + items_per_e)[None, :], axis=1), N_EXPERTS - 1)
    local = it - item_start[item_e]
    item_row = group_start[item_e] + local * item_rows
    item_sub = jnp.clip((padded[item_e] - local * item_rows) // MOE_SUB, 0, MOE_ITEM_SUBS)
    live = it < n_items
    last = jnp.maximum(n_items - 1, 0)
    item_e = jnp.where(live, item_e, item_e[last])
    item_row = jnp.where(live, item_row, 0)
    item_sub = jnp.where(live, item_sub, 0)
    i32 = lambda a: a.astype(jnp.int32)
    plan = jnp.stack([n_items, jnp.sum(padded)])
    return (i32(pos.reshape(T, TOP_K)), src_rows, i32(plan), i32(item_e), i32(item_row),
            i32(item_sub), max_rows, max_items)


def _moe_and_final_norm(h, g_ffn, w_router, w_gate, w_up, w_down, g_final):
    top_idx, top_w = _router(h, g_ffn, w_router)
    gather_rows = _gather_rows_per_item(w_gate.shape[3] // MOE_TF)
    pos, src_rows, plan, item_e, item_row, item_sub, max_rows, max_items = _moe_plan(top_idx, gather_rows)
    ys = _expert_ffn(h, g_ffn, w_gate, w_up, w_down, plan, item_e, item_row, item_sub, src_rows, max_rows,
                     max_items)
    return _combine_norm(h, ys, pos, top_w, g_final)


def kernel(x, w_in, w_branch, w_out, norm_mix, norm_ffn, norm_final, sinks, mla_q_norm, mla_kv_norm, mla_w_uq,
           mla_w_ukv, ffn_w_gate, ffn_w_up, ffn_w_down, router_w, moe_w_gate, moe_w_up, moe_w_down):
    nb, seq, d = x.shape
    assert (seq, d) == (SEQ, D_MODEL)
    depth = w_in.shape[0]
    assert depth == 2, "layer 0 uses the dense FFN, layer 1 the MoE followed by the final norm"
    cos64, sin64 = _rope_tables(HEAD_DIM, 0, HEAD_DIM)
    cos32, sin32 = _rope_tables(D_ROPE, D_NOPE, LANES)
    h = x.reshape(nb * seq, d)
    w_in_t = jnp.swapaxes(w_in, 1, 2)
    out = None
    for layer in range(depth):
        n = _rmsnorm(h, norm_mix[layer], BF16)
        proj = _in_proj(n, w_in_t, layer, cos64, sin64)
        o_a = _dilated_attention(proj, nb)
        o_b = _sink_attention(proj, sinks[layer], nb)
        o_c = _stick_attention(proj, nb)
        q_d, k_d, v_d = _mla_prep(proj, mla_q_norm[layer], mla_kv_norm[layer], mla_w_uq[layer], mla_w_ukv[layer],
                                  cos32, sin32)
        o_d = _mla_attention(q_d, k_d, v_d, nb)
        w_gates_t = w_in_t[layer, GATE_COL0:, :].astype(BF16)
        merged = _merge(n, (o_a, o_b, o_c, o_d), w_gates_t, w_branch, layer)
        h = _matmul_res(merged, w_out, layer, h, 1024, 512)
        if layer == 0:
            n2 = _rmsnorm(h, norm_ffn[layer], BF16)
            mid = _swiglu_up(n2, ffn_w_gate, ffn_w_up, 0)
            h = _matmul_res(mid, ffn_w_down, 0, h, 1024, 256)
        else:
            out = _moe_and_final_norm(h, norm_ffn[layer], router_w[0], moe_w_gate, moe_w_up, moe_w_down, norm_final)
    return out.reshape(nb, seq, d)
```

```python
import functools

import numpy as np
import jax
import jax.numpy as jnp
from jax import lax
from jax.experimental import pallas as pl
from jax.experimental.pallas import tpu as pltpu

F32 = jnp.float32
BF16 = jnp.bfloat16

D_MODEL = 2048
SEQ = 2048
HEAD_DIM = 64
BLOCK = 128
LANES = 128
ROPE_THETA = 10000.0
NORM_EPS = 1e-6
DIL_PATTERNS = ((128, 1), (512, 4), (2048, 16))
B_WINDOW = 128
D_NOPE = 64
D_ROPE = 32
Q_LORA = 384
KV_LORA = 256
N_BRANCH = 4
BRANCH_WIDTH = 512
N_EXPERTS = 8
TOP_K = 2

A_Q0, A_K0, A_V0 = 0, 12, 24
B_Q0, B_K0, B_V0 = 36, 40, 41
C_Q0, C_K0, C_V0 = 42, 46, 50
MLA_COL0 = 6912
MLA_BLOCK_W = 768
QKV_WIDTH = 7680
GATE_COL0 = 7584
PROJ_TN = 512

VMEM_LIMIT = 56 * 1024 * 1024
NEG = -1e30
NT_DIMS = (((1,), (1,)), ((), ()))
LOG2_E = 1.4426950408889634

MOE_SUB = 128
MOE_ITEM_SUBS = 8
MOE_TF = 256


def _cparams(n_axes):
    return pltpu.CompilerParams(dimension_semantics=("arbitrary",) * n_axes, vmem_limit_bytes=VMEM_LIMIT)


def _lane_half():
    return lax.broadcasted_iota(jnp.int32, (BLOCK, LANES), 1) < HEAD_DIM


def _rms(x, g):
    return x * lax.rsqrt(jnp.mean(x * x, axis=-1, keepdims=True) + NORM_EPS) * g


def _rmsnorm_kernel(x_ref, g_ref, o_ref):
    o_ref[...] = _rms(x_ref[...], g_ref[...]).astype(o_ref.dtype)


def _rmsnorm(x, g, out_dtype, tm=512):
    T, D = x.shape
    return pl.pallas_call(
        _rmsnorm_kernel,
        out_shape=jax.ShapeDtypeStruct((T, D), out_dtype),
        grid=(T // tm,),
        in_specs=[pl.BlockSpec((tm, D), lambda i: (i, 0)), pl.BlockSpec((1, D), lambda i: (0, 0))],
        out_specs=pl.BlockSpec((tm, D), lambda i: (i, 0)),
        compiler_params=_cparams(1),
        name="rmsnorm",
    )(x, g.reshape(1, D))


def _rope_tables(dh, lane0, period):
    half = dh // 2
    freqs = ROPE_THETA ** (-2.0 * jnp.arange(half, dtype=F32) / dh)
    ang = jnp.arange(SEQ, dtype=F32)[:, None] * freqs[None, :]
    cos, sin = jnp.cos(ang), jnp.sin(ang)
    cos_h = jnp.concatenate([cos, cos], axis=1)
    sin_h = jnp.concatenate([-sin, sin], axis=1)
    ones = jnp.ones((SEQ, period - dh), F32)
    cos_p = jnp.concatenate([cos_h, ones], axis=1)
    sin_p = jnp.concatenate([sin_h, 0.0 * ones], axis=1)
    reps = LANES // period
    cos_t = jnp.roll(jnp.tile(cos_p, (1, reps)), lane0, axis=1)
    sin_t = jnp.roll(jnp.tile(sin_p, (1, reps)), lane0, axis=1)
    return cos_t, sin_t


def _rope_lanes(x, cos, sin, half, first_half_mask):
    fwd = pltpu.roll(x, LANES - half, 1)
    bwd = pltpu.roll(x, half, 1)
    return x * cos + jnp.where(first_half_mask, fwd, bwd) * sin


def _proj_kernel(x_ref, w_ref, cos_ref, sin_ref, o_ref):
    j = pl.program_id(1)
    tiles = PROJ_TN // LANES
    all_rope = (j < (A_V0 // tiles)) | (j == B_Q0 // tiles)
    first_rope = j == B_K0 // tiles
    assert A_V0 % tiles == 0 and B_Q0 % tiles == 0 and B_K0 % tiles == 0 and B_K0 - B_Q0 == tiles

    def project():
        return lax.dot_general(x_ref[...], w_ref[...].astype(BF16), NT_DIMS, preferred_element_type=F32)

    def store(acc, n_rope):
        lane = lax.broadcasted_iota(jnp.int32, (SEQ, LANES), 1)
        first = (lane % HEAD_DIM) < HEAD_DIM // 2
        for c in range(n_rope):
            cols = slice(c * LANES, (c + 1) * LANES)
            o_ref[:, cols] = _rope_lanes(acc[:, cols], cos_ref[...], sin_ref[...], HEAD_DIM // 2, first)
        if n_rope < tiles:
            o_ref[:, n_rope * LANES:] = acc[:, n_rope * LANES:]

    @pl.when(all_rope)
    def _():
        store(project(), tiles)

    @pl.when(first_rope)
    def _():
        store(project(), 1)

    @pl.when(jnp.logical_not(all_rope | first_rope))
    def _():
        o_ref[...] = project()


def _in_proj(n, w_in_t, layer, cos, sin):
    T, D = n.shape
    nb = T // SEQ
    return pl.pallas_call(
        _proj_kernel,
        out_shape=jax.ShapeDtypeStruct((T, QKV_WIDTH), F32),
        grid=(nb, QKV_WIDTH // PROJ_TN),
        in_specs=[
            pl.BlockSpec((SEQ, D), lambda b, j: (b, 0)),
            pl.BlockSpec((None, PROJ_TN, D), lambda b, j: (layer, j, 0)),
            pl.BlockSpec((SEQ, LANES), lambda b, j: (0, 0)),
            pl.BlockSpec((SEQ, LANES), lambda b, j: (0, 0)),
        ],
        out_specs=pl.BlockSpec((SEQ, PROJ_TN), lambda b, j: (b, j)),
        compiler_params=_cparams(2),
        name="in_proj_rope",
    )(n, w_in_t, cos, sin)


BAND_UNROLL = 8


def _band_scores(q, kcat, vcat, head, has_prev, strict_prev):
    half = _lane_half()
    q_scale = HEAD_DIM ** -0.5 * LOG2_E
    qh = jnp.where(half if head == 0 else jnp.logical_not(half), q * q_scale, 0.0).astype(BF16)
    s = lax.dot_general(qh, kcat, NT_DIMS, preferred_element_type=F32)
    row = lax.broadcasted_iota(jnp.int32, (BLOCK, 2 * BLOCK), 0)
    col = lax.broadcasted_iota(jnp.int32, (BLOCK, 2 * BLOCK), 1)
    dist = row + BLOCK - col
    max_dist = BLOCK - 1 if strict_prev else BLOCK
    first_col = jnp.where(has_prev, 0, BLOCK)
    valid = (dist >= 0) & (dist <= max_dist) & (col >= first_col)
    s = jnp.where(valid, s, NEG)
    m = jnp.max(s, axis=1, keepdims=True)
    p = jnp.exp2(s - m)
    l = jnp.sum(p, axis=1, keepdims=True)
    acc = jnp.dot(p.astype(BF16), vcat, preferred_element_type=F32)
    return acc, m, l


def _pair(a0, a1):
    return jnp.where(_lane_half(), a0, a1)


def _dilated_kernel(*refs):
    q_refs, k_refs, v_refs = refs[0:3], refs[3:6], refs[6:9]
    o_ref = refs[9]
    acc_sc, m_sc, l_sc = refs[10:13]
    n_blocks = SEQ // BLOCK

    for g, (win, dil) in enumerate(DIL_PATTERNS):
        assert win // dil == BLOCK and SEQ % (dil * BLOCK) == 0
        q_ref, k_ref, v_ref = q_refs[g], k_refs[g], v_refs[g]

        def rows(start, dil=dil):
            return pl.ds(start, BLOCK) if dil == 1 else pl.ds(start, BLOCK, stride=dil)

        def block(it, g=g, dil=dil, q_ref=q_ref, k_ref=k_ref, v_ref=v_ref, rows=rows):
            r = it % dil
            i = it // dil
            base = i * (BLOCK * dil) + r
            prev = jnp.maximum(base - BLOCK * dil, 0)
            q = q_ref[rows(base), :]
            kcat = jnp.concatenate([k_ref[rows(prev), :], k_ref[rows(base), :]], axis=0).astype(BF16)
            vcat = jnp.concatenate([v_ref[rows(prev), :], v_ref[rows(base), :]], axis=0).astype(BF16)
            a0, m0, l0 = _band_scores(q, kcat, vcat, 0, i > 0, False)
            a1, m1, l1 = _band_scores(q, kcat, vcat, 1, i > 0, False)
            acc_sc[g, rows(base), :] = _pair(a0, a1)
            m_sc[g, rows(base), :] = _pair(m0, m1)
            l_sc[g, rows(base), :] = _pair(l0, l1)

        def body(it, carry, block=block):
            for u in range(BAND_UNROLL):
                block(it + u * (n_blocks // BAND_UNROLL))
            return carry

        lax.fori_loop(0, n_blocks // BAND_UNROLL, body, 0)

    def combine(i, carry):
        rows = pl.ds(pl.multiple_of(i * BLOCK, BLOCK), BLOCK)
        m = [m_sc[g, rows, :] for g in range(len(DIL_PATTERNS))]
        m_all = jnp.maximum(jnp.maximum(m[0], m[1]), m[2])
        num = jnp.zeros((BLOCK, LANES), F32)
        den = jnp.zeros((BLOCK, LANES), F32)
        for g in range(len(DIL_PATTERNS)):
            a = jnp.exp2(m[g] - m_all)
            num = num + a * acc_sc[g, rows, :]
            den = den + a * l_sc[g, rows, :]
        o_ref[rows, :] = (num / den).astype(o_ref.dtype)
        return carry

    lax.fori_loop(0, n_blocks, combine, 0)


def _dilated_attention(proj, nb):
    T = proj.shape[0]
    specs = []
    for col0 in (A_Q0, A_K0, A_V0):
        for g in range(len(DIL_PATTERNS)):
            specs.append(pl.BlockSpec((SEQ, LANES), lambda b, hp, c=col0 + 4 * g: (b, c + hp)))
    return pl.pallas_call(
        _dilated_kernel,
        out_shape=jax.ShapeDtypeStruct((T, BRANCH_WIDTH), BF16),
        grid=(nb, BRANCH_WIDTH // LANES),
        in_specs=specs,
        out_specs=pl.BlockSpec((SEQ, LANES), lambda b, hp: (b, hp)),
        scratch_shapes=[pltpu.VMEM((len(DIL_PATTERNS), SEQ, LANES), F32)] * 3,
        compiler_params=_cparams(2),
        name="dilated_attention",
    )(*([proj] * 9))


def _sink_kernel(q_ref, k_ref, v_ref, sink_ref, o_ref, k_sc, v_sc):
    hp = pl.program_id(1)
    kv_head = hp // 2
    k = k_ref[...]
    v = v_ref[...]
    k_sw = pltpu.roll(k, HEAD_DIM, 1)
    v_sw = pltpu.roll(v, HEAD_DIM, 1)
    first = kv_head == 0
    k_sc[0] = jnp.where(first, k, k_sw).astype(BF16)
    k_sc[1] = jnp.where(first, k_sw, k).astype(BF16)
    v_sc[0] = jnp.where(first, v, v_sw).astype(BF16)
    v_sc[1] = jnp.where(first, v_sw, v).astype(BF16)
    sink = _pair(sink_ref[pl.ds(2 * hp, 1), :], sink_ref[pl.ds(2 * hp + 1, 1), :])

    def block(i):
        base = pl.multiple_of(i * BLOCK, BLOCK)
        prev = pl.multiple_of(jnp.maximum(base - BLOCK, 0), BLOCK)
        q = q_ref[pl.ds(base, BLOCK), :]
        out = []
        for a in range(2):
            kcat = jnp.concatenate([k_sc[a, pl.ds(prev, BLOCK), :], k_sc[a, pl.ds(base, BLOCK), :]], axis=0)
            vcat = jnp.concatenate([v_sc[a, pl.ds(prev, BLOCK), :], v_sc[a, pl.ds(base, BLOCK), :]], axis=0)
            out.append(_band_scores(q, kcat, vcat, a, i > 0, True))
        acc = _pair(out[0][0], out[1][0])
        m = _pair(out[0][1], out[1][1])
        l = _pair(out[0][2], out[1][2])
        m_new = jnp.maximum(m, sink)
        scale = jnp.exp2(m - m_new)
        den = l * scale + jnp.exp2(sink - m_new)
        o_ref[pl.ds(base, BLOCK), :] = (acc * scale / den).astype(o_ref.dtype)

    n_blocks = SEQ // BLOCK

    def body(it, carry):
        for u in range(BAND_UNROLL):
            block(it + u * (n_blocks // BAND_UNROLL))
        return carry

    lax.fori_loop(0, n_blocks // BAND_UNROLL, body, 0)


def _sink_attention(proj, sinks_l, nb):
    T = proj.shape[0]
    sink_tab = jnp.broadcast_to(sinks_l.astype(F32)[:, None] * LOG2_E, (8, LANES))
    return pl.pallas_call(
        _sink_kernel,
        out_shape=jax.ShapeDtypeStruct((T, BRANCH_WIDTH), BF16),
        grid=(nb, BRANCH_WIDTH // LANES),
        in_specs=[
            pl.BlockSpec((SEQ, LANES), lambda b, hp: (b, B_Q0 + hp)),
            pl.BlockSpec((SEQ, LANES), lambda b, hp: (b, B_K0)),
            pl.BlockSpec((SEQ, LANES), lambda b, hp: (b, B_V0)),
            pl.BlockSpec((8, LANES), lambda b, hp: (0, 0)),
        ],
        out_specs=pl.BlockSpec((SEQ, LANES), lambda b, hp: (b, hp)),
        scratch_shapes=[pltpu.VMEM((2, SEQ, LANES), BF16)] * 2,
        compiler_params=_cparams(2),
        name="sink_attention",
    )(proj, proj, proj, sink_tab)


STICK_TQ = 256
STICK_SUBS = STICK_TQ // BLOCK
STICK_STAGES = 5


def _stick_tiles():
    n_key_blocks, n_q_tiles = SEQ // BLOCK, SEQ // STICK_TQ
    tiles = []
    for j in reversed(range(n_key_blocks)):
        first = j // STICK_SUBS
        tiles.append((first, j, 1 + j % STICK_SUBS))
        tiles += [(qi, j, 0) for qi in range(first + 1, n_q_tiles)]
    idle = (0, n_key_blocks - 1, STICK_SUBS + 1)
    lag = STICK_STAGES - 1
    stages = [[idle] * s + tiles + [idle] * (lag - s) for s in range(STICK_STAGES)]
    table = np.asarray(stages, np.int32)
    return table.shape[1], [np.ascontiguousarray(table[:, :, c]).reshape(-1) for c in range(3)]


def _stick_masks():
    row = np.arange(STICK_TQ)[:, None]
    col = np.arange(BLOCK)[None, :]
    valid = [np.ones((STICK_TQ, BLOCK), bool)]
    valid += [col + r * BLOCK < row for r in range(STICK_SUBS)]
    valid += [np.zeros((STICK_TQ, BLOCK), bool)]
    valid = np.stack(valid)
    masks = np.stack([valid.astype(np.float32), np.where(valid, 0.0, NEG).astype(np.float32)], axis=1)
    return masks.reshape(-1, STICK_TQ, BLOCK)


def _stick_kernel(n_slots, tq_ref, tk_ref, tm_ref, q_ref, k_ref, v_ref, tri_ref, mask_ref, o_ref,
                  q_sc, kd_sc, vd_sc, carry_sc, acc_sc, z_buf, hl_buf, sums_buf, w_buf):
    half = _lane_half()
    q_sc[...] = (q_ref[...] * (HEAD_DIM ** -0.5 * LOG2_E)).astype(BF16)
    n_key_blocks = SEQ // BLOCK
    k = k_ref[...]
    v = v_ref[...]
    for j in range(n_key_blocks):
        rows = slice(j * BLOCK, (j + 1) * BLOCK)
        kd_sc[j, :BLOCK] = jnp.where(half, k[rows], 0.0).astype(BF16)
        kd_sc[j, BLOCK:] = jnp.where(half, 0.0, k[rows]).astype(BF16)
        vd_sc[j, :BLOCK] = jnp.where(half, v[rows], 0.0).astype(BF16)
        vd_sc[j, BLOCK:] = jnp.where(half, 0.0, v[rows]).astype(BF16)
    carry_sc[...] = jnp.zeros_like(carry_sc)
    acc_sc[...] = jnp.zeros_like(acc_sc)
    z_buf[...] = jnp.zeros_like(z_buf)
    hl_buf[...] = jnp.zeros_like(hl_buf)
    sums_buf[...] = jnp.zeros_like(sums_buf)
    w_buf[...] = jnp.zeros_like(w_buf)
    tri = tri_ref[...]

    def q_rows(qi):
        return pl.ds(pl.multiple_of(qi * STICK_TQ, STICK_TQ), STICK_TQ)

    def trip(t, phase):
        cur, other = phase % 2, (phase + 1) % 2
        i = 4 * n_slots + t
        acc_sc[q_rows(tq_ref[i]), :] += jnp.dot(w_buf[cur], vd_sc[tk_ref[i]], preferred_element_type=F32)
        for a in range(2):
            sums_buf[2 * cur + a] = jnp.dot(hl_buf[2 * cur + a], tri, preferred_element_type=F32)
        z_buf[phase] = lax.dot_general(q_sc[q_rows(tq_ref[t]), :], kd_sc[tk_ref[t]], NT_DIMS,
                                       preferred_element_type=F32)
        i = 3 * n_slots + t
        rows = q_rows(tq_ref[i])
        bias = mask_ref[2 * tm_ref[i] + 1]
        z = z_buf[(phase + 1) % 4]
        w = []
        for a in range(2):
            sums = sums_buf[2 * other + a]
            carry = carry_sc[a, rows, :]
            w.append(jnp.exp2(z[:, a * BLOCK:(a + 1) * BLOCK] + carry + sums[:, :BLOCK] + bias))
            carry_sc[a, rows, :] = carry + sums[:, BLOCK:]
        w_buf[other] = jnp.concatenate(w, axis=1).astype(BF16)
        keep = mask_ref[2 * tm_ref[n_slots + t]]
        z = z_buf[(phase + 3) % 4]
        for a in range(2):
            za = z[:, a * BLOCK:(a + 1) * BLOCK]
            sp = (jnp.maximum(za, 0.0) + jnp.log2(1.0 + jnp.exp2(-jnp.abs(za)))) * keep
            hi = sp.astype(BF16)
            lo = (sp - hi.astype(F32)).astype(BF16)
            hl_buf[2 * other + a] = jnp.concatenate([hi, lo], axis=1)

    def body(i, c):
        for phase in range(4):
            trip(4 * i + phase, phase)
        return c

    assert n_slots % 4 == 0
    lax.fori_loop(0, n_slots // 4, body, 0)
    o_ref[...] = acc_sc[...].astype(o_ref.dtype)


def _stick_attention(proj, nb):
    T = proj.shape[0]
    jp = np.arange(BLOCK)[:, None]
    s = np.arange(BLOCK)[None, :]
    tri = -np.concatenate([(jp >= s), np.ones((BLOCK, BLOCK), bool)], axis=1).astype(np.float32)
    tri = jnp.asarray(np.concatenate([tri, tri], axis=0), BF16)
    n_slots, tables = _stick_tiles()
    masks = _stick_masks()
    n_key_blocks = SEQ // BLOCK
    return pl.pallas_call(
        functools.partial(_stick_kernel, n_slots),
        out_shape=jax.ShapeDtypeStruct((T, BRANCH_WIDTH), BF16),
        grid_spec=pltpu.PrefetchScalarGridSpec(
            num_scalar_prefetch=3,
            grid=(nb, BRANCH_WIDTH // LANES),
            in_specs=[
                pl.BlockSpec((SEQ, LANES), lambda b, hp, *_: (b, C_Q0 + hp)),
                pl.BlockSpec((SEQ, LANES), lambda b, hp, *_: (b, C_K0 + hp)),
                pl.BlockSpec((SEQ, LANES), lambda b, hp, *_: (b, C_V0 + hp)),
                pl.BlockSpec((2 * BLOCK, 2 * BLOCK), lambda b, hp, *_: (0, 0)),
                pl.BlockSpec(masks.shape, lambda b, hp, *_: (0, 0, 0)),
            ],
            out_specs=pl.BlockSpec((SEQ, LANES), lambda b, hp, *_: (b, hp)),
            scratch_shapes=[
                pltpu.VMEM((SEQ, LANES), BF16),
                pltpu.VMEM((n_key_blocks, 2 * BLOCK, LANES), BF16),
                pltpu.VMEM((n_key_blocks, 2 * BLOCK, LANES), BF16),
                pltpu.VMEM((2, SEQ, LANES), F32),
                pltpu.VMEM((SEQ, LANES), F32),
                pltpu.VMEM((4, STICK_TQ, 2 * BLOCK), F32),
                pltpu.VMEM((4, STICK_TQ, 2 * BLOCK), BF16),
                pltpu.VMEM((4, STICK_TQ, 2 * BLOCK), F32),
                pltpu.VMEM((2, STICK_TQ, 2 * BLOCK), BF16),
            ],
        ),
        compiler_params=_cparams(2),
        name="stick_breaking_attention",
    )(*[jnp.asarray(t) for t in tables], proj, proj, proj, tri, jnp.asarray(masks))


MLA_HEADS = 8
MLA_TM = 512


def _mla_prep_kernel(c_ref, gq_ref, gkv_ref, wq_ref, wk_ref, wv_ref, cos_ref, sin_ref, q_ref, k_ref, v_ref):
    c = c_ref[...]
    cq = _rms(c[:, :Q_LORA], gq_ref[...]).astype(BF16)
    ckv = _rms(c[:, Q_LORA:Q_LORA + KV_LORA], gkv_ref[...]).astype(BF16)
    q = jnp.dot(cq, wq_ref[...].astype(BF16), preferred_element_type=F32)
    k = jnp.dot(ckv, wk_ref[...].astype(BF16), preferred_element_type=F32)
    v_ref[...] = jnp.dot(ckv, wv_ref[...].astype(BF16), preferred_element_type=F32).astype(BF16)
    cos, sin = cos_ref[...], sin_ref[...]
    lane = lax.broadcasted_iota(jnp.int32, (MLA_TM, LANES), 1)
    rope_lane = (lane >= D_NOPE) & (lane < D_NOPE + D_ROPE)
    first = lane < D_NOPE + D_ROPE // 2
    kr = pltpu.roll(c[:, Q_LORA + KV_LORA:], D_NOPE, 1)
    kr = jnp.where(rope_lane, _rope_lanes(kr, cos, sin, D_ROPE // 2, first), 0.0)
    scale = (D_NOPE + D_ROPE) ** -0.5 * LOG2_E
    for h in range(MLA_HEADS):
        sl = slice(h * LANES, (h + 1) * LANES)
        q_ref[:, sl] = (_rope_lanes(q[:, sl], cos, sin, D_ROPE // 2, first) * scale).astype(BF16)
        k_ref[:, sl] = (k[:, sl] + kr).astype(BF16)


def _mla_prep(proj, g_qa, g_kva, w_uq, w_ukv, cos, sin):
    T = proj.shape[0]
    wq = jnp.pad(w_uq.reshape(Q_LORA, MLA_HEADS, D_NOPE + D_ROPE), ((0, 0), (0, 0), (0, LANES - D_NOPE - D_ROPE)))
    wq = wq.reshape(Q_LORA, MLA_HEADS * LANES)
    wkv = w_ukv.reshape(KV_LORA, MLA_HEADS, 2 * HEAD_DIM)
    wk = jnp.pad(wkv[:, :, :D_NOPE], ((0, 0), (0, 0), (0, LANES - D_NOPE))).reshape(KV_LORA, MLA_HEADS * LANES)
    wv = wkv[:, :, D_NOPE:].reshape(KV_LORA, MLA_HEADS * HEAD_DIM)
    full = lambda shape: pl.BlockSpec(shape, lambda i: (0, 0))
    per_seq = SEQ // MLA_TM
    return pl.pallas_call(
        _mla_prep_kernel,
        out_shape=[jax.ShapeDtypeStruct((T, MLA_HEADS * LANES), BF16),
                   jax.ShapeDtypeStruct((T, MLA_HEADS * LANES), BF16),
                   jax.ShapeDtypeStruct((T, MLA_HEADS * HEAD_DIM), BF16)],
        grid=(T // MLA_TM,),
        in_specs=[
            pl.BlockSpec((MLA_TM, MLA_BLOCK_W), lambda i: (i, MLA_COL0 // MLA_BLOCK_W)),
            full((1, Q_LORA)), full((1, KV_LORA)),
            full(wq.shape), full(wk.shape), full(wv.shape),
            pl.BlockSpec((MLA_TM, LANES), lambda i: (i % per_seq, 0)),
            pl.BlockSpec((MLA_TM, LANES), lambda i: (i % per_seq, 0)),
        ],
        out_specs=[pl.BlockSpec((MLA_TM, MLA_HEADS * LANES), lambda i: (i, 0)),
                   pl.BlockSpec((MLA_TM, MLA_HEADS * LANES), lambda i: (i, 0)),
                   pl.BlockSpec((MLA_TM, MLA_HEADS * HEAD_DIM), lambda i: (i, 0))],
        compiler_params=_cparams(1),
        name="mla_prep",
    )(proj, g_qa.reshape(1, Q_LORA), g_kva.reshape(1, KV_LORA), wq, wk, wv, cos, sin)


MLA_TQ = 1024
MLA_TK = 256


def _mla_attn_kernel(q_ref, k_ref, v_ref, o_ref, v_sc, m_sc, acc_sc):
    half = lax.broadcasted_iota(jnp.int32, (SEQ, LANES), 1) < HEAD_DIM
    v = v_ref[...]
    one = jnp.ones_like(v)
    v_sc[0] = jnp.where(half, v, one)
    v_sc[1] = jnp.where(half, one, v)
    lane_half = lax.broadcasted_iota(jnp.int32, (MLA_TQ, LANES), 1) < HEAD_DIM

    def step(qb, kb, diagonal):
        for a in range(2):
            q = q_ref[pl.ds(qb, MLA_TQ), a * LANES:(a + 1) * LANES]
            k = k_ref[pl.ds(kb, MLA_TK), a * LANES:(a + 1) * LANES]
            s = lax.dot_general(q, k, (((1,), (1,)), ((), ())), preferred_element_type=F32)
            if diagonal:
                row = lax.broadcasted_iota(jnp.int32, (MLA_TQ, MLA_TK), 0)
                col = lax.broadcasted_iota(jnp.int32, (MLA_TQ, MLA_TK), 1)
                s = jnp.where(col + (kb - qb) <= row, s, NEG)
            m = m_sc[a]
            m_new = jnp.maximum(m, jnp.max(s, axis=1, keepdims=True))
            alpha = jnp.exp2(m - m_new)
            p = jnp.exp2(s - jnp.concatenate([m_new] * (MLA_TK // LANES), axis=1))
            pv = jnp.dot(p.astype(BF16), v_sc[a, pl.ds(kb, MLA_TK), :], preferred_element_type=F32)
            acc_sc[a] = alpha * acc_sc[a] + pv
            m_sc[a] = m_new

    def q_block(qi, c):
        qb = pl.multiple_of(qi * MLA_TQ, MLA_TQ)
        m_sc[...] = jnp.full(m_sc.shape, NEG, F32)
        acc_sc[...] = jnp.zeros(acc_sc.shape, F32)

        def k_block(kj, c2):
            step(qb, pl.multiple_of(kj * MLA_TK, MLA_TK), False)
            return c2

        lax.fori_loop(0, qi * (MLA_TQ // MLA_TK), k_block, 0)
        for d in range(MLA_TQ // MLA_TK):
            step(qb, qb + d * MLA_TK, True)
        outs = [acc_sc[a] / pltpu.roll(acc_sc[a], HEAD_DIM, 1) for a in range(2)]
        o_ref[pl.ds(qb, MLA_TQ), :] = jnp.where(lane_half, outs[0], outs[1]).astype(o_ref.dtype)
        return c

    lax.fori_loop(0, SEQ // MLA_TQ, q_block, 0)


def _mla_attention(q, k, v, nb):
    T = q.shape[0]
    return pl.pallas_call(
        _mla_attn_kernel,
        out_shape=jax.ShapeDtypeStruct((T, BRANCH_WIDTH), BF16),
        grid=(nb, BRANCH_WIDTH // LANES),
        in_specs=[
            pl.BlockSpec((SEQ, 2 * LANES), lambda b, hp: (b, hp)),
            pl.BlockSpec((SEQ, 2 * LANES), lambda b, hp: (b, hp)),
            pl.BlockSpec((SEQ, LANES), lambda b, hp: (b, hp)),
        ],
        out_specs=pl.BlockSpec((SEQ, LANES), lambda b, hp: (b, hp)),
        scratch_shapes=[pltpu.VMEM((2, SEQ, LANES), BF16),
                        pltpu.VMEM((2, MLA_TQ, LANES), F32),
                        pltpu.VMEM((2, MLA_TQ, LANES), F32)],
        compiler_params=_cparams(2),
        name="mla_attention",
    )(q, k, v)


MERGE_TM = 1024
MERGE_TN = 256


def _merge_kernel(*refs):
    n_ref = refs[0]
    o_refs = refs[1:5]
    wg_refs = refs[5:9]
    wb_refs = refs[9:13]
    out_ref = refs[13]
    n = n_ref[...]
    acc = jnp.zeros((MERGE_TM, MERGE_TN), F32)
    for i in range(N_BRANCH):
        gate = lax.dot_general(n, wg_refs[i][...], NT_DIMS, preferred_element_type=F32)
        br = jnp.dot(o_refs[i][...], wb_refs[i][...].astype(BF16), preferred_element_type=F32)
        acc = acc + br * jax.nn.sigmoid(gate)
    out_ref[...] = acc.astype(out_ref.dtype)


GATE_CAST_ROWS = 256


def _gate_cast_kernel(layer, w_hbm, o_ref, buf, sem):
    j = pl.program_id(0)

    def copy(step, slot):
        rows = pl.ds(pl.multiple_of(GATE_COL0 + step * GATE_CAST_ROWS, 8), GATE_CAST_ROWS)
        return pltpu.make_async_copy(w_hbm.at[layer, rows, :], buf.at[slot], sem.at[slot])

    @pl.when(j == 0)
    def _():
        copy(0, 0).start()

    @pl.when(j + 1 < pl.num_programs(0))
    def _():
        copy(j + 1, (j + 1) % 2).start()

    copy(j, j % 2).wait()
    o_ref[...] = buf[j % 2].astype(o_ref.dtype)


def _gate_weights_t(w_in_t, layer):
    n_gate = N_BRANCH * D_MODEL
    D = w_in_t.shape[2]
    assert GATE_COL0 + n_gate == w_in_t.shape[1] and GATE_COL0 % 8 == 0
    return pl.pallas_call(
        functools.partial(_gate_cast_kernel, layer),
        out_shape=jax.ShapeDtypeStruct((n_gate, D), BF16),
        grid=(n_gate // GATE_CAST_ROWS,),
        in_specs=[pl.BlockSpec(memory_space=pl.ANY)],
        out_specs=pl.BlockSpec((GATE_CAST_ROWS, D), lambda j: (j, 0)),
        scratch_shapes=[pltpu.VMEM((2, GATE_CAST_ROWS, D), F32), pltpu.SemaphoreType.DMA((2,))],
        compiler_params=_cparams(1),
        name="gate_weight_cast",
    )(w_in_t)


def _merge(n, branches, w_gates_t, w_branch, layer):
    T, D = n.shape
    nblk = D // MERGE_TN
    in_specs = [pl.BlockSpec((MERGE_TM, D), lambda m, j: (m, 0))]
    in_specs += [pl.BlockSpec((MERGE_TM, BRANCH_WIDTH), lambda m, j: (m, 0))] * N_BRANCH
    in_specs += [pl.BlockSpec((MERGE_TN, D), lambda m, j, i=i: (i * nblk + j, 0)) for i in range(N_BRANCH)]
    in_specs += [pl.BlockSpec((None, None, BRANCH_WIDTH, MERGE_TN), lambda m, j, i=i: (layer, i, 0, j))
                 for i in range(N_BRANCH)]
    return pl.pallas_call(
        _merge_kernel,
        out_shape=jax.ShapeDtypeStruct((T, D), BF16),
        grid=(T // MERGE_TM, nblk),
        in_specs=in_specs,
        out_specs=pl.BlockSpec((MERGE_TM, MERGE_TN), lambda m, j: (m, j)),
        compiler_params=_cparams(2),
        name="gated_merge",
    )(n, *branches, *([w_gates_t] * N_BRANCH), *([w_branch] * N_BRANCH))


def _matmul_res_kernel(a_ref, w_ref, r_ref, o_ref):
    o_ref[...] = r_ref[...] + jnp.dot(a_ref[...], w_ref[...].astype(BF16), preferred_element_type=F32)


def _matmul_res(a, w, layer, res, tm, tn):
    M, K = a.shape
    N = w.shape[2]
    return pl.pallas_call(
        _matmul_res_kernel,
        out_shape=jax.ShapeDtypeStruct((M, N), F32),
        grid=(M // tm, N // tn),
        in_specs=[pl.BlockSpec((tm, K), lambda m, j: (m, 0)),
                  pl.BlockSpec((None, K, tn), lambda m, j: (layer, 0, j)),
                  pl.BlockSpec((tm, tn), lambda m, j: (m, j))],
        out_specs=pl.BlockSpec((tm, tn), lambda m, j: (m, j)),
        compiler_params=_cparams(2),
        name="matmul_residual",
    )(a, w, res)


def _swiglu_up_kernel(x_ref, wg_ref, wu_ref, o_ref):
    x = x_ref[...]
    g = jnp.dot(x, wg_ref[...].astype(BF16), preferred_element_type=F32)
    u = jnp.dot(x, wu_ref[...].astype(BF16), preferred_element_type=F32)
    o_ref[...] = (jax.nn.silu(g) * u).astype(o_ref.dtype)


def _swiglu_up(x, wg, wu, layer, tm=1024, tn=512):
    M, K = x.shape
    N = wg.shape[2]
    return pl.pallas_call(
        _swiglu_up_kernel,
        out_shape=jax.ShapeDtypeStruct((M, N), BF16),
        grid=(M // tm, N // tn),
        in_specs=[pl.BlockSpec((tm, K), lambda m, j: (m, 0)),
                  pl.BlockSpec((None, K, tn), lambda m, j: (layer, 0, j)),
                  pl.BlockSpec((None, K, tn), lambda m, j: (layer, 0, j))],
        out_specs=pl.BlockSpec((tm, tn), lambda m, j: (m, j)),
        compiler_params=_cparams(2),
        name="swiglu_up",
    )(x, wg, wu)


ROUTER_TM = 512


def _router_kernel(h_ref, g_ref, wr_ref, idx_ref, w_ref):
    n = _rms(h_ref[...], g_ref[...])
    logits = jnp.dot(n, wr_ref[...], preferred_element_type=F32, precision=lax.Precision.HIGHEST)
    lane = lax.broadcasted_iota(jnp.int32, logits.shape, 1)
    v1 = jnp.max(logits, axis=1, keepdims=True)
    i1 = jnp.min(jnp.where(logits == v1, lane, N_EXPERTS), axis=1, keepdims=True)
    rest = jnp.where(lane == i1, -jnp.inf, logits)
    v2 = jnp.max(rest, axis=1, keepdims=True)
    i2 = jnp.min(jnp.where(rest == v2, lane, N_EXPERTS), axis=1, keepdims=True)
    e2 = jnp.exp(v2 - v1)
    den = 1.0 + e2
    two = lax.broadcasted_iota(jnp.int32, (ROUTER_TM, TOP_K), 1)
    idx_ref[...] = jnp.where(two == 0, i1, i2)
    w_ref[...] = jnp.where(two == 0, 1.0 / den, e2 / den)


def _router(h, g, w_router):
    T, D = h.shape
    return pl.pallas_call(
        _router_kernel,
        out_shape=[jax.ShapeDtypeStruct((T, TOP_K), jnp.int32), jax.ShapeDtypeStruct((T, TOP_K), F32)],
        grid=(T // ROUTER_TM,),
        in_specs=[pl.BlockSpec((ROUTER_TM, D), lambda i: (i, 0)),
                  pl.BlockSpec((1, D), lambda i: (0, 0)),
                  pl.BlockSpec((D, N_EXPERTS), lambda i: (0, 0))],
        out_specs=[pl.BlockSpec((ROUTER_TM, TOP_K), lambda i: (i, 0)),
                   pl.BlockSpec((ROUTER_TM, TOP_K), lambda i: (i, 0))],
        compiler_params=_cparams(1),
        name="moe_router",
    )(h, g.reshape(1, D), w_router)


def _row_copy(src_hbm, row, dst_vmem, slot, sem):
    return pltpu.make_async_copy(src_hbm.at[pl.ds(row, 1), :], dst_vmem.at[pl.ds(slot, 1), :], sem)


def _expert_kernel(n_f, plan_ref, item_e_ref, item_row_ref, item_sub_ref, src_ref,
                   h_hbm, g_ref, wg_ref, wu_ref, wd_ref, y_hbm, gbuf, x_sc, acc_sc, sem_in, sem_out):
    it = pl.program_id(0)
    f = pl.program_id(1)
    n_it = pl.num_programs(0)
    n_items = plan_ref[0]
    live = it < n_items
    row0 = pl.multiple_of(item_row_ref[it], MOE_SUB)
    n_sub = item_sub_ref[it]
    item_rows = x_sc.shape[0]
    g_rows = gbuf.shape[0]
    chunk = g_rows // n_f

    def gather_wait():
        pltpu.make_async_copy(h_hbm.at[pl.ds(0, g_rows), :], gbuf, sem_in).wait()

    def y_copy(s):
        rows = pl.ds(s * MOE_SUB, MOE_SUB)
        return pltpu.make_async_copy(acc_sc.at[rows, :], y_hbm.at[pl.ds(row0 + s * MOE_SUB, MOE_SUB), :], sem_out)

    @pl.when(live & (it == 0) & (f == 0))
    def _():
        def start(r, c):
            _row_copy(h_hbm, src_ref[row0 + r], gbuf, r, sem_in).start()
            return c
        lax.fori_loop(0, g_rows, start, 0)

    @pl.when(live & (f == 0))
    def _():
        gather_wait()

        def norm(i, c):
            rows = pl.ds(pl.multiple_of(i * MOE_SUB, MOE_SUB), MOE_SUB)
            x_sc[rows, :] = _rms(gbuf[rows, :], g_ref[...]).astype(BF16)
            return c
        lax.fori_loop(0, item_rows // MOE_SUB, norm, 0)
        acc_sc[...] = jnp.zeros(acc_sc.shape, F32)

    @pl.when((it == n_items) & (it > 0) & (f == 0))
    def _():
        gather_wait()

    next_row0 = item_row_ref[jnp.minimum(it + 1, n_it - 1)]

    for n in range(1, MOE_ITEM_SUBS + 1):
        @pl.when(live & (n_sub == n))
        def _(n=n):
            lo = f * chunk
            for r in range(chunk):
                _row_copy(h_hbm, src_ref[next_row0 + lo + r], gbuf, lo + r, sem_in).start()
            rows = n * MOE_SUB
            x = x_sc[:rows, :]
            g = jnp.dot(x, wg_ref[...].astype(BF16), preferred_element_type=F32)
            u = jnp.dot(x, wu_ref[...].astype(BF16), preferred_element_type=F32)
            mid = (jax.nn.silu(g) * u).astype(BF16)
            acc_sc[:rows, :] += jnp.dot(mid, wd_ref[...].astype(BF16), preferred_element_type=F32)

    @pl.when(live & (f == n_f - 1))
    def _():
        for s in range(MOE_ITEM_SUBS):
            @pl.when(s < n_sub)
            def _(s=s):
                y_copy(s).start()
        for s in range(MOE_ITEM_SUBS):
            @pl.when(s < n_sub)
            def _(s=s):
                y_copy(s).wait()

    @pl.when((it == n_it - 1) & (f == n_f - 1))
    def _():
        @pl.when(live)
        def _():
            gather_wait()

        acc_sc[:MOE_SUB, :] = jnp.zeros((MOE_SUB, acc_sc.shape[1]), F32)
        first = plan_ref[1] // MOE_SUB
        n_blocks = y_hbm.shape[0] // MOE_SUB

        def fill(s):
            dst = y_hbm.at[pl.ds(pl.multiple_of(s * MOE_SUB, MOE_SUB), MOE_SUB), :]
            return pltpu.make_async_copy(acc_sc.at[pl.ds(0, MOE_SUB), :], dst, sem_out)

        lax.fori_loop(first, n_blocks, lambda s, c: (fill(s).start(), c)[1], 0)
        lax.fori_loop(first, n_blocks, lambda s, c: (fill(s).wait(), c)[1], 0)


def _gather_rows_per_item(n_f):
    item_rows = MOE_SUB * MOE_ITEM_SUBS
    chunk = -(-item_rows // n_f)
    chunk += -chunk % 8
    return chunk * n_f


def _expert_ffn(h, g, w_gate, w_up, w_down, plan, item_e, item_row, item_sub, src_rows, max_rows, max_items):
    D = h.shape[1]
    n_f = w_gate.shape[3] // MOE_TF
    item_rows = MOE_SUB * MOE_ITEM_SUBS

    def f_of(it, f, plan_ref):
        return jnp.where(it < plan_ref[0], f, n_f - 1)

    return pl.pallas_call(
        functools.partial(_expert_kernel, n_f),
        out_shape=jax.ShapeDtypeStruct((max_rows, D), F32),
        grid_spec=pltpu.PrefetchScalarGridSpec(
            num_scalar_prefetch=5,
            grid=(max_items, n_f),
            in_specs=[
                pl.BlockSpec(memory_space=pl.ANY),
                pl.BlockSpec((1, D), lambda it, f, n, e, r, s, src: (0, 0)),
                pl.BlockSpec((None, None, D, MOE_TF), lambda it, f, n, e, r, s, src: (0, e[it], 0, f_of(it, f, n))),
                pl.BlockSpec((None, None, D, MOE_TF), lambda it, f, n, e, r, s, src: (0, e[it], 0, f_of(it, f, n))),
                pl.BlockSpec((None, None, MOE_TF, D), lambda it, f, n, e, r, s, src: (0, e[it], f_of(it, f, n), 0)),
            ],
            out_specs=pl.BlockSpec(memory_space=pl.ANY),
            scratch_shapes=[pltpu.VMEM((_gather_rows_per_item(n_f), D), F32),
                            pltpu.VMEM((item_rows, D), BF16),
                            pltpu.VMEM((item_rows, D), F32),
                            pltpu.SemaphoreType.DMA(()), pltpu.SemaphoreType.DMA(())],
        ),
        compiler_params=_cparams(2),
        name="moe_expert_ffn",
    )(plan, item_e, item_row, item_sub, src_rows, h, g.reshape(1, D), w_gate, w_up, w_down)


COMBINE_TM = 256


def _combine_kernel(pos_ref, h_ref, w_ref, g_ref, y_hbm, o_ref, buf, sem):
    base = pl.program_id(0) * COMBINE_TM

    def start(r, c):
        for k in range(TOP_K):
            _row_copy(y_hbm, pos_ref[(base + r) * TOP_K + k], buf.at[k], r, sem).start()
        return c

    lax.fori_loop(0, COMBINE_TM, start, 0, unroll=8)
    for k in range(TOP_K):
        pltpu.make_async_copy(y_hbm.at[pl.ds(0, COMBINE_TM), :], buf.at[k], sem).wait()
    w = w_ref[...]
    out = h_ref[...] + w[:, 0:1] * buf[0] + w[:, 1:2] * buf[1]
    o_ref[...] = _rms(out, g_ref[...])


def _combine_norm(h, ys, pos, top_w, g):
    T, D = h.shape
    return pl.pallas_call(
        _combine_kernel,
        out_shape=jax.ShapeDtypeStruct((T, D), F32),
        grid_spec=pltpu.PrefetchScalarGridSpec(
            num_scalar_prefetch=1,
            grid=(T // COMBINE_TM,),
            in_specs=[pl.BlockSpec((COMBINE_TM, D), lambda i, p: (i, 0)),
                      pl.BlockSpec((COMBINE_TM, TOP_K), lambda i, p: (i, 0)),
                      pl.BlockSpec((1, D), lambda i, p: (0, 0)),
                      pl.BlockSpec(memory_space=pl.ANY)],
            out_specs=pl.BlockSpec((COMBINE_TM, D), lambda i, p: (i, 0)),
            scratch_shapes=[pltpu.VMEM((TOP_K, COMBINE_TM, D), F32), pltpu.SemaphoreType.DMA(())],
        ),
        compiler_params=_cparams(1),
        name="moe_combine_norm",
    )(pos.reshape(-1), h, top_w, g.reshape(1, D), ys)


def _moe_plan(top_idx, gather_rows):
    T = top_idx.shape[0]
    n_assign = T * TOP_K
    item_rows = MOE_SUB * MOE_ITEM_SUBS
    max_rows = n_assign + N_EXPERTS * MOE_SUB
    max_items = n_assign // item_rows + N_EXPERTS
    e_flat = top_idx.reshape(-1)
    onehot = (e_flat[:, None] == jnp.arange(N_EXPERTS)[None, :]).astype(jnp.int32)
    ranks = jnp.cumsum(onehot, axis=0) - onehot
    counts = jnp.sum(onehot, axis=0)
    padded = ((counts + MOE_SUB - 1) // MOE_SUB) * MOE_SUB
    group_start = jnp.cumsum(padded) - padded
    rank = jnp.sum(ranks * onehot, axis=1)
    pos = group_start[e_flat] + rank
    src_rows = jnp.zeros((max_rows + gather_rows,), jnp.int32).at[pos].set(
        jnp.arange(n_assign, dtype=jnp.int32) // TOP_K)
    items_per_e = (padded + item_rows - 1) // item_rows
    item_start = jnp.cumsum(items_per_e) - items_per_e
    n_items = jnp.sum(items_per_e)
    it = jnp.arange(max_items)
    item_e = jnp.minimum(jnp.sum(it[:, None] >= (item_start + items_per_e)[None, :], axis=1), N_EXPERTS - 1)
    local = it - item_start[item_e]
    item_row = group_start[item_e] + local * item_rows
    item_sub = jnp.clip((padded[item_e] - local * item_rows) // MOE_SUB, 0, MOE_ITEM_SUBS)
    live = it < n_items
    last = jnp.maximum(n_items - 1, 0)
    item_e = jnp.where(live, item_e, item_e[last])
    item_row = jnp.where(live, item_row, 0)
    item_sub = jnp.where(live, item_sub, 0)
    i32 = lambda a: a.astype(jnp.int32)
    plan = jnp.stack([n_items, jnp.sum(padded)])
    return (i32(pos.reshape(T, TOP_K)), src_rows, i32(plan), i32(item_e), i32(item_row),
            i32(item_sub), max_rows, max_items)


def _moe_and_final_norm(h, g_ffn, w_router, w_gate, w_up, w_down, g_final):
    top_idx, top_w = _router(h, g_ffn, w_router)
    gather_rows = _gather_rows_per_item(w_gate.shape[3] // MOE_TF)
    pos, src_rows, plan, item_e, item_row, item_sub, max_rows, max_items = _moe_plan(top_idx, gather_rows)
    ys = _expert_ffn(h, g_ffn, w_gate, w_up, w_down, plan, item_e, item_row, item_sub, src_rows, max_rows,
                     max_items)
    return _combine_norm(h, ys, pos, top_w, g_final)


def kernel(x, w_in, w_branch, w_out, norm_mix, norm_ffn, norm_final, sinks, mla_q_norm, mla_kv_norm, mla_w_uq,
           mla_w_ukv, ffn_w_gate, ffn_w_up, ffn_w_down, router_w, moe_w_gate, moe_w_up, moe_w_down):
    nb, seq, d = x.shape
    assert (seq, d) == (SEQ, D_MODEL)
    depth = w_in.shape[0]
    assert depth == 2, "layer 0 uses the dense FFN, layer 1 the MoE followed by the final norm"
    cos64, sin64 = _rope_tables(HEAD_DIM, 0, HEAD_DIM)
    cos32, sin32 = _rope_tables(D_ROPE, D_NOPE, LANES)
    h = x.reshape(nb * seq, d)
    w_in_t = jnp.swapaxes(w_in, 1, 2)
    out = None
    for layer in range(depth):
        n = _rmsnorm(h, norm_mix[layer], BF16)
        proj = _in_proj(n, w_in_t, layer, cos64, sin64)
        o_a = _dilated_attention(proj, nb)
        o_b = _sink_attention(proj, sinks[layer], nb)
        o_c = _stick_attention(proj, nb)
        q_d, k_d, v_d = _mla_prep(proj, mla_q_norm[layer], mla_kv_norm[layer], mla_w_uq[layer], mla_w_ukv[layer],
                                  cos32, sin32)
        o_d = _mla_attention(q_d, k_d, v_d, nb)
        w_gates_t = _gate_weights_t(w_in_t, layer)
        merged = _merge(n, (o_a, o_b, o_c, o_d), w_gates_t, w_branch, layer)
        h = _matmul_res(merged, w_out, layer, h, 1024, 512)
        if layer == 0:
            n2 = _rmsnorm(h, norm_ffn[layer], BF16)
            mid = _swiglu_up(n2, ffn_w_gate, ffn_w_up, 0)
            h = _matmul_res(mid, ffn_w_down, 0, h, 1024, 256)
        else:
            out = _moe_and_final_norm(h, norm_ffn[layer], router_w[0], moe_w_gate, moe_w_up, moe_w_down, norm_final)
    return out.reshape(nb, seq, d)
```

```python
import functools

import numpy as np
import jax
import jax.numpy as jnp
from jax import lax
from jax.experimental import pallas as pl
from jax.experimental.pallas import tpu as pltpu

F32 = jnp.float32
BF16 = jnp.bfloat16

D_MODEL = 2048
SEQ = 2048
HEAD_DIM = 64
BLOCK = 128
LANES = 128
ROPE_THETA = 10000.0
NORM_EPS = 1e-6
DIL_PATTERNS = ((128, 1), (512, 4), (2048, 16))
B_WINDOW = 128
D_NOPE = 64
D_ROPE = 32
Q_LORA = 384
KV_LORA = 256
N_BRANCH = 4
BRANCH_WIDTH = 512
N_EXPERTS = 8
TOP_K = 2

A_Q0, A_K0, A_V0 = 0, 12, 24
B_Q0, B_K0, B_V0 = 36, 40, 41
C_Q0, C_K0, C_V0 = 42, 46, 50
MLA_COL0 = 6912
MLA_BLOCK_W = 768
QKV_WIDTH = 7680
GATE_COL0 = 7584
PROJ_TN = 512

VMEM_LIMIT = 56 * 1024 * 1024
NEG = -1e30
NT_DIMS = (((1,), (1,)), ((), ()))
LOG2_E = 1.4426950408889634

MOE_SUB = 128
MOE_ITEM_SUBS = 8
MOE_TF = 256


def _cparams(n_axes):
    return pltpu.CompilerParams(dimension_semantics=("arbitrary",) * n_axes, vmem_limit_bytes=VMEM_LIMIT)


def _lane_half():
    return lax.broadcasted_iota(jnp.int32, (BLOCK, LANES), 1) < HEAD_DIM


def _rms(x, g):
    return x * lax.rsqrt(jnp.mean(x * x, axis=-1, keepdims=True) + NORM_EPS) * g


def _rmsnorm_kernel(x_ref, g_ref, o_ref):
    o_ref[...] = _rms(x_ref[...], g_ref[...]).astype(o_ref.dtype)


def _rmsnorm(x, g, out_dtype, tm=512):
    T, D = x.shape
    return pl.pallas_call(
        _rmsnorm_kernel,
        out_shape=jax.ShapeDtypeStruct((T, D), out_dtype),
        grid=(T // tm,),
        in_specs=[pl.BlockSpec((tm, D), lambda i: (i, 0)), pl.BlockSpec((1, D), lambda i: (0, 0))],
        out_specs=pl.BlockSpec((tm, D), lambda i: (i, 0)),
        compiler_params=_cparams(1),
        name="rmsnorm",
    )(x, g.reshape(1, D))


def _rope_tables(dh, lane0, period):
    half = dh // 2
    freqs = ROPE_THETA ** (-2.0 * jnp.arange(half, dtype=F32) / dh)
    ang = jnp.arange(SEQ, dtype=F32)[:, None] * freqs[None, :]
    cos, sin = jnp.cos(ang), jnp.sin(ang)
    cos_h = jnp.concatenate([cos, cos], axis=1)
    sin_h = jnp.concatenate([-sin, sin], axis=1)
    ones = jnp.ones((SEQ, period - dh), F32)
    cos_p = jnp.concatenate([cos_h, ones], axis=1)
    sin_p = jnp.concatenate([sin_h, 0.0 * ones], axis=1)
    reps = LANES // period
    cos_t = jnp.roll(jnp.tile(cos_p, (1, reps)), lane0, axis=1)
    sin_t = jnp.roll(jnp.tile(sin_p, (1, reps)), lane0, axis=1)
    return cos_t, sin_t


def _rope_lanes(x, cos, sin, half, first_half_mask):
    fwd = pltpu.roll(x, LANES - half, 1)
    bwd = pltpu.roll(x, half, 1)
    return x * cos + jnp.where(first_half_mask, fwd, bwd) * sin


def _proj_kernel(x_ref, w_ref, cos_ref, sin_ref, o_ref):
    j = pl.program_id(1)
    tiles = PROJ_TN // LANES
    all_rope = (j < (A_V0 // tiles)) | (j == B_Q0 // tiles)
    first_rope = j == B_K0 // tiles
    assert A_V0 % tiles == 0 and B_Q0 % tiles == 0 and B_K0 % tiles == 0 and B_K0 - B_Q0 == tiles

    def project():
        return lax.dot_general(x_ref[...], w_ref[...].astype(BF16), NT_DIMS, preferred_element_type=F32)

    def store(acc, n_rope):
        lane = lax.broadcasted_iota(jnp.int32, (SEQ, LANES), 1)
        first = (lane % HEAD_DIM) < HEAD_DIM // 2
        for c in range(n_rope):
            cols = slice(c * LANES, (c + 1) * LANES)
            o_ref[:, cols] = _rope_lanes(acc[:, cols], cos_ref[...], sin_ref[...], HEAD_DIM // 2, first)
        if n_rope < tiles:
            o_ref[:, n_rope * LANES:] = acc[:, n_rope * LANES:]

    @pl.when(all_rope)
    def _():
        store(project(), tiles)

    @pl.when(first_rope)
    def _():
        store(project(), 1)

    @pl.when(jnp.logical_not(all_rope | first_rope))
    def _():
        o_ref[...] = project()


def _in_proj(n, w_in_t, layer, cos, sin):
    T, D = n.shape
    nb = T // SEQ
    return pl.pallas_call(
        _proj_kernel,
        out_shape=jax.ShapeDtypeStruct((T, QKV_WIDTH), F32),
        grid=(nb, QKV_WIDTH // PROJ_TN),
        in_specs=[
            pl.BlockSpec((SEQ, D), lambda b, j: (b, 0)),
            pl.BlockSpec((None, PROJ_TN, D), lambda b, j: (layer, j, 0)),
            pl.BlockSpec((SEQ, LANES), lambda b, j: (0, 0)),
            pl.BlockSpec((SEQ, LANES), lambda b, j: (0, 0)),
        ],
        out_specs=pl.BlockSpec((SEQ, PROJ_TN), lambda b, j: (b, j)),
        compiler_params=_cparams(2),
        name="in_proj_rope",
    )(n, w_in_t, cos, sin)


BAND_UNROLL = 8


def _band_scores(q, kcat, vcat, head, has_prev, strict_prev):
    half = _lane_half()
    q_scale = HEAD_DIM ** -0.5 * LOG2_E
    qh = jnp.where(half if head == 0 else jnp.logical_not(half), q * q_scale, 0.0).astype(BF16)
    s = lax.dot_general(qh, kcat, NT_DIMS, preferred_element_type=F32)
    row = lax.broadcasted_iota(jnp.int32, (BLOCK, 2 * BLOCK), 0)
    col = lax.broadcasted_iota(jnp.int32, (BLOCK, 2 * BLOCK), 1)
    dist = row + BLOCK - col
    max_dist = BLOCK - 1 if strict_prev else BLOCK
    first_col = jnp.where(has_prev, 0, BLOCK)
    valid = (dist >= 0) & (dist <= max_dist) & (col >= first_col)
    s = jnp.where(valid, s, NEG)
    m = jnp.max(s, axis=1, keepdims=True)
    p = jnp.exp2(s - m)
    l = jnp.sum(p, axis=1, keepdims=True)
    acc = jnp.dot(p.astype(BF16), vcat, preferred_element_type=F32)
    return acc, m, l


def _pair(a0, a1):
    return jnp.where(_lane_half(), a0, a1)


def _dilated_kernel(*refs):
    q_refs, k_refs, v_refs = refs[0:3], refs[3:6], refs[6:9]
    o_ref = refs[9]
    acc_sc, m_sc, l_sc = refs[10:13]
    n_blocks = SEQ // BLOCK

    for g, (win, dil) in enumerate(DIL_PATTERNS):
        assert win // dil == BLOCK and SEQ % (dil * BLOCK) == 0
        q_ref, k_ref, v_ref = q_refs[g], k_refs[g], v_refs[g]

        def rows(start, dil=dil):
            return pl.ds(start, BLOCK) if dil == 1 else pl.ds(start, BLOCK, stride=dil)

        def block(it, g=g, dil=dil, q_ref=q_ref, k_ref=k_ref, v_ref=v_ref, rows=rows):
            r = it % dil
            i = it // dil
            base = i * (BLOCK * dil) + r
            prev = jnp.maximum(base - BLOCK * dil, 0)
            q = q_ref[rows(base), :]
            kcat = jnp.concatenate([k_ref[rows(prev), :], k_ref[rows(base), :]], axis=0).astype(BF16)
            vcat = jnp.concatenate([v_ref[rows(prev), :], v_ref[rows(base), :]], axis=0).astype(BF16)
            a0, m0, l0 = _band_scores(q, kcat, vcat, 0, i > 0, False)
            a1, m1, l1 = _band_scores(q, kcat, vcat, 1, i > 0, False)
            acc_sc[g, rows(base), :] = _pair(a0, a1)
            m_sc[g, rows(base), :] = _pair(m0, m1)
            l_sc[g, rows(base), :] = _pair(l0, l1)

        def body(it, carry, block=block):
            for u in range(BAND_UNROLL):
                block(it + u * (n_blocks // BAND_UNROLL))
            return carry

        lax.fori_loop(0, n_blocks // BAND_UNROLL, body, 0)

    def combine(i, carry):
        rows = pl.ds(pl.multiple_of(i * BLOCK, BLOCK), BLOCK)
        m = [m_sc[g, rows, :] for g in range(len(DIL_PATTERNS))]
        m_all = jnp.maximum(jnp.maximum(m[0], m[1]), m[2])
        num = jnp.zeros((BLOCK, LANES), F32)
        den = jnp.zeros((BLOCK, LANES), F32)
        for g in range(len(DIL_PATTERNS)):
            a = jnp.exp2(m[g] - m_all)
            num = num + a * acc_sc[g, rows, :]
            den = den + a * l_sc[g, rows, :]
        o_ref[rows, :] = (num / den).astype(o_ref.dtype)
        return carry

    lax.fori_loop(0, n_blocks, combine, 0)


def _dilated_attention(proj, nb):
    T = proj.shape[0]
    specs = []
    for col0 in (A_Q0, A_K0, A_V0):
        for g in range(len(DIL_PATTERNS)):
            specs.append(pl.BlockSpec((SEQ, LANES), lambda b, hp, c=col0 + 4 * g: (b, c + hp)))
    return pl.pallas_call(
        _dilated_kernel,
        out_shape=jax.ShapeDtypeStruct((T, BRANCH_WIDTH), BF16),
        grid=(nb, BRANCH_WIDTH // LANES),
        in_specs=specs,
        out_specs=pl.BlockSpec((SEQ, LANES), lambda b, hp: (b, hp)),
        scratch_shapes=[pltpu.VMEM((len(DIL_PATTERNS), SEQ, LANES), F32)] * 3,
        compiler_params=_cparams(2),
        name="dilated_attention",
    )(*([proj] * 9))


def _sink_kernel(q_ref, k_ref, v_ref, sink_ref, o_ref, k_sc, v_sc):
    hp = pl.program_id(1)
    kv_head = hp // 2
    k = k_ref[...]
    v = v_ref[...]
    k_sw = pltpu.roll(k, HEAD_DIM, 1)
    v_sw = pltpu.roll(v, HEAD_DIM, 1)
    first = kv_head == 0
    k_sc[0] = jnp.where(first, k, k_sw).astype(BF16)
    k_sc[1] = jnp.where(first, k_sw, k).astype(BF16)
    v_sc[0] = jnp.where(first, v, v_sw).astype(BF16)
    v_sc[1] = jnp.where(first, v_sw, v).astype(BF16)
    sink = _pair(sink_ref[pl.ds(2 * hp, 1), :], sink_ref[pl.ds(2 * hp + 1, 1), :])

    def block(i):
        base = pl.multiple_of(i * BLOCK, BLOCK)
        prev = pl.multiple_of(jnp.maximum(base - BLOCK, 0), BLOCK)
        q = q_ref[pl.ds(base, BLOCK), :]
        out = []
        for a in range(2):
            kcat = jnp.concatenate([k_sc[a, pl.ds(prev, BLOCK), :], k_sc[a, pl.ds(base, BLOCK), :]], axis=0)
            vcat = jnp.concatenate([v_sc[a, pl.ds(prev, BLOCK), :], v_sc[a, pl.ds(base, BLOCK), :]], axis=0)
            out.append(_band_scores(q, kcat, vcat, a, i > 0, True))
        acc = _pair(out[0][0], out[1][0])
        m = _pair(out[0][1], out[1][1])
        l = _pair(out[0][2], out[1][2])
        m_new = jnp.maximum(m, sink)
        scale = jnp.exp2(m - m_new)
        den = l * scale + jnp.exp2(sink - m_new)
        o_ref[pl.ds(base, BLOCK), :] = (acc * scale / den).astype(o_ref.dtype)

    n_blocks = SEQ // BLOCK

    def body(it, carry):
        for u in range(BAND_UNROLL):
            block(it + u * (n_blocks // BAND_UNROLL))
        return carry

    lax.fori_loop(0, n_blocks // BAND_UNROLL, body, 0)


def _sink_attention(proj, sinks_l, nb):
    T = proj.shape[0]
    sink_tab = jnp.broadcast_to(sinks_l.astype(F32)[:, None] * LOG2_E, (8, LANES))
    return pl.pallas_call(
        _sink_kernel,
        out_shape=jax.ShapeDtypeStruct((T, BRANCH_WIDTH), BF16),
        grid=(nb, BRANCH_WIDTH // LANES),
        in_specs=[
            pl.BlockSpec((SEQ, LANES), lambda b, hp: (b, B_Q0 + hp)),
            pl.BlockSpec((SEQ, LANES), lambda b, hp: (b, B_K0)),
            pl.BlockSpec((SEQ, LANES), lambda b, hp: (b, B_V0)),
            pl.BlockSpec((8, LANES), lambda b, hp: (0, 0)),
        ],
        out_specs=pl.BlockSpec((SEQ, LANES), lambda b, hp: (b, hp)),
        scratch_shapes=[pltpu.VMEM((2, SEQ, LANES), BF16)] * 2,
        compiler_params=_cparams(2),
        name="sink_attention",
    )(proj, proj, proj, sink_tab)


STICK_TQ = 256
STICK_SUBS = STICK_TQ // BLOCK
STICK_STAGES = 5


def _stick_tiles():
    n_key_blocks, n_q_tiles = SEQ // BLOCK, SEQ // STICK_TQ
    tiles = []
    for j in reversed(range(n_key_blocks)):
        first = j // STICK_SUBS
        tiles.append((first, j, 1 + j % STICK_SUBS))
        tiles += [(qi, j, 0) for qi in range(first + 1, n_q_tiles)]
    idle = (0, n_key_blocks - 1, STICK_SUBS + 1)
    lag = STICK_STAGES - 1
    stages = [[idle] * s + tiles + [idle] * (lag - s) for s in range(STICK_STAGES)]
    table = np.asarray(stages, np.int32)
    return table.shape[1], [np.ascontiguousarray(table[:, :, c]).reshape(-1) for c in range(3)]


def _stick_masks():
    row = np.arange(STICK_TQ)[:, None]
    col = np.arange(BLOCK)[None, :]
    valid = [np.ones((STICK_TQ, BLOCK), bool)]
    valid += [col + r * BLOCK < row for r in range(STICK_SUBS)]
    valid += [np.zeros((STICK_TQ, BLOCK), bool)]
    valid = np.stack(valid)
    masks = np.stack([valid.astype(np.float32), np.where(valid, 0.0, NEG).astype(np.float32)], axis=1)
    return masks.reshape(-1, STICK_TQ, BLOCK)


def _stick_kernel(n_slots, tq_ref, tk_ref, tm_ref, q_ref, k_ref, v_ref, tri_ref, mask_ref, o_ref,
                  q_sc, kd_sc, vd_sc, carry_sc, acc_sc, z_buf, hl_buf, sums_buf, w_buf):
    half = _lane_half()
    q_sc[...] = (q_ref[...] * (HEAD_DIM ** -0.5 * LOG2_E)).astype(BF16)
    n_key_blocks = SEQ // BLOCK
    k = k_ref[...]
    v = v_ref[...]
    for j in range(n_key_blocks):
        rows = slice(j * BLOCK, (j + 1) * BLOCK)
        kd_sc[j, :BLOCK] = jnp.where(half, k[rows], 0.0).astype(BF16)
        kd_sc[j, BLOCK:] = jnp.where(half, 0.0, k[rows]).astype(BF16)
        vd_sc[j, :BLOCK] = jnp.where(half, v[rows], 0.0).astype(BF16)
        vd_sc[j, BLOCK:] = jnp.where(half, 0.0, v[rows]).astype(BF16)
    carry_sc[...] = jnp.zeros_like(carry_sc)
    acc_sc[...] = jnp.zeros_like(acc_sc)
    z_buf[...] = jnp.zeros_like(z_buf)
    hl_buf[...] = jnp.zeros_like(hl_buf)
    sums_buf[...] = jnp.zeros_like(sums_buf)
    w_buf[...] = jnp.zeros_like(w_buf)
    tri = tri_ref[...]

    def q_rows(qi):
        return pl.ds(pl.multiple_of(qi * STICK_TQ, STICK_TQ), STICK_TQ)

    def trip(t, phase):
        cur, other = phase % 2, (phase + 1) % 2
        i = 4 * n_slots + t
        acc_sc[q_rows(tq_ref[i]), :] += jnp.dot(w_buf[cur], vd_sc[tk_ref[i]], preferred_element_type=F32)
        for a in range(2):
            sums_buf[2 * cur + a] = jnp.dot(hl_buf[2 * cur + a], tri, preferred_element_type=F32)
        z_buf[phase] = lax.dot_general(q_sc[q_rows(tq_ref[t]), :], kd_sc[tk_ref[t]], NT_DIMS,
                                       preferred_element_type=F32)
        i = 3 * n_slots + t
        rows = q_rows(tq_ref[i])
        bias = mask_ref[2 * tm_ref[i] + 1]
        z = z_buf[(phase + 1) % 4]
        w = []
        for a in range(2):
            sums = sums_buf[2 * other + a]
            carry = carry_sc[a, rows, :]
            w.append(jnp.exp2(z[:, a * BLOCK:(a + 1) * BLOCK] + carry + sums[:, :BLOCK] + bias))
            carry_sc[a, rows, :] = carry + sums[:, BLOCK:]
        w_buf[other] = jnp.concatenate(w, axis=1).astype(BF16)
        keep = mask_ref[2 * tm_ref[n_slots + t]]
        z = z_buf[(phase + 3) % 4]
        for a in range(2):
            za = z[:, a * BLOCK:(a + 1) * BLOCK]
            sp = (jnp.maximum(za, 0.0) + jnp.log2(1.0 + jnp.exp2(-jnp.abs(za)))) * keep
            hi = sp.astype(BF16)
            lo = (sp - hi.astype(F32)).astype(BF16)
            hl_buf[2 * other + a] = jnp.concatenate([hi, lo], axis=1)

    def body(i, c):
        for phase in range(4):
            trip(4 * i + phase, phase)
        return c

    assert n_slots % 4 == 0
    lax.fori_loop(0, n_slots // 4, body, 0)
    o_ref[...] = acc_sc[...].astype(o_ref.dtype)


def _stick_attention(proj, nb):
    T = proj.shape[0]
    jp = np.arange(BLOCK)[:, None]
    s = np.arange(BLOCK)[None, :]
    tri = -np.concatenate([(jp >= s), np.ones((BLOCK, BLOCK), bool)], axis=1).astype(np.float32)
    tri = jnp.asarray(np.concatenate([tri, tri], axis=0), BF16)
    n_slots, tables = _stick_tiles()
    masks = _stick_masks()
    n_key_blocks = SEQ // BLOCK
    return pl.pallas_call(
        functools.partial(_stick_kernel, n_slots),
        out_shape=jax.ShapeDtypeStruct((T, BRANCH_WIDTH), BF16),
        grid_spec=pltpu.PrefetchScalarGridSpec(
            num_scalar_prefetch=3,
            grid=(nb, BRANCH_WIDTH // LANES),
            in_specs=[
                pl.BlockSpec((SEQ, LANES), lambda b, hp, *_: (b, C_Q0 + hp)),
                pl.BlockSpec((SEQ, LANES), lambda b, hp, *_: (b, C_K0 + hp)),
                pl.BlockSpec((SEQ, LANES), lambda b, hp, *_: (b, C_V0 + hp)),
                pl.BlockSpec((2 * BLOCK, 2 * BLOCK), lambda b, hp, *_: (0, 0)),
                pl.BlockSpec(masks.shape, lambda b, hp, *_: (0, 0, 0)),
            ],
            out_specs=pl.BlockSpec((SEQ, LANES), lambda b, hp, *_: (b, hp)),
            scratch_shapes=[
                pltpu.VMEM((SEQ, LANES), BF16),
                pltpu.VMEM((n_key_blocks, 2 * BLOCK, LANES), BF16),
                pltpu.VMEM((n_key_blocks, 2 * BLOCK, LANES), BF16),
                pltpu.VMEM((2, SEQ, LANES), F32),
                pltpu.VMEM((SEQ, LANES), F32),
                pltpu.VMEM((4, STICK_TQ, 2 * BLOCK), F32),
                pltpu.VMEM((4, STICK_TQ, 2 * BLOCK), BF16),
                pltpu.VMEM((4, STICK_TQ, 2 * BLOCK), F32),
                pltpu.VMEM((2, STICK_TQ, 2 * BLOCK), BF16),
            ],
        ),
        compiler_params=_cparams(2),
        name="stick_breaking_attention",
    )(*[jnp.asarray(t) for t in tables], proj, proj, proj, tri, jnp.asarray(masks))


MLA_HEADS = 8
MLA_TM = 512


def _mla_prep_kernel(c_ref, gq_ref, gkv_ref, wq_ref, wk_ref, wv_ref, cos_ref, sin_ref, q_ref, k_ref, v_ref):
    c = c_ref[...]
    cq = _rms(c[:, :Q_LORA], gq_ref[...]).astype(BF16)
    ckv = _rms(c[:, Q_LORA:Q_LORA + KV_LORA], gkv_ref[...]).astype(BF16)
    q = jnp.dot(cq, wq_ref[...].astype(BF16), preferred_element_type=F32)
    k = jnp.dot(ckv, wk_ref[...].astype(BF16), preferred_element_type=F32)
    v_ref[...] = jnp.dot(ckv, wv_ref[...].astype(BF16), preferred_element_type=F32).astype(BF16)
    cos, sin = cos_ref[...], sin_ref[...]
    lane = lax.broadcasted_iota(jnp.int32, (MLA_TM, LANES), 1)
    rope_lane = (lane >= D_NOPE) & (lane < D_NOPE + D_ROPE)
    first = lane < D_NOPE + D_ROPE // 2
    kr = pltpu.roll(c[:, Q_LORA + KV_LORA:], D_NOPE, 1)
    kr = jnp.where(rope_lane, _rope_lanes(kr, cos, sin, D_ROPE // 2, first), 0.0)
    scale = (D_NOPE + D_ROPE) ** -0.5 * LOG2_E
    for h in range(MLA_HEADS):
        sl = slice(h * LANES, (h + 1) * LANES)
        q_ref[:, sl] = (_rope_lanes(q[:, sl], cos, sin, D_ROPE // 2, first) * scale).astype(BF16)
        k_ref[:, sl] = (k[:, sl] + kr).astype(BF16)


def _mla_prep(proj, g_qa, g_kva, w_uq, w_ukv, cos, sin):
    T = proj.shape[0]
    wq = jnp.pad(w_uq.reshape(Q_LORA, MLA_HEADS, D_NOPE + D_ROPE), ((0, 0), (0, 0), (0, LANES - D_NOPE - D_ROPE)))
    wq = wq.reshape(Q_LORA, MLA_HEADS * LANES)
    wkv = w_ukv.reshape(KV_LORA, MLA_HEADS, 2 * HEAD_DIM)
    wk = jnp.pad(wkv[:, :, :D_NOPE], ((0, 0), (0, 0), (0, LANES - D_NOPE))).reshape(KV_LORA, MLA_HEADS * LANES)
    wv = wkv[:, :, D_NOPE:].reshape(KV_LORA, MLA_HEADS * HEAD_DIM)
    full = lambda shape: pl.BlockSpec(shape, lambda i: (0, 0))
    per_seq = SEQ // MLA_TM
    return pl.pallas_call(
        _mla_prep_kernel,
        out_shape=[jax.ShapeDtypeStruct((T, MLA_HEADS * LANES), BF16),
                   jax.ShapeDtypeStruct((T, MLA_HEADS * LANES), BF16),
                   jax.ShapeDtypeStruct((T, MLA_HEADS * HEAD_DIM), BF16)],
        grid=(T // MLA_TM,),
        in_specs=[
            pl.BlockSpec((MLA_TM, MLA_BLOCK_W), lambda i: (i, MLA_COL0 // MLA_BLOCK_W)),
            full((1, Q_LORA)), full((1, KV_LORA)),
            full(wq.shape), full(wk.shape), full(wv.shape),
            pl.BlockSpec((MLA_TM, LANES), lambda i: (i % per_seq, 0)),
            pl.BlockSpec((MLA_TM, LANES), lambda i: (i % per_seq, 0)),
        ],
        out_specs=[pl.BlockSpec((MLA_TM, MLA_HEADS * LANES), lambda i: (i, 0)),
                   pl.BlockSpec((MLA_TM, MLA_HEADS * LANES), lambda i: (i, 0)),
                   pl.BlockSpec((MLA_TM, MLA_HEADS * HEAD_DIM), lambda i: (i, 0))],
        compiler_params=_cparams(1),
        name="mla_prep",
    )(proj, g_qa.reshape(1, Q_LORA), g_kva.reshape(1, KV_LORA), wq, wk, wv, cos, sin)


MLA_TQ = 1024
MLA_TK = 256


def _mla_attn_kernel(q_ref, k_ref, v_ref, o_ref, v_sc, m_sc, acc_sc):
    half = lax.broadcasted_iota(jnp.int32, (SEQ, LANES), 1) < HEAD_DIM
    v = v_ref[...]
    one = jnp.ones_like(v)
    v_sc[0] = jnp.where(half, v, one)
    v_sc[1] = jnp.where(half, one, v)
    lane_half = lax.broadcasted_iota(jnp.int32, (MLA_TQ, LANES), 1) < HEAD_DIM

    def step(qb, kb, diagonal):
        for a in range(2):
            q = q_ref[pl.ds(qb, MLA_TQ), a * LANES:(a + 1) * LANES]
            k = k_ref[pl.ds(kb, MLA_TK), a * LANES:(a + 1) * LANES]
            s = lax.dot_general(q, k, (((1,), (1,)), ((), ())), preferred_element_type=F32)
            if diagonal:
                row = lax.broadcasted_iota(jnp.int32, (MLA_TQ, MLA_TK), 0)
                col = lax.broadcasted_iota(jnp.int32, (MLA_TQ, MLA_TK), 1)
                s = jnp.where(col + (kb - qb) <= row, s, NEG)
            m = m_sc[a]
            m_new = jnp.maximum(m, jnp.max(s, axis=1, keepdims=True))
            alpha = jnp.exp2(m - m_new)
            p = jnp.exp2(s - jnp.concatenate([m_new] * (MLA_TK // LANES), axis=1))
            pv = jnp.dot(p.astype(BF16), v_sc[a, pl.ds(kb, MLA_TK), :], preferred_element_type=F32)
            acc_sc[a] = alpha * acc_sc[a] + pv
            m_sc[a] = m_new

    def q_block(qi, c):
        qb = pl.multiple_of(qi * MLA_TQ, MLA_TQ)
        m_sc[...] = jnp.full(m_sc.shape, NEG, F32)
        acc_sc[...] = jnp.zeros(acc_sc.shape, F32)

        def k_block(kj, c2):
            step(qb, pl.multiple_of(kj * MLA_TK, MLA_TK), False)
            return c2

        lax.fori_loop(0, qi * (MLA_TQ // MLA_TK), k_block, 0)
        for d in range(MLA_TQ // MLA_TK):
            step(qb, qb + d * MLA_TK, True)
        outs = [acc_sc[a] / pltpu.roll(acc_sc[a], HEAD_DIM, 1) for a in range(2)]
        o_ref[pl.ds(qb, MLA_TQ), :] = jnp.where(lane_half, outs[0], outs[1]).astype(o_ref.dtype)
        return c

    lax.fori_loop(0, SEQ // MLA_TQ, q_block, 0)


def _mla_attention(q, k, v, nb):
    T = q.shape[0]
    return pl.pallas_call(
        _mla_attn_kernel,
        out_shape=jax.ShapeDtypeStruct((T, BRANCH_WIDTH), BF16),
        grid=(nb, BRANCH_WIDTH // LANES),
        in_specs=[
            pl.BlockSpec((SEQ, 2 * LANES), lambda b, hp: (b, hp)),
            pl.BlockSpec((SEQ, 2 * LANES), lambda b, hp: (b, hp)),
            pl.BlockSpec((SEQ, LANES), lambda b, hp: (b, hp)),
        ],
        out_specs=pl.BlockSpec((SEQ, LANES), lambda b, hp: (b, hp)),
        scratch_shapes=[pltpu.VMEM((2, SEQ, LANES), BF16),
                        pltpu.VMEM((2, MLA_TQ, LANES), F32),
                        pltpu.VMEM((2, MLA_TQ, LANES), F32)],
        compiler_params=_cparams(2),
        name="mla_attention",
    )(q, k, v)


MERGE_TM = 1024
MERGE_TN = 256


def _merge_kernel(*refs):
    n_ref = refs[0]
    o_refs = refs[1:5]
    wg_refs = refs[5:9]
    wb_refs = refs[9:13]
    out_ref = refs[13]
    n = n_ref[...]
    acc = jnp.zeros((MERGE_TM, MERGE_TN), F32)
    for i in range(N_BRANCH):
        gate = lax.dot_general(n, wg_refs[i][...], NT_DIMS, preferred_element_type=F32)
        br = jnp.dot(o_refs[i][...], wb_refs[i][...].astype(BF16), preferred_element_type=F32)
        acc = acc + br * jax.nn.sigmoid(gate)
    out_ref[...] = acc.astype(out_ref.dtype)


GATE_CAST_ROWS = 256


def _gate_cast_kernel(layer, w_hbm, o_ref, buf, sem):
    j = pl.program_id(0)

    def copy(step, slot):
        rows = pl.ds(pl.multiple_of(GATE_COL0 + step * GATE_CAST_ROWS, 8), GATE_CAST_ROWS)
        return pltpu.make_async_copy(w_hbm.at[layer, rows, :], buf.at[slot], sem.at[slot])

    @pl.when(j == 0)
    def _():
        copy(0, 0).start()

    @pl.when(j + 1 < pl.num_programs(0))
    def _():
        copy(j + 1, (j + 1) % 2).start()

    copy(j, j % 2).wait()
    o_ref[...] = buf[j % 2].astype(o_ref.dtype)


def _gate_weights_t(w_in_t, layer):
    n_gate = N_BRANCH * D_MODEL
    D = w_in_t.shape[2]
    assert GATE_COL0 + n_gate == w_in_t.shape[1] and GATE_COL0 % 8 == 0
    return pl.pallas_call(
        functools.partial(_gate_cast_kernel, layer),
        out_shape=jax.ShapeDtypeStruct((n_gate, D), BF16),
        grid=(n_gate // GATE_CAST_ROWS,),
        in_specs=[pl.BlockSpec(memory_space=pl.ANY)],
        out_specs=pl.BlockSpec((GATE_CAST_ROWS, D), lambda j: (j, 0)),
        scratch_shapes=[pltpu.VMEM((2, GATE_CAST_ROWS, D), F32), pltpu.SemaphoreType.DMA((2,))],
        compiler_params=_cparams(1),
        name="gate_weight_cast",
    )(w_in_t)


def _merge(n, branches, w_gates_t, w_branch, layer):
    T, D = n.shape
    nblk = D // MERGE_TN
    in_specs = [pl.BlockSpec((MERGE_TM, D), lambda m, j: (m, 0))]
    in_specs += [pl.BlockSpec((MERGE_TM, BRANCH_WIDTH), lambda m, j: (m, 0))] * N_BRANCH
    in_specs += [pl.BlockSpec((MERGE_TN, D), lambda m, j, i=i: (i * nblk + j, 0)) for i in range(N_BRANCH)]
    in_specs += [pl.BlockSpec((None, None, BRANCH_WIDTH, MERGE_TN), lambda m, j, i=i: (layer, i, 0, j))
                 for i in range(N_BRANCH)]
    return pl.pallas_call(
        _merge_kernel,
        out_shape=jax.ShapeDtypeStruct((T, D), BF16),
        grid=(T // MERGE_TM, nblk),
        in_specs=in_specs,
        out_specs=pl.BlockSpec((MERGE_TM, MERGE_TN), lambda m, j: (m, j)),
        compiler_params=_cparams(2),
        name="gated_merge",
    )(n, *branches, *([w_gates_t] * N_BRANCH), *([w_branch] * N_BRANCH))


def _matmul_res_kernel(a_ref, w_ref, r_ref, o_ref):
    o_ref[...] = r_ref[...] + jnp.dot(a_ref[...], w_ref[...].astype(BF16), preferred_element_type=F32)


def _matmul_res(a, w, layer, res, tm, tn):
    M, K = a.shape
    N = w.shape[2]
    return pl.pallas_call(
        _matmul_res_kernel,
        out_shape=jax.ShapeDtypeStruct((M, N), F32),
        grid=(M // tm, N // tn),
        in_specs=[pl.BlockSpec((tm, K), lambda m, j: (m, 0)),
                  pl.BlockSpec((None, K, tn), lambda m, j: (layer, 0, j)),
                  pl.BlockSpec((tm, tn), lambda m, j: (m, j))],
        out_specs=pl.BlockSpec((tm, tn), lambda m, j: (m, j)),
        compiler_params=_cparams(2),
        name="matmul_residual",
    )(a, w, res)


def _swiglu_up_kernel(x_ref, wg_ref, wu_ref, o_ref):
    x = x_ref[...]
    g = jnp.dot(x, wg_ref[...].astype(BF16), preferred_element_type=F32)
    u = jnp.dot(x, wu_ref[...].astype(BF16), preferred_element_type=F32)
    o_ref[...] = (jax.nn.silu(g) * u).astype(o_ref.dtype)


def _swiglu_up(x, wg, wu, layer, tm=1024, tn=512):
    M, K = x.shape
    N = wg.shape[2]
    return pl.pallas_call(
        _swiglu_up_kernel,
        out_shape=jax.ShapeDtypeStruct((M, N), BF16),
        grid=(M // tm, N // tn),
        in_specs=[pl.BlockSpec((tm, K), lambda m, j: (m, 0)),
                  pl.BlockSpec((None, K, tn), lambda m, j: (layer, 0, j)),
                  pl.BlockSpec((None, K, tn), lambda m, j: (layer, 0, j))],
        out_specs=pl.BlockSpec((tm, tn), lambda m, j: (m, j)),
        compiler_params=_cparams(2),
        name="swiglu_up",
    )(x, wg, wu)


ROUTER_TM = 512


def _router_kernel(h_ref, g_ref, wr_ref, idx_ref, w_ref):
    n = _rms(h_ref[...], g_ref[...])
    wr = wr_ref[...]
    n_hi = n.astype(BF16)
    n_lo = (n - n_hi.astype(F32)).astype(BF16)
    w_hi = wr.astype(BF16)
    w_lo = (wr - w_hi.astype(F32)).astype(BF16)
    logits = (jnp.dot(n_hi, w_hi, preferred_element_type=F32) + jnp.dot(n_lo, w_hi, preferred_element_type=F32)
              + jnp.dot(n_hi, w_lo, preferred_element_type=F32))
    lane = lax.broadcasted_iota(jnp.int32, logits.shape, 1)
    v1 = jnp.max(logits, axis=1, keepdims=True)
    i1 = jnp.min(jnp.where(logits == v1, lane, N_EXPERTS), axis=1, keepdims=True)
    rest = jnp.where(lane == i1, -jnp.inf, logits)
    v2 = jnp.max(rest, axis=1, keepdims=True)
    i2 = jnp.min(jnp.where(rest == v2, lane, N_EXPERTS), axis=1, keepdims=True)
    e2 = jnp.exp(v2 - v1)
    den = 1.0 + e2
    two = lax.broadcasted_iota(jnp.int32, (ROUTER_TM, TOP_K), 1)
    idx_ref[...] = jnp.where(two == 0, i1, i2)
    w_ref[...] = jnp.where(two == 0, 1.0 / den, e2 / den)


def _router(h, g, w_router):
    T, D = h.shape
    return pl.pallas_call(
        _router_kernel,
        out_shape=[jax.ShapeDtypeStruct((T, TOP_K), jnp.int32), jax.ShapeDtypeStruct((T, TOP_K), F32)],
        grid=(T // ROUTER_TM,),
        in_specs=[pl.BlockSpec((ROUTER_TM, D), lambda i: (i, 0)),
                  pl.BlockSpec((1, D), lambda i: (0, 0)),
                  pl.BlockSpec((D, N_EXPERTS), lambda i: (0, 0))],
        out_specs=[pl.BlockSpec((ROUTER_TM, TOP_K), lambda i: (i, 0)),
                   pl.BlockSpec((ROUTER_TM, TOP_K), lambda i: (i, 0))],
        compiler_params=_cparams(1),
        name="moe_router",
    )(h, g.reshape(1, D), w_router)


def _row_copy(src_hbm, row, dst_vmem, slot, sem):
    return pltpu.make_async_copy(src_hbm.at[pl.ds(row, 1), :], dst_vmem.at[pl.ds(slot, 1), :], sem)


def _expert_kernel(n_f, plan_ref, item_e_ref, item_row_ref, item_sub_ref, src_ref,
                   h_hbm, g_ref, wg_ref, wu_ref, wd_ref, y_hbm, gbuf, x_sc, acc_sc, sem_in, sem_out):
    it = pl.program_id(0)
    f = pl.program_id(1)
    n_it = pl.num_programs(0)
    n_items = plan_ref[0]
    live = it < n_items
    row0 = pl.multiple_of(item_row_ref[it], MOE_SUB)
    n_sub = item_sub_ref[it]
    item_rows = x_sc.shape[0]
    g_rows = gbuf.shape[0]
    chunk = g_rows // n_f

    def gather_wait():
        pltpu.make_async_copy(h_hbm.at[pl.ds(0, g_rows), :], gbuf, sem_in).wait()

    def y_copy(s):
        rows = pl.ds(s * MOE_SUB, MOE_SUB)
        return pltpu.make_async_copy(acc_sc.at[rows, :], y_hbm.at[pl.ds(row0 + s * MOE_SUB, MOE_SUB), :], sem_out)

    @pl.when(live & (it == 0) & (f == 0))
    def _():
        def start(r, c):
            _row_copy(h_hbm, src_ref[row0 + r], gbuf, r, sem_in).start()
            return c
        lax.fori_loop(0, g_rows, start, 0)

    @pl.when(live & (f == 0))
    def _():
        gather_wait()

        def norm(i, c):
            rows = pl.ds(pl.multiple_of(i * MOE_SUB, MOE_SUB), MOE_SUB)
            x_sc[rows, :] = _rms(gbuf[rows, :], g_ref[...]).astype(BF16)
            return c
        lax.fori_loop(0, item_rows // MOE_SUB, norm, 0)
        acc_sc[...] = jnp.zeros(acc_sc.shape, F32)

    @pl.when((it == n_items) & (it > 0) & (f == 0))
    def _():
        gather_wait()

    next_row0 = item_row_ref[jnp.minimum(it + 1, n_it - 1)]

    for n in range(1, MOE_ITEM_SUBS + 1):
        @pl.when(live & (n_sub == n))
        def _(n=n):
            lo = f * chunk
            for r in range(chunk):
                _row_copy(h_hbm, src_ref[next_row0 + lo + r], gbuf, lo + r, sem_in).start()
            rows = n * MOE_SUB
            x = x_sc[:rows, :]
            g = jnp.dot(x, wg_ref[...].astype(BF16), preferred_element_type=F32)
            u = jnp.dot(x, wu_ref[...].astype(BF16), preferred_element_type=F32)
            mid = (jax.nn.silu(g) * u).astype(BF16)
            acc_sc[:rows, :] += jnp.dot(mid, wd_ref[...].astype(BF16), preferred_element_type=F32)

    @pl.when(live & (f == n_f - 1))
    def _():
        for s in range(MOE_ITEM_SUBS):
            @pl.when(s < n_sub)
            def _(s=s):
                y_copy(s).start()
        for s in range(MOE_ITEM_SUBS):
            @pl.when(s < n_sub)
            def _(s=s):
                y_copy(s).wait()

    @pl.when((it == n_it - 1) & (f == n_f - 1))
    def _():
        @pl.when(live)
        def _():
            gather_wait()

        acc_sc[:MOE_SUB, :] = jnp.zeros((MOE_SUB, acc_sc.shape[1]), F32)
        first = plan_ref[1] // MOE_SUB
        n_blocks = y_hbm.shape[0] // MOE_SUB

        def fill(s):
            dst = y_hbm.at[pl.ds(pl.multiple_of(s * MOE_SUB, MOE_SUB), MOE_SUB), :]
            return pltpu.make_async_copy(acc_sc.at[pl.ds(0, MOE_SUB), :], dst, sem_out)

        lax.fori_loop(first, n_blocks, lambda s, c: (fill(s).start(), c)[1], 0)
        lax.fori_loop(first, n_blocks, lambda s, c: (fill(s).wait(), c)[1], 0)


def _gather_rows_per_item(n_f):
    item_rows = MOE_SUB * MOE_ITEM_SUBS
    chunk = -(-item_rows // n_f)
    chunk += -chunk % 8
    return chunk * n_f


def _expert_ffn(h, g, w_gate, w_up, w_down, plan, item_e, item_row, item_sub, src_rows, max_rows, max_items):
    D = h.shape[1]
    n_f = w_gate.shape[3] // MOE_TF
    item_rows = MOE_SUB * MOE_ITEM_SUBS

    def f_of(it, f, plan_ref):
        return jnp.where(it < plan_ref[0], f, n_f - 1)

    return pl.pallas_call(
        functools.partial(_expert_kernel, n_f),
        out_shape=jax.ShapeDtypeStruct((max_rows, D), F32),
        grid_spec=pltpu.PrefetchScalarGridSpec(
            num_scalar_prefetch=5,
            grid=(max_items, n_f),
            in_specs=[
                pl.BlockSpec(memory_space=pl.ANY),
                pl.BlockSpec((1, D), lambda it, f, n, e, r, s, src: (0, 0)),
                pl.BlockSpec((None, None, D, MOE_TF), lambda it, f, n, e, r, s, src: (0, e[it], 0, f_of(it, f, n))),
                pl.BlockSpec((None, None, D, MOE_TF), lambda it, f, n, e, r, s, src: (0, e[it], 0, f_of(it, f, n))),
                pl.BlockSpec((None, None, MOE_TF, D), lambda it, f, n, e, r, s, src: (0, e[it], f_of(it, f, n), 0)),
            ],
            out_specs=pl.BlockSpec(memory_space=pl.ANY),
            scratch_shapes=[pltpu.VMEM((_gather_rows_per_item(n_f), D), F32),
                            pltpu.VMEM((item_rows, D), BF16),
                            pltpu.VMEM((item_rows, D), F32),
                            pltpu.SemaphoreType.DMA(()), pltpu.SemaphoreType.DMA(())],
        ),
        compiler_params=_cparams(2),
        name="moe_expert_ffn",
    )(plan, item_e, item_row, item_sub, src_rows, h, g.reshape(1, D), w_gate, w_up, w_down)


COMBINE_TM = 256


def _combine_kernel(pos_ref, h_ref, w_ref, g_ref, y_hbm, o_ref, buf, sem):
    base = pl.program_id(0) * COMBINE_TM

    def start(r, c):
        for k in range(TOP_K):
            _row_copy(y_hbm, pos_ref[(base + r) * TOP_K + k], buf.at[k], r, sem).start()
        return c

    lax.fori_loop(0, COMBINE_TM, start, 0, unroll=8)
    for k in range(TOP_K):
        pltpu.make_async_copy(y_hbm.at[pl.ds(0, COMBINE_TM), :], buf.at[k], sem).wait()
    w = w_ref[...]
    out = h_ref[...] + w[:, 0:1] * buf[0] + w[:, 1:2] * buf[1]
    o_ref[...] = _rms(out, g_ref[...])


def _combine_norm(h, ys, pos, top_w, g):
    T, D = h.shape
    return pl.pallas_call(
        _combine_kernel,
        out_shape=jax.ShapeDtypeStruct((T, D), F32),
        grid_spec=pltpu.PrefetchScalarGridSpec(
            num_scalar_prefetch=1,
            grid=(T // COMBINE_TM,),
            in_specs=[pl.BlockSpec((COMBINE_TM, D), lambda i, p: (i, 0)),
                      pl.BlockSpec((COMBINE_TM, TOP_K), lambda i, p: (i, 0)),
                      pl.BlockSpec((1, D), lambda i, p: (0, 0)),
                      pl.BlockSpec(memory_space=pl.ANY)],
            out_specs=pl.BlockSpec((COMBINE_TM, D), lambda i, p: (i, 0)),
            scratch_shapes=[pltpu.VMEM((TOP_K, COMBINE_TM, D), F32), pltpu.SemaphoreType.DMA(())],
        ),
        compiler_params=_cparams(1),
        name="moe_combine_norm",
    )(pos.reshape(-1), h, top_w, g.reshape(1, D), ys)


def _moe_plan(top_idx, gather_rows):
    T = top_idx.shape[0]
    n_assign = T * TOP_K
    item_rows = MOE_SUB * MOE_ITEM_SUBS
    max_rows = n_assign + N_EXPERTS * MOE_SUB
    max_items = n_assign // item_rows + N_EXPERTS
    e_flat = top_idx.reshape(-1)
    onehot = (e_flat[:, None] == jnp.arange(N_EXPERTS)[None, :]).astype(jnp.int32)
    ranks = jnp.cumsum(onehot, axis=0) - onehot
    counts = jnp.sum(onehot, axis=0)
    padded = ((counts + MOE_SUB - 1) // MOE_SUB) * MOE_SUB
    group_start = jnp.cumsum(padded) - padded
    rank = jnp.sum(ranks * onehot, axis=1)
    pos = group_start[e_flat] + rank
    src_rows = jnp.zeros((max_rows + gather_rows,), jnp.int32).at[pos].set(
        jnp.arange(n_assign, dtype=jnp.int32) // TOP_K, unique_indices=True)
    items_per_e = (padded + item_rows - 1) // item_rows
    item_start = jnp.cumsum(items_per_e) - items_per_e
    n_items = jnp.sum(items_per_e)
    it = jnp.arange(max_items)
    item_e = jnp.minimum(jnp.sum(it[:, None] >= (item_start + items_per_e)[None, :], axis=1), N_EXPERTS - 1)
    local = it - item_start[item_e]
    item_row = group_start[item_e] + local * item_rows
    item_sub = jnp.clip((padded[item_e] - local * item_rows) // MOE_SUB, 0, MOE_ITEM_SUBS)
    live = it < n_items
    last = jnp.maximum(n_items - 1, 0)
    item_e = jnp.where(live, item_e, item_e[last])
    item_row = jnp.where(live, item_row, 0)
    item_sub = jnp.where(live, item_sub, 0)
    i32 = lambda a: a.astype(jnp.int32)
    plan = jnp.stack([n_items, jnp.sum(padded)])
    return (i32(pos.reshape(T, TOP_K)), src_rows, i32(plan), i32(item_e), i32(item_row),
            i32(item_sub), max_rows, max_items)


def _moe_and_final_norm(h, g_ffn, w_router, w_gate, w_up, w_down, g_final):
    top_idx, top_w = _router(h, g_ffn, w_router)
    gather_rows = _gather_rows_per_item(w_gate.shape[3] // MOE_TF)
    pos, src_rows, plan, item_e, item_row, item_sub, max_rows, max_items = _moe_plan(top_idx, gather_rows)
    ys = _expert_ffn(h, g_ffn, w_gate, w_up, w_down, plan, item_e, item_row, item_sub, src_rows, max_rows,
                     max_items)
    return _combine_norm(h, ys, pos, top_w, g_final)


def kernel(x, w_in, w_branch, w_out, norm_mix, norm_ffn, norm_final, sinks, mla_q_norm, mla_kv_norm, mla_w_uq,
           mla_w_ukv, ffn_w_gate, ffn_w_up, ffn_w_down, router_w, moe_w_gate, moe_w_up, moe_w_down):
    nb, seq, d = x.shape
    assert (seq, d) == (SEQ, D_MODEL)
    depth = w_in.shape[0]
    assert depth == 2, "layer 0 uses the dense FFN, layer 1 the MoE followed by the final norm"
    cos64, sin64 = _rope_tables(HEAD_DIM, 0, HEAD_DIM)
    cos32, sin32 = _rope_tables(D_ROPE, D_NOPE, LANES)
    h = x.reshape(nb * seq, d)
    w_in_t = jnp.swapaxes(w_in, 1, 2)
    out = None
    for layer in range(depth):
        n = _rmsnorm(h, norm_mix[layer], BF16)
        proj = _in_proj(n, w_in_t, layer, cos64, sin64)
        o_a = _dilated_attention(proj, nb)
        o_b = _sink_attention(proj, sinks[layer], nb)
        o_c = _stick_attention(proj, nb)
        q_d, k_d, v_d = _mla_prep(proj, mla_q_norm[layer], mla_kv_norm[layer], mla_w_uq[layer], mla_w_ukv[layer],
                                  cos32, sin32)
        o_d = _mla_attention(q_d, k_d, v_d, nb)
        w_gates_t = _gate_weights_t(w_in_t, layer)
        merged = _merge(n, (o_a, o_b, o_c, o_d), w_gates_t, w_branch, layer)
        h = _matmul_res(merged, w_out, layer, h, 2048, 512)
        if layer == 0:
            n2 = _rmsnorm(h, norm_ffn[layer], BF16)
            mid = _swiglu_up(n2, ffn_w_gate, ffn_w_up, 0)
            h = _matmul_res(mid, ffn_w_down, 0, h, 1024, 256)
        else:
            out = _moe_and_final_norm(h, norm_ffn[layer], router_w[0], moe_w_gate, moe_w_up, moe_w_down, norm_final)
    return out.reshape(nb, seq, d)
```

```python
import functools

import numpy as np
import jax
import jax.numpy as jnp
from jax import lax
from jax.experimental import pallas as pl
from jax.experimental.pallas import tpu as pltpu

F32 = jnp.float32
BF16 = jnp.bfloat16

D_MODEL = 2048
SEQ = 2048
HEAD_DIM = 64
BLOCK = 128
LANES = 128
ROPE_THETA = 10000.0
NORM_EPS = 1e-6
DIL_PATTERNS = ((128, 1), (512, 4), (2048, 16))
B_WINDOW = 128
D_NOPE = 64
D_ROPE = 32
Q_LORA = 384
KV_LORA = 256
N_BRANCH = 4
BRANCH_WIDTH = 512
N_EXPERTS = 8
TOP_K = 2

A_Q0, A_K0, A_V0 = 0, 12, 24
B_Q0, B_K0, B_V0 = 36, 40, 41
C_Q0, C_K0, C_V0 = 42, 46, 50
MLA_COL0 = 6912
MLA_BLOCK_W = 768
QKV_WIDTH = 7680
GATE_COL0 = 7584
PROJ_TN = 512

VMEM_LIMIT = 56 * 1024 * 1024
NEG = -1e30
NT_DIMS = (((1,), (1,)), ((), ()))
LOG2_E = 1.4426950408889634

MOE_SUB = 128
MOE_ITEM_SUBS = 8
MOE_TF = 256


def _cparams(n_axes):
    return pltpu.CompilerParams(dimension_semantics=("arbitrary",) * n_axes, vmem_limit_bytes=VMEM_LIMIT)


def _lane_half():
    return lax.broadcasted_iota(jnp.int32, (BLOCK, LANES), 1) < HEAD_DIM


def _rms(x, g):
    return x * lax.rsqrt(jnp.mean(x * x, axis=-1, keepdims=True) + NORM_EPS) * g


def _rmsnorm_kernel(x_ref, g_ref, o_ref):
    o_ref[...] = _rms(x_ref[...], g_ref[...]).astype(o_ref.dtype)


def _rmsnorm(x, g, out_dtype, tm=512):
    T, D = x.shape
    return pl.pallas_call(
        _rmsnorm_kernel,
        out_shape=jax.ShapeDtypeStruct((T, D), out_dtype),
        grid=(T // tm,),
        in_specs=[pl.BlockSpec((tm, D), lambda i: (i, 0)), pl.BlockSpec((1, D), lambda i: (0, 0))],
        out_specs=pl.BlockSpec((tm, D), lambda i: (i, 0)),
        compiler_params=_cparams(1),
        name="rmsnorm",
    )(x, g.reshape(1, D))


def _rope_tables(dh, lane0, period):
    half = dh // 2
    freqs = ROPE_THETA ** (-2.0 * jnp.arange(half, dtype=F32) / dh)
    ang = jnp.arange(SEQ, dtype=F32)[:, None] * freqs[None, :]
    cos, sin = jnp.cos(ang), jnp.sin(ang)
    cos_h = jnp.concatenate([cos, cos], axis=1)
    sin_h = jnp.concatenate([-sin, sin], axis=1)
    ones = jnp.ones((SEQ, period - dh), F32)
    cos_p = jnp.concatenate([cos_h, ones], axis=1)
    sin_p = jnp.concatenate([sin_h, 0.0 * ones], axis=1)
    reps = LANES // period
    cos_t = jnp.roll(jnp.tile(cos_p, (1, reps)), lane0, axis=1)
    sin_t = jnp.roll(jnp.tile(sin_p, (1, reps)), lane0, axis=1)
    return cos_t, sin_t


def _rope_lanes(x, cos, sin, half, first_half_mask):
    fwd = pltpu.roll(x, LANES - half, 1)
    bwd = pltpu.roll(x, half, 1)
    return x * cos + jnp.where(first_half_mask, fwd, bwd) * sin


def _proj_kernel(x_ref, w_ref, cos_ref, sin_ref, o_ref):
    j = pl.program_id(1)
    tiles = PROJ_TN // LANES
    all_rope = (j < (A_V0 // tiles)) | (j == B_Q0 // tiles)
    first_rope = j == B_K0 // tiles
    assert A_V0 % tiles == 0 and B_Q0 % tiles == 0 and B_K0 % tiles == 0 and B_K0 - B_Q0 == tiles

    def project():
        return lax.dot_general(x_ref[...], w_ref[...].astype(BF16), NT_DIMS, preferred_element_type=F32)

    def store(acc, n_rope):
        lane = lax.broadcasted_iota(jnp.int32, (SEQ, LANES), 1)
        first = (lane % HEAD_DIM) < HEAD_DIM // 2
        for c in range(n_rope):
            cols = slice(c * LANES, (c + 1) * LANES)
            o_ref[:, cols] = _rope_lanes(acc[:, cols], cos_ref[...], sin_ref[...], HEAD_DIM // 2, first)
        if n_rope < tiles:
            o_ref[:, n_rope * LANES:] = acc[:, n_rope * LANES:]

    @pl.when(all_rope)
    def _():
        store(project(), tiles)

    @pl.when(first_rope)
    def _():
        store(project(), 1)

    @pl.when(jnp.logical_not(all_rope | first_rope))
    def _():
        o_ref[...] = project()


def _in_proj(n, w_in_t, layer, cos, sin):
    T, D = n.shape
    nb = T // SEQ
    return pl.pallas_call(
        _proj_kernel,
        out_shape=jax.ShapeDtypeStruct((T, QKV_WIDTH), F32),
        grid=(nb, QKV_WIDTH // PROJ_TN),
        in_specs=[
            pl.BlockSpec((SEQ, D), lambda b, j: (b, 0)),
            pl.BlockSpec((None, PROJ_TN, D), lambda b, j: (layer, j, 0)),
            pl.BlockSpec((SEQ, LANES), lambda b, j: (0, 0)),
            pl.BlockSpec((SEQ, LANES), lambda b, j: (0, 0)),
        ],
        out_specs=pl.BlockSpec((SEQ, PROJ_TN), lambda b, j: (b, j)),
        compiler_params=_cparams(2),
        name="in_proj_rope",
    )(n, w_in_t, cos, sin)


BAND_UNROLL = 16


def _band_scores(q, kcat, vcat, head, has_prev, strict_prev):
    half = _lane_half()
    q_scale = HEAD_DIM ** -0.5 * LOG2_E
    qh = jnp.where(half if head == 0 else jnp.logical_not(half), q * q_scale, 0.0).astype(BF16)
    s = lax.dot_general(qh, kcat, NT_DIMS, preferred_element_type=F32)
    row = lax.broadcasted_iota(jnp.int32, (BLOCK, 2 * BLOCK), 0)
    col = lax.broadcasted_iota(jnp.int32, (BLOCK, 2 * BLOCK), 1)
    dist = row + BLOCK - col
    max_dist = BLOCK - 1 if strict_prev else BLOCK
    first_col = jnp.where(has_prev, 0, BLOCK)
    valid = (dist >= 0) & (dist <= max_dist) & (col >= first_col)
    s = jnp.where(valid, s, NEG)
    m = jnp.max(s, axis=1, keepdims=True)
    p = jnp.exp2(s - m)
    l = jnp.sum(p, axis=1, keepdims=True)
    acc = jnp.dot(p.astype(BF16), vcat, preferred_element_type=F32)
    return acc, m, l


def _pair(a0, a1):
    return jnp.where(_lane_half(), a0, a1)


def _dilated_kernel(*refs):
    q_refs, k_refs, v_refs = refs[0:3], refs[3:6], refs[6:9]
    o_ref = refs[9]
    acc_sc, m_sc, l_sc = refs[10:13]
    n_blocks = SEQ // BLOCK

    for g, (win, dil) in enumerate(DIL_PATTERNS):
        assert win // dil == BLOCK and SEQ % (dil * BLOCK) == 0
        q_ref, k_ref, v_ref = q_refs[g], k_refs[g], v_refs[g]

        def rows(start, dil=dil):
            return pl.ds(start, BLOCK) if dil == 1 else pl.ds(start, BLOCK, stride=dil)

        def block(it, g=g, dil=dil, q_ref=q_ref, k_ref=k_ref, v_ref=v_ref, rows=rows):
            r = it % dil
            i = it // dil
            base = i * (BLOCK * dil) + r
            prev = jnp.maximum(base - BLOCK * dil, 0)
            q = q_ref[rows(base), :]
            kcat = jnp.concatenate([k_ref[rows(prev), :], k_ref[rows(base), :]], axis=0).astype(BF16)
            vcat = jnp.concatenate([v_ref[rows(prev), :], v_ref[rows(base), :]], axis=0).astype(BF16)
            a0, m0, l0 = _band_scores(q, kcat, vcat, 0, i > 0, False)
            a1, m1, l1 = _band_scores(q, kcat, vcat, 1, i > 0, False)
            acc_sc[g, rows(base), :] = _pair(a0, a1)
            m_sc[g, rows(base), :] = _pair(m0, m1)
            l_sc[g, rows(base), :] = _pair(l0, l1)

        def body(it, carry, block=block):
            for u in range(BAND_UNROLL):
                block(it + u * (n_blocks // BAND_UNROLL))
            return carry

        lax.fori_loop(0, n_blocks // BAND_UNROLL, body, 0)

    def combine(i, carry):
        rows = pl.ds(pl.multiple_of(i * BLOCK, BLOCK), BLOCK)
        m = [m_sc[g, rows, :] for g in range(len(DIL_PATTERNS))]
        m_all = jnp.maximum(jnp.maximum(m[0], m[1]), m[2])
        num = jnp.zeros((BLOCK, LANES), F32)
        den = jnp.zeros((BLOCK, LANES), F32)
        for g in range(len(DIL_PATTERNS)):
            a = jnp.exp2(m[g] - m_all)
            num = num + a * acc_sc[g, rows, :]
            den = den + a * l_sc[g, rows, :]
        o_ref[rows, :] = (num / den).astype(o_ref.dtype)
        return carry

    lax.fori_loop(0, n_blocks, combine, 0)


def _dilated_attention(proj, nb):
    T = proj.shape[0]
    specs = []
    for col0 in (A_Q0, A_K0, A_V0):
        for g in range(len(DIL_PATTERNS)):
            specs.append(pl.BlockSpec((SEQ, LANES), lambda b, hp, c=col0 + 4 * g: (b, c + hp)))
    return pl.pallas_call(
        _dilated_kernel,
        out_shape=jax.ShapeDtypeStruct((T, BRANCH_WIDTH), BF16),
        grid=(nb, BRANCH_WIDTH // LANES),
        in_specs=specs,
        out_specs=pl.BlockSpec((SEQ, LANES), lambda b, hp: (b, hp)),
        scratch_shapes=[pltpu.VMEM((len(DIL_PATTERNS), SEQ, LANES), F32)] * 3,
        compiler_params=_cparams(2),
        name="dilated_attention",
    )(*([proj] * 9))


def _sink_kernel(q_ref, k_ref, v_ref, sink_ref, o_ref, k_sc, v_sc):
    hp = pl.program_id(1)
    kv_head = hp // 2
    k = k_ref[...]
    v = v_ref[...]
    k_sw = pltpu.roll(k, HEAD_DIM, 1)
    v_sw = pltpu.roll(v, HEAD_DIM, 1)
    first = kv_head == 0
    k_sc[0] = jnp.where(first, k, k_sw).astype(BF16)
    k_sc[1] = jnp.where(first, k_sw, k).astype(BF16)
    v_sc[0] = jnp.where(first, v, v_sw).astype(BF16)
    v_sc[1] = jnp.where(first, v_sw, v).astype(BF16)
    sink = _pair(sink_ref[pl.ds(2 * hp, 1), :], sink_ref[pl.ds(2 * hp + 1, 1), :])

    def block(i):
        base = pl.multiple_of(i * BLOCK, BLOCK)
        prev = pl.multiple_of(jnp.maximum(base - BLOCK, 0), BLOCK)
        q = q_ref[pl.ds(base, BLOCK), :]
        out = []
        for a in range(2):
            kcat = jnp.concatenate([k_sc[a, pl.ds(prev, BLOCK), :], k_sc[a, pl.ds(base, BLOCK), :]], axis=0)
            vcat = jnp.concatenate([v_sc[a, pl.ds(prev, BLOCK), :], v_sc[a, pl.ds(base, BLOCK), :]], axis=0)
            out.append(_band_scores(q, kcat, vcat, a, i > 0, True))
        acc = _pair(out[0][0], out[1][0])
        m = _pair(out[0][1], out[1][1])
        l = _pair(out[0][2], out[1][2])
        m_new = jnp.maximum(m, sink)
        scale = jnp.exp2(m - m_new)
        den = l * scale + jnp.exp2(sink - m_new)
        o_ref[pl.ds(base, BLOCK), :] = (acc * scale / den).astype(o_ref.dtype)

    n_blocks = SEQ // BLOCK

    def body(it, carry):
        for u in range(BAND_UNROLL):
            block(it + u * (n_blocks // BAND_UNROLL))
        return carry

    lax.fori_loop(0, n_blocks // BAND_UNROLL, body, 0)


def _sink_attention(proj, sinks_l, nb):
    T = proj.shape[0]
    sink_tab = jnp.broadcast_to(sinks_l.astype(F32)[:, None] * LOG2_E, (8, LANES))
    return pl.pallas_call(
        _sink_kernel,
        out_shape=jax.ShapeDtypeStruct((T, BRANCH_WIDTH), BF16),
        grid=(nb, BRANCH_WIDTH // LANES),
        in_specs=[
            pl.BlockSpec((SEQ, LANES), lambda b, hp: (b, B_Q0 + hp)),
            pl.BlockSpec((SEQ, LANES), lambda b, hp: (b, B_K0)),
            pl.BlockSpec((SEQ, LANES), lambda b, hp: (b, B_V0)),
            pl.BlockSpec((8, LANES), lambda b, hp: (0, 0)),
        ],
        out_specs=pl.BlockSpec((SEQ, LANES), lambda b, hp: (b, hp)),
        scratch_shapes=[pltpu.VMEM((2, SEQ, LANES), BF16)] * 2,
        compiler_params=_cparams(2),
        name="sink_attention",
    )(proj, proj, proj, sink_tab)


STICK_TQ = 256
STICK_SUBS = STICK_TQ // BLOCK
STICK_STAGES = 5


def _stick_tiles():
    n_key_blocks, n_q_tiles = SEQ // BLOCK, SEQ // STICK_TQ
    tiles = []
    for j in reversed(range(n_key_blocks)):
        first = j // STICK_SUBS
        tiles.append((first, j, 1 + j % STICK_SUBS))
        tiles += [(qi, j, 0) for qi in range(first + 1, n_q_tiles)]
    idle = (0, n_key_blocks - 1, STICK_SUBS + 1)
    lag = STICK_STAGES - 1
    stages = [[idle] * s + tiles + [idle] * (lag - s) for s in range(STICK_STAGES)]
    table = np.asarray(stages, np.int32)
    return table.shape[1], [np.ascontiguousarray(table[:, :, c]).reshape(-1) for c in range(3)]


def _stick_masks():
    row = np.arange(STICK_TQ)[:, None]
    col = np.arange(BLOCK)[None, :]
    valid = [np.ones((STICK_TQ, BLOCK), bool)]
    valid += [col + r * BLOCK < row for r in range(STICK_SUBS)]
    valid += [np.zeros((STICK_TQ, BLOCK), bool)]
    valid = np.stack(valid)
    masks = np.stack([valid.astype(np.float32), np.where(valid, 0.0, NEG).astype(np.float32)], axis=1)
    return masks.reshape(-1, STICK_TQ, BLOCK)


def _stick_kernel(n_slots, tq_ref, tk_ref, tm_ref, q_ref, k_ref, v_ref, tri_ref, mask_ref, o_ref,
                  q_sc, kd_sc, vd_sc, carry_sc, acc_sc, z_buf, hl_buf, sums_buf, w_buf):
    half = _lane_half()
    q_sc[...] = (q_ref[...] * (HEAD_DIM ** -0.5 * LOG2_E)).astype(BF16)
    n_key_blocks = SEQ // BLOCK
    k = k_ref[...]
    v = v_ref[...]
    for j in range(n_key_blocks):
        rows = slice(j * BLOCK, (j + 1) * BLOCK)
        kd_sc[j, :BLOCK] = jnp.where(half, k[rows], 0.0).astype(BF16)
        kd_sc[j, BLOCK:] = jnp.where(half, 0.0, k[rows]).astype(BF16)
        vd_sc[j, :BLOCK] = jnp.where(half, v[rows], 0.0).astype(BF16)
        vd_sc[j, BLOCK:] = jnp.where(half, 0.0, v[rows]).astype(BF16)
    carry_sc[...] = jnp.zeros_like(carry_sc)
    acc_sc[...] = jnp.zeros_like(acc_sc)
    z_buf[...] = jnp.zeros_like(z_buf)
    hl_buf[...] = jnp.zeros_like(hl_buf)
    sums_buf[...] = jnp.zeros_like(sums_buf)
    w_buf[...] = jnp.zeros_like(w_buf)
    tri = tri_ref[...]

    def q_rows(qi):
        return pl.ds(pl.multiple_of(qi * STICK_TQ, STICK_TQ), STICK_TQ)

    def trip(t, phase):
        cur, other = phase % 2, (phase + 1) % 2
        i = 4 * n_slots + t
        acc_sc[q_rows(tq_ref[i]), :] += jnp.dot(w_buf[cur], vd_sc[tk_ref[i]], preferred_element_type=F32)
        for a in range(2):
            sums_buf[2 * cur + a] = jnp.dot(hl_buf[2 * cur + a], tri, preferred_element_type=F32)
        z_buf[phase] = lax.dot_general(q_sc[q_rows(tq_ref[t]), :], kd_sc[tk_ref[t]], NT_DIMS,
                                       preferred_element_type=F32)
        i = 3 * n_slots + t
        rows = q_rows(tq_ref[i])
        bias = mask_ref[2 * tm_ref[i] + 1]
        z = z_buf[(phase + 1) % 4]
        w = []
        for a in range(2):
            sums = sums_buf[2 * other + a]
            carry = carry_sc[a, rows, :]
            w.append(jnp.exp2(z[:, a * BLOCK:(a + 1) * BLOCK] + carry + sums[:, :BLOCK] + bias))
            carry_sc[a, rows, :] = carry + sums[:, BLOCK:]
        w_buf[other] = jnp.concatenate(w, axis=1).astype(BF16)
        keep = mask_ref[2 * tm_ref[n_slots + t]]
        z = z_buf[(phase + 3) % 4]
        for a in range(2):
            za = z[:, a * BLOCK:(a + 1) * BLOCK]
            sp = (jnp.maximum(za, 0.0) + jnp.log2(1.0 + jnp.exp2(-jnp.abs(za)))) * keep
            hi = sp.astype(BF16)
            lo = (sp - hi.astype(F32)).astype(BF16)
            hl_buf[2 * other + a] = jnp.concatenate([hi, lo], axis=1)

    def body(i, c):
        for phase in range(4):
            trip(4 * i + phase, phase)
        return c

    assert n_slots % 4 == 0
    lax.fori_loop(0, n_slots // 4, body, 0)
    o_ref[...] = acc_sc[...].astype(o_ref.dtype)


def _stick_attention(proj, nb):
    T = proj.shape[0]
    jp = np.arange(BLOCK)[:, None]
    s = np.arange(BLOCK)[None, :]
    tri = -np.concatenate([(jp >= s), np.ones((BLOCK, BLOCK), bool)], axis=1).astype(np.float32)
    tri = jnp.asarray(np.concatenate([tri, tri], axis=0), BF16)
    n_slots, tables = _stick_tiles()
    masks = _stick_masks()
    n_key_blocks = SEQ // BLOCK
    return pl.pallas_call(
        functools.partial(_stick_kernel, n_slots),
        out_shape=jax.ShapeDtypeStruct((T, BRANCH_WIDTH), BF16),
        grid_spec=pltpu.PrefetchScalarGridSpec(
            num_scalar_prefetch=3,
            grid=(nb, BRANCH_WIDTH // LANES),
            in_specs=[
                pl.BlockSpec((SEQ, LANES), lambda b, hp, *_: (b, C_Q0 + hp)),
                pl.BlockSpec((SEQ, LANES), lambda b, hp, *_: (b, C_K0 + hp)),
                pl.BlockSpec((SEQ, LANES), lambda b, hp, *_: (b, C_V0 + hp)),
                pl.BlockSpec((2 * BLOCK, 2 * BLOCK), lambda b, hp, *_: (0, 0)),
                pl.BlockSpec(masks.shape, lambda b, hp, *_: (0, 0, 0)),
            ],
            out_specs=pl.BlockSpec((SEQ, LANES), lambda b, hp, *_: (b, hp)),
            scratch_shapes=[
                pltpu.VMEM((SEQ, LANES), BF16),
                pltpu.VMEM((n_key_blocks, 2 * BLOCK, LANES), BF16),
                pltpu.VMEM((n_key_blocks, 2 * BLOCK, LANES), BF16),
                pltpu.VMEM((2, SEQ, LANES), F32),
                pltpu.VMEM((SEQ, LANES), F32),
                pltpu.VMEM((4, STICK_TQ, 2 * BLOCK), F32),
                pltpu.VMEM((4, STICK_TQ, 2 * BLOCK), BF16),
                pltpu.VMEM((4, STICK_TQ, 2 * BLOCK), F32),
                pltpu.VMEM((2, STICK_TQ, 2 * BLOCK), BF16),
            ],
        ),
        compiler_params=_cparams(2),
        name="stick_breaking_attention",
    )(*[jnp.asarray(t) for t in tables], proj, proj, proj, tri, jnp.asarray(masks))


MLA_HEADS = 8
MLA_TM = 512


def _mla_prep_kernel(c_ref, gq_ref, gkv_ref, wq_ref, wk_ref, wv_ref, cos_ref, sin_ref, q_ref, k_ref, v_ref):
    c = c_ref[...]
    cq = _rms(c[:, :Q_LORA], gq_ref[...]).astype(BF16)
    ckv = _rms(c[:, Q_LORA:Q_LORA + KV_LORA], gkv_ref[...]).astype(BF16)
    q = jnp.dot(cq, wq_ref[...].astype(BF16), preferred_element_type=F32)
    k = jnp.dot(ckv, wk_ref[...].astype(BF16), preferred_element_type=F32)
    v_ref[...] = jnp.dot(ckv, wv_ref[...].astype(BF16), preferred_element_type=F32).astype(BF16)
    cos, sin = cos_ref[...], sin_ref[...]
    lane = lax.broadcasted_iota(jnp.int32, (MLA_TM, LANES), 1)
    rope_lane = (lane >= D_NOPE) & (lane < D_NOPE + D_ROPE)
    first = lane < D_NOPE + D_ROPE // 2
    kr = pltpu.roll(c[:, Q_LORA + KV_LORA:], D_NOPE, 1)
    kr = jnp.where(rope_lane, _rope_lanes(kr, cos, sin, D_ROPE // 2, first), 0.0)
    scale = (D_NOPE + D_ROPE) ** -0.5 * LOG2_E
    for h in range(MLA_HEADS):
        sl = slice(h * LANES, (h + 1) * LANES)
        q_ref[:, sl] = (_rope_lanes(q[:, sl], cos, sin, D_ROPE // 2, first) * scale).astype(BF16)
        k_ref[:, sl] = (k[:, sl] + kr).astype(BF16)


def _mla_prep(proj, g_qa, g_kva, w_uq, w_ukv, cos, sin):
    T = proj.shape[0]
    wq = jnp.pad(w_uq.reshape(Q_LORA, MLA_HEADS, D_NOPE + D_ROPE), ((0, 0), (0, 0), (0, LANES - D_NOPE - D_ROPE)))
    wq = wq.reshape(Q_LORA, MLA_HEADS * LANES)
    wkv = w_ukv.reshape(KV_LORA, MLA_HEADS, 2 * HEAD_DIM)
    wk = jnp.pad(wkv[:, :, :D_NOPE], ((0, 0), (0, 0), (0, LANES - D_NOPE))).reshape(KV_LORA, MLA_HEADS * LANES)
    wv = wkv[:, :, D_NOPE:].reshape(KV_LORA, MLA_HEADS * HEAD_DIM)
    full = lambda shape: pl.BlockSpec(shape, lambda i: (0, 0))
    per_seq = SEQ // MLA_TM
    return pl.pallas_call(
        _mla_prep_kernel,
        out_shape=[jax.ShapeDtypeStruct((T, MLA_HEADS * LANES), BF16),
                   jax.ShapeDtypeStruct((T, MLA_HEADS * LANES), BF16),
                   jax.ShapeDtypeStruct((T, MLA_HEADS * HEAD_DIM), BF16)],
        grid=(T // MLA_TM,),
        in_specs=[
            pl.BlockSpec((MLA_TM, MLA_BLOCK_W), lambda i: (i, MLA_COL0 // MLA_BLOCK_W)),
            full((1, Q_LORA)), full((1, KV_LORA)),
            full(wq.shape), full(wk.shape), full(wv.shape),
            pl.BlockSpec((MLA_TM, LANES), lambda i: (i % per_seq, 0)),
            pl.BlockSpec((MLA_TM, LANES), lambda i: (i % per_seq, 0)),
        ],
        out_specs=[pl.BlockSpec((MLA_TM, MLA_HEADS * LANES), lambda i: (i, 0)),
                   pl.BlockSpec((MLA_TM, MLA_HEADS * LANES), lambda i: (i, 0)),
                   pl.BlockSpec((MLA_TM, MLA_HEADS * HEAD_DIM), lambda i: (i, 0))],
        compiler_params=_cparams(1),
        name="mla_prep",
    )(proj, g_qa.reshape(1, Q_LORA), g_kva.reshape(1, KV_LORA), wq, wk, wv, cos, sin)


MLA_TQ = 1024
MLA_TK = 512


def _mla_attn_kernel(q_ref, k_ref, v_ref, o_ref, v_sc, m_sc, acc_sc):
    half = lax.broadcasted_iota(jnp.int32, (SEQ, LANES), 1) < HEAD_DIM
    v = v_ref[...]
    one = jnp.ones_like(v)
    v_sc[0] = jnp.where(half, v, one)
    v_sc[1] = jnp.where(half, one, v)
    lane_half = lax.broadcasted_iota(jnp.int32, (MLA_TQ, LANES), 1) < HEAD_DIM

    def step(qb, kb, diagonal):
        for a in range(2):
            q = q_ref[pl.ds(qb, MLA_TQ), a * LANES:(a + 1) * LANES]
            k = k_ref[pl.ds(kb, MLA_TK), a * LANES:(a + 1) * LANES]
            s = lax.dot_general(q, k, (((1,), (1,)), ((), ())), preferred_element_type=F32)
            if diagonal:
                row = lax.broadcasted_iota(jnp.int32, (MLA_TQ, MLA_TK), 0)
                col = lax.broadcasted_iota(jnp.int32, (MLA_TQ, MLA_TK), 1)
                s = jnp.where(col + (kb - qb) <= row, s, NEG)
            m = m_sc[a]
            m_new = jnp.maximum(m, jnp.max(s, axis=1, keepdims=True))
            alpha = jnp.exp2(m - m_new)
            p = jnp.exp2(s - jnp.concatenate([m_new] * (MLA_TK // LANES), axis=1))
            pv = jnp.dot(p.astype(BF16), v_sc[a, pl.ds(kb, MLA_TK), :], preferred_element_type=F32)
            acc_sc[a] = alpha * acc_sc[a] + pv
            m_sc[a] = m_new

    def q_block(qi, c):
        qb = pl.multiple_of(qi * MLA_TQ, MLA_TQ)
        m_sc[...] = jnp.full(m_sc.shape, NEG, F32)
        acc_sc[...] = jnp.zeros(acc_sc.shape, F32)

        def k_block(kj, c2):
            step(qb, pl.multiple_of(kj * MLA_TK, MLA_TK), False)
            return c2

        lax.fori_loop(0, qi * (MLA_TQ // MLA_TK), k_block, 0)
        for d in range(MLA_TQ // MLA_TK):
            step(qb, qb + d * MLA_TK, True)
        outs = [acc_sc[a] / pltpu.roll(acc_sc[a], HEAD_DIM, 1) for a in range(2)]
        o_ref[pl.ds(qb, MLA_TQ), :] = jnp.where(lane_half, outs[0], outs[1]).astype(o_ref.dtype)
        return c

    lax.fori_loop(0, SEQ // MLA_TQ, q_block, 0)


def _mla_attention(q, k, v, nb):
    T = q.shape[0]
    return pl.pallas_call(
        _mla_attn_kernel,
        out_shape=jax.ShapeDtypeStruct((T, BRANCH_WIDTH), BF16),
        grid=(nb, BRANCH_WIDTH // LANES),
        in_specs=[
            pl.BlockSpec((SEQ, 2 * LANES), lambda b, hp: (b, hp)),
            pl.BlockSpec((SEQ, 2 * LANES), lambda b, hp: (b, hp)),
            pl.BlockSpec((SEQ, LANES), lambda b, hp: (b, hp)),
        ],
        out_specs=pl.BlockSpec((SEQ, LANES), lambda b, hp: (b, hp)),
        scratch_shapes=[pltpu.VMEM((2, SEQ, LANES), BF16),
                        pltpu.VMEM((2, MLA_TQ, LANES), F32),
                        pltpu.VMEM((2, MLA_TQ, LANES), F32)],
        compiler_params=_cparams(2),
        name="mla_attention",
    )(q, k, v)


MERGE_TM = 1024
MERGE_TN = 256


def _merge_kernel(*refs):
    n_ref = refs[0]
    o_refs = refs[1:5]
    wg_refs = refs[5:9]
    wb_refs = refs[9:13]
    out_ref = refs[13]
    n = n_ref[...]
    acc = jnp.zeros((MERGE_TM, MERGE_TN), F32)
    for i in range(N_BRANCH):
        gate = lax.dot_general(n, wg_refs[i][...], NT_DIMS, preferred_element_type=F32)
        br = jnp.dot(o_refs[i][...], wb_refs[i][...].astype(BF16), preferred_element_type=F32)
        acc = acc + br * jax.nn.sigmoid(gate)
    out_ref[...] = acc.astype(out_ref.dtype)


GATE_CAST_ROWS = 256


def _gate_cast_kernel(layer, w_hbm, o_ref, buf, sem):
    j = pl.program_id(0)

    def copy(step, slot):
        rows = pl.ds(pl.multiple_of(GATE_COL0 + step * GATE_CAST_ROWS, 8), GATE_CAST_ROWS)
        return pltpu.make_async_copy(w_hbm.at[layer, rows, :], buf.at[slot], sem.at[slot])

    @pl.when(j == 0)
    def _():
        copy(0, 0).start()

    @pl.when(j + 1 < pl.num_programs(0))
    def _():
        copy(j + 1, (j + 1) % 2).start()

    copy(j, j % 2).wait()
    o_ref[...] = buf[j % 2].astype(o_ref.dtype)


def _gate_weights_t(w_in_t, layer):
    n_gate = N_BRANCH * D_MODEL
    D = w_in_t.shape[2]
    assert GATE_COL0 + n_gate == w_in_t.shape[1] and GATE_COL0 % 8 == 0
    return pl.pallas_call(
        functools.partial(_gate_cast_kernel, layer),
        out_shape=jax.ShapeDtypeStruct((n_gate, D), BF16),
        grid=(n_gate // GATE_CAST_ROWS,),
        in_specs=[pl.BlockSpec(memory_space=pl.ANY)],
        out_specs=pl.BlockSpec((GATE_CAST_ROWS, D), lambda j: (j, 0)),
        scratch_shapes=[pltpu.VMEM((2, GATE_CAST_ROWS, D), F32), pltpu.SemaphoreType.DMA((2,))],
        compiler_params=_cparams(1),
        name="gate_weight_cast",
    )(w_in_t)


def _merge(n, branches, w_gates_t, w_branch, layer):
    T, D = n.shape
    nblk = D // MERGE_TN
    in_specs = [pl.BlockSpec((MERGE_TM, D), lambda m, j: (m, 0))]
    in_specs += [pl.BlockSpec((MERGE_TM, BRANCH_WIDTH), lambda m, j: (m, 0))] * N_BRANCH
    in_specs += [pl.BlockSpec((MERGE_TN, D), lambda m, j, i=i: (i * nblk + j, 0)) for i in range(N_BRANCH)]
    in_specs += [pl.BlockSpec((None, None, BRANCH_WIDTH, MERGE_TN), lambda m, j, i=i: (layer, i, 0, j))
                 for i in range(N_BRANCH)]
    return pl.pallas_call(
        _merge_kernel,
        out_shape=jax.ShapeDtypeStruct((T, D), BF16),
        grid=(T // MERGE_TM, nblk),
        in_specs=in_specs,
        out_specs=pl.BlockSpec((MERGE_TM, MERGE_TN), lambda m, j: (m, j)),
        compiler_params=_cparams(2),
        name="gated_merge",
    )(n, *branches, *([w_gates_t] * N_BRANCH), *([w_branch] * N_BRANCH))


def _matmul_res_kernel(a_ref, w_ref, r_ref, o_ref):
    o_ref[...] = r_ref[...] + jnp.dot(a_ref[...], w_ref[...].astype(BF16), preferred_element_type=F32)


def _matmul_res(a, w, layer, res, tm, tn):
    M, K = a.shape
    N = w.shape[2]
    return pl.pallas_call(
        _matmul_res_kernel,
        out_shape=jax.ShapeDtypeStruct((M, N), F32),
        grid=(M // tm, N // tn),
        in_specs=[pl.BlockSpec((tm, K), lambda m, j: (m, 0)),
                  pl.BlockSpec((None, K, tn), lambda m, j: (layer, 0, j)),
                  pl.BlockSpec((tm, tn), lambda m, j: (m, j))],
        out_specs=pl.BlockSpec((tm, tn), lambda m, j: (m, j)),
        compiler_params=_cparams(2),
        name="matmul_residual",
    )(a, w, res)


def _swiglu_up_kernel(x_ref, wg_ref, wu_ref, o_ref):
    x = x_ref[...]
    g = jnp.dot(x, wg_ref[...].astype(BF16), preferred_element_type=F32)
    u = jnp.dot(x, wu_ref[...].astype(BF16), preferred_element_type=F32)
    o_ref[...] = (jax.nn.silu(g) * u).astype(o_ref.dtype)


def _swiglu_up(x, wg, wu, layer, tm=1024, tn=512):
    M, K = x.shape
    N = wg.shape[2]
    return pl.pallas_call(
        _swiglu_up_kernel,
        out_shape=jax.ShapeDtypeStruct((M, N), BF16),
        grid=(M // tm, N // tn),
        in_specs=[pl.BlockSpec((tm, K), lambda m, j: (m, 0)),
                  pl.BlockSpec((None, K, tn), lambda m, j: (layer, 0, j)),
                  pl.BlockSpec((None, K, tn), lambda m, j: (layer, 0, j))],
        out_specs=pl.BlockSpec((tm, tn), lambda m, j: (m, j)),
        compiler_params=_cparams(2),
        name="swiglu_up",
    )(x, wg, wu)


ROUTER_TM = 512


def _router_kernel(h_ref, g_ref, wr_ref, idx_ref, w_ref):
    n = _rms(h_ref[...], g_ref[...])
    wr = wr_ref[...]
    n_hi = n.astype(BF16)
    n_lo = (n - n_hi.astype(F32)).astype(BF16)
    w_hi = wr.astype(BF16)
    w_lo = (wr - w_hi.astype(F32)).astype(BF16)
    logits = (jnp.dot(n_hi, w_hi, preferred_element_type=F32) + jnp.dot(n_lo, w_hi, preferred_element_type=F32)
              + jnp.dot(n_hi, w_lo, preferred_element_type=F32))
    lane = lax.broadcasted_iota(jnp.int32, logits.shape, 1)
    v1 = jnp.max(logits, axis=1, keepdims=True)
    i1 = jnp.min(jnp.where(logits == v1, lane, N_EXPERTS), axis=1, keepdims=True)
    rest = jnp.where(lane == i1, -jnp.inf, logits)
    v2 = jnp.max(rest, axis=1, keepdims=True)
    i2 = jnp.min(jnp.where(rest == v2, lane, N_EXPERTS), axis=1, keepdims=True)
    e2 = jnp.exp(v2 - v1)
    den = 1.0 + e2
    two = lax.broadcasted_iota(jnp.int32, (ROUTER_TM, TOP_K), 1)
    idx_ref[...] = jnp.where(two == 0, i1, i2)
    w_ref[...] = jnp.where(two == 0, 1.0 / den, e2 / den)


def _router(h, g, w_router):
    T, D = h.shape
    return pl.pallas_call(
        _router_kernel,
        out_shape=[jax.ShapeDtypeStruct((T, TOP_K), jnp.int32), jax.ShapeDtypeStruct((T, TOP_K), F32)],
        grid=(T // ROUTER_TM,),
        in_specs=[pl.BlockSpec((ROUTER_TM, D), lambda i: (i, 0)),
                  pl.BlockSpec((1, D), lambda i: (0, 0)),
                  pl.BlockSpec((D, N_EXPERTS), lambda i: (0, 0))],
        out_specs=[pl.BlockSpec((ROUTER_TM, TOP_K), lambda i: (i, 0)),
                   pl.BlockSpec((ROUTER_TM, TOP_K), lambda i: (i, 0))],
        compiler_params=_cparams(1),
        name="moe_router",
    )(h, g.reshape(1, D), w_router)


def _row_copy(src_hbm, row, dst_vmem, slot, sem):
    return pltpu.make_async_copy(src_hbm.at[pl.ds(row, 1), :], dst_vmem.at[pl.ds(slot, 1), :], sem)


def _expert_kernel(n_f, plan_ref, item_e_ref, item_row_ref, item_sub_ref, src_ref,
                   h_hbm, g_ref, wg_ref, wu_ref, wd_ref, y_hbm, gbuf, x_sc, acc_sc, sem_in, sem_out):
    it = pl.program_id(0)
    f = pl.program_id(1)
    n_it = pl.num_programs(0)
    n_items = plan_ref[0]
    live = it < n_items
    row0 = pl.multiple_of(item_row_ref[it], MOE_SUB)
    n_sub = item_sub_ref[it]
    item_rows = x_sc.shape[0]
    g_rows = gbuf.shape[0]
    chunk = g_rows // n_f

    def gather_wait():
        pltpu.make_async_copy(h_hbm.at[pl.ds(0, g_rows), :], gbuf, sem_in).wait()

    def y_copy(s):
        rows = pl.ds(s * MOE_SUB, MOE_SUB)
        return pltpu.make_async_copy(acc_sc.at[rows, :], y_hbm.at[pl.ds(row0 + s * MOE_SUB, MOE_SUB), :], sem_out)

    @pl.when(live & (it == 0) & (f == 0))
    def _():
        def start(r, c):
            _row_copy(h_hbm, src_ref[row0 + r], gbuf, r, sem_in).start()
            return c
        lax.fori_loop(0, g_rows, start, 0)

    @pl.when(live & (f == 0))
    def _():
        gather_wait()

        def norm(i, c):
            rows = pl.ds(pl.multiple_of(i * MOE_SUB, MOE_SUB), MOE_SUB)
            x_sc[rows, :] = _rms(gbuf[rows, :], g_ref[...]).astype(BF16)
            return c
        lax.fori_loop(0, item_rows // MOE_SUB, norm, 0)
        acc_sc[...] = jnp.zeros(acc_sc.shape, F32)

    @pl.when((it == n_items) & (it > 0) & (f == 0))
    def _():
        gather_wait()

    next_row0 = item_row_ref[jnp.minimum(it + 1, n_it - 1)]

    for n in range(1, MOE_ITEM_SUBS + 1):
        @pl.when(live & (n_sub == n))
        def _(n=n):
            lo = f * chunk
            for r in range(chunk):
                _row_copy(h_hbm, src_ref[next_row0 + lo + r], gbuf, lo + r, sem_in).start()
            rows = n * MOE_SUB
            x = x_sc[:rows, :]
            g = jnp.dot(x, wg_ref[...].astype(BF16), preferred_element_type=F32)
            u = jnp.dot(x, wu_ref[...].astype(BF16), preferred_element_type=F32)
            mid = (jax.nn.silu(g) * u).astype(BF16)
            acc_sc[:rows, :] += jnp.dot(mid, wd_ref[...].astype(BF16), preferred_element_type=F32)

    @pl.when(live & (f == n_f - 1))
    def _():
        for s in range(MOE_ITEM_SUBS):
            @pl.when(s < n_sub)
            def _(s=s):
                y_copy(s).start()
        for s in range(MOE_ITEM_SUBS):
            @pl.when(s < n_sub)
            def _(s=s):
                y_copy(s).wait()

    @pl.when((it == n_it - 1) & (f == n_f - 1))
    def _():
        @pl.when(live)
        def _():
            gather_wait()

        acc_sc[:MOE_SUB, :] = jnp.zeros((MOE_SUB, acc_sc.shape[1]), F32)
        first = plan_ref[1] // MOE_SUB
        n_blocks = y_hbm.shape[0] // MOE_SUB

        def fill(s):
            dst = y_hbm.at[pl.ds(pl.multiple_of(s * MOE_SUB, MOE_SUB), MOE_SUB), :]
            return pltpu.make_async_copy(acc_sc.at[pl.ds(0, MOE_SUB), :], dst, sem_out)

        lax.fori_loop(first, n_blocks, lambda s, c: (fill(s).start(), c)[1], 0)
        lax.fori_loop(first, n_blocks, lambda s, c: (fill(s).wait(), c)[1], 0)


def _gather_rows_per_item(n_f):
    item_rows = MOE_SUB * MOE_ITEM_SUBS
    chunk = -(-item_rows // n_f)
    chunk += -chunk % 8
    return chunk * n_f


def _expert_ffn(h, g, w_gate, w_up, w_down, plan, item_e, item_row, item_sub, src_rows, max_rows, max_items):
    D = h.shape[1]
    n_f = w_gate.shape[3] // MOE_TF
    item_rows = MOE_SUB * MOE_ITEM_SUBS

    def f_of(it, f, plan_ref):
        return jnp.where(it < plan_ref[0], f, n_f - 1)

    return pl.pallas_call(
        functools.partial(_expert_kernel, n_f),
        out_shape=jax.ShapeDtypeStruct((max_rows, D), F32),
        grid_spec=pltpu.PrefetchScalarGridSpec(
            num_scalar_prefetch=5,
            grid=(max_items, n_f),
            in_specs=[
                pl.BlockSpec(memory_space=pl.ANY),
                pl.BlockSpec((1, D), lambda it, f, n, e, r, s, src: (0, 0)),
                pl.BlockSpec((None, None, D, MOE_TF), lambda it, f, n, e, r, s, src: (0, e[it], 0, f_of(it, f, n))),
                pl.BlockSpec((None, None, D, MOE_TF), lambda it, f, n, e, r, s, src: (0, e[it], 0, f_of(it, f, n))),
                pl.BlockSpec((None, None, MOE_TF, D), lambda it, f, n, e, r, s, src: (0, e[it], f_of(it, f, n), 0)),
            ],
            out_specs=pl.BlockSpec(memory_space=pl.ANY),
            scratch_shapes=[pltpu.VMEM((_gather_rows_per_item(n_f), D), F32),
                            pltpu.VMEM((item_rows, D), BF16),
                            pltpu.VMEM((item_rows, D), F32),
                            pltpu.SemaphoreType.DMA(()), pltpu.SemaphoreType.DMA(())],
        ),
        compiler_params=_cparams(2),
        name="moe_expert_ffn",
    )(plan, item_e, item_row, item_sub, src_rows, h, g.reshape(1, D), w_gate, w_up, w_down)


COMBINE_TM = 256


def _combine_kernel(pos_ref, h_ref, w_ref, g_ref, y_hbm, o_ref, buf, sem):
    base = pl.program_id(0) * COMBINE_TM

    def start(r, c):
        for k in range(TOP_K):
            _row_copy(y_hbm, pos_ref[(base + r) * TOP_K + k], buf.at[k], r, sem).start()
        return c

    lax.fori_loop(0, COMBINE_TM, start, 0, unroll=8)
    for k in range(TOP_K):
        pltpu.make_async_copy(y_hbm.at[pl.ds(0, COMBINE_TM), :], buf.at[k], sem).wait()
    w = w_ref[...]
    out = h_ref[...] + w[:, 0:1] * buf[0] + w[:, 1:2] * buf[1]
    o_ref[...] = _rms(out, g_ref[...])


def _combine_norm(h, ys, pos, top_w, g):
    T, D = h.shape
    return pl.pallas_call(
        _combine_kernel,
        out_shape=jax.ShapeDtypeStruct((T, D), F32),
        grid_spec=pltpu.PrefetchScalarGridSpec(
            num_scalar_prefetch=1,
            grid=(T // COMBINE_TM,),
            in_specs=[pl.BlockSpec((COMBINE_TM, D), lambda i, p: (i, 0)),
                      pl.BlockSpec((COMBINE_TM, TOP_K), lambda i, p: (i, 0)),
                      pl.BlockSpec((1, D), lambda i, p: (0, 0)),
                      pl.BlockSpec(memory_space=pl.ANY)],
            out_specs=pl.BlockSpec((COMBINE_TM, D), lambda i, p: (i, 0)),
            scratch_shapes=[pltpu.VMEM((TOP_K, COMBINE_TM, D), F32), pltpu.SemaphoreType.DMA(())],
        ),
        compiler_params=_cparams(1),
        name="moe_combine_norm",
    )(pos.reshape(-1), h, top_w, g.reshape(1, D), ys)


def _moe_plan(top_idx, gather_rows):
    T = top_idx.shape[0]
    n_assign = T * TOP_K
    item_rows = MOE_SUB * MOE_ITEM_SUBS
    max_rows = n_assign + N_EXPERTS * MOE_SUB
    max_items = n_assign // item_rows + N_EXPERTS
    e_flat = top_idx.reshape(-1)
    onehot = (e_flat[:, None] == jnp.arange(N_EXPERTS)[None, :]).astype(jnp.int32)
    ranks = jnp.cumsum(onehot, axis=0) - onehot
    counts = jnp.sum(onehot, axis=0)
    padded = ((counts + MOE_SUB - 1) // MOE_SUB) * MOE_SUB
    group_start = jnp.cumsum(padded) - padded
    rank = jnp.sum(ranks * onehot, axis=1)
    pos = group_start[e_flat] + rank
    src_rows = jnp.zeros((max_rows + gather_rows,), jnp.int32).at[pos].set(
        jnp.arange(n_assign, dtype=jnp.int32) // TOP_K, unique_indices=True)
    items_per_e = (padded + item_rows - 1) // item_rows
    item_start = jnp.cumsum(items_per_e) - items_per_e
    n_items = jnp.sum(items_per_e)
    it = jnp.arange(max_items)
    item_e = jnp.minimum(jnp.sum(it[:, None] >= (item_start + items_per_e)[None, :], axis=1), N_EXPERTS - 1)
    local = it - item_start[item_e]
    item_row = group_start[item_e] + local * item_rows
    item_sub = jnp.clip((padded[item_e] - local * item_rows) // MOE_SUB, 0, MOE_ITEM_SUBS)
    live = it < n_items
    last = jnp.maximum(n_items - 1, 0)
    item_e = jnp.where(live, item_e, item_e[last])
    item_row = jnp.where(live, item_row, 0)
    item_sub = jnp.where(live, item_sub, 0)
    i32 = lambda a: a.astype(jnp.int32)
    plan = jnp.stack([n_items, jnp.sum(padded)])
    return (i32(pos.reshape(T, TOP_K)), src_rows, i32(plan), i32(item_e), i32(item_row),
            i32(item_sub), max_rows, max_items)


def _moe_and_final_norm(h, g_ffn, w_router, w_gate, w_up, w_down, g_final):
    top_idx, top_w = _router(h, g_ffn, w_router)
    gather_rows = _gather_rows_per_item(w_gate.shape[3] // MOE_TF)
    pos, src_rows, plan, item_e, item_row, item_sub, max_rows, max_items = _moe_plan(top_idx, gather_rows)
    ys = _expert_ffn(h, g_ffn, w_gate, w_up, w_down, plan, item_e, item_row, item_sub, src_rows, max_rows,
                     max_items)
    return _combine_norm(h, ys, pos, top_w, g_final)


def kernel(x, w_in, w_branch, w_out, norm_mix, norm_ffn, norm_final, sinks, mla_q_norm, mla_kv_norm, mla_w_uq,
           mla_w_ukv, ffn_w_gate, ffn_w_up, ffn_w_down, router_w, moe_w_gate, moe_w_up, moe_w_down):
    nb, seq, d = x.shape
    assert (seq, d) == (SEQ, D_MODEL)
    depth = w_in.shape[0]
    assert depth == 2, "layer 0 uses the dense FFN, layer 1 the MoE followed by the final norm"
    cos64, sin64 = _rope_tables(HEAD_DIM, 0, HEAD_DIM)
    cos32, sin32 = _rope_tables(D_ROPE, D_NOPE, LANES)
    h = x.reshape(nb * seq, d)
    w_in_t = jnp.swapaxes(w_in, 1, 2)
    out = None
    for layer in range(depth):
        n = _rmsnorm(h, norm_mix[layer], BF16)
        proj = _in_proj(n, w_in_t, layer, cos64, sin64)
        o_a = _dilated_attention(proj, nb)
        o_b = _sink_attention(proj, sinks[layer], nb)
        o_c = _stick_attention(proj, nb)
        q_d, k_d, v_d = _mla_prep(proj, mla_q_norm[layer], mla_kv_norm[layer], mla_w_uq[layer], mla_w_ukv[layer],
                                  cos32, sin32)
        o_d = _mla_attention(q_d, k_d, v_d, nb)
        w_gates_t = _gate_weights_t(w_in_t, layer)
        merged = _merge(n, (o_a, o_b, o_c, o_d), w_gates_t, w_branch, layer)
        h = _matmul_res(merged, w_out, layer, h, 2048, 512)
        if layer == 0:
            n2 = _rmsnorm(h, norm_ffn[layer], BF16)
            mid = _swiglu_up(n2, ffn_w_gate, ffn_w_up, 0)
            h = _matmul_res(mid, ffn_w_down, 0, h, 1024, 256)
        else:
            out = _moe_and_final_norm(h, norm_ffn[layer], router_w[0], moe_w_gate, moe_w_up, moe_w_down, norm_final)
    return out.reshape(nb, seq, d)
```

```python
import functools

import numpy as np
import jax
import jax.numpy as jnp
from jax import lax
from jax.experimental import pallas as pl
from jax.experimental.pallas import tpu as pltpu

F32 = jnp.float32
BF16 = jnp.bfloat16

D_MODEL = 2048
SEQ = 2048
HEAD_DIM = 64
BLOCK = 128
LANES = 128
ROPE_THETA = 10000.0
NORM_EPS = 1e-6
DIL_PATTERNS = ((128, 1), (512, 4), (2048, 16))
B_WINDOW = 128
D_NOPE = 64
D_ROPE = 32
Q_LORA = 384
KV_LORA = 256
N_BRANCH = 4
BRANCH_WIDTH = 512
N_EXPERTS = 8
TOP_K = 2

A_Q0, A_K0, A_V0 = 0, 12, 24
B_Q0, B_K0, B_V0 = 36, 40, 41
C_Q0, C_K0, C_V0 = 42, 46, 50
MLA_COL0 = 6912
MLA_BLOCK_W = 768
QKV_WIDTH = 7680
GATE_COL0 = 7584
PROJ_TN = 512

VMEM_LIMIT = 56 * 1024 * 1024
NEG = -1e30
NT_DIMS = (((1,), (1,)), ((), ()))
LOG2_E = 1.4426950408889634

MOE_SUB = 128
MOE_ITEM_SUBS = 8
MOE_TF = 256


def _cparams(n_axes):
    return pltpu.CompilerParams(dimension_semantics=("arbitrary",) * n_axes, vmem_limit_bytes=VMEM_LIMIT)


def _lane_half():
    return lax.broadcasted_iota(jnp.int32, (BLOCK, LANES), 1) < HEAD_DIM


def _rms(x, g):
    return x * lax.rsqrt(jnp.mean(x * x, axis=-1, keepdims=True) + NORM_EPS) * g


def _rmsnorm_kernel(x_ref, g_ref, o_ref):
    o_ref[...] = _rms(x_ref[...], g_ref[...]).astype(o_ref.dtype)


def _rmsnorm(x, g, out_dtype, tm=512):
    T, D = x.shape
    return pl.pallas_call(
        _rmsnorm_kernel,
        out_shape=jax.ShapeDtypeStruct((T, D), out_dtype),
        grid=(T // tm,),
        in_specs=[pl.BlockSpec((tm, D), lambda i: (i, 0)), pl.BlockSpec((1, D), lambda i: (0, 0))],
        out_specs=pl.BlockSpec((tm, D), lambda i: (i, 0)),
        compiler_params=_cparams(1),
        name="rmsnorm",
    )(x, g.reshape(1, D))


def _rope_tables(dh, lane0, period):
    half = dh // 2
    freqs = ROPE_THETA ** (-2.0 * jnp.arange(half, dtype=F32) / dh)
    ang = jnp.arange(SEQ, dtype=F32)[:, None] * freqs[None, :]
    cos, sin = jnp.cos(ang), jnp.sin(ang)
    cos_h = jnp.concatenate([cos, cos], axis=1)
    sin_h = jnp.concatenate([-sin, sin], axis=1)
    ones = jnp.ones((SEQ, period - dh), F32)
    cos_p = jnp.concatenate([cos_h, ones], axis=1)
    sin_p = jnp.concatenate([sin_h, 0.0 * ones], axis=1)
    reps = LANES // period
    cos_t = jnp.roll(jnp.tile(cos_p, (1, reps)), lane0, axis=1)
    sin_t = jnp.roll(jnp.tile(sin_p, (1, reps)), lane0, axis=1)
    return cos_t, sin_t


def _rope_lanes(x, cos, sin, half, first_half_mask):
    fwd = pltpu.roll(x, LANES - half, 1)
    bwd = pltpu.roll(x, half, 1)
    return x * cos + jnp.where(first_half_mask, fwd, bwd) * sin


def _proj_kernel(x_ref, w_ref, cos_ref, sin_ref, o_ref):
    j = pl.program_id(1)
    tiles = PROJ_TN // LANES
    all_rope = (j < (A_V0 // tiles)) | (j == B_Q0 // tiles)
    first_rope = j == B_K0 // tiles
    assert A_V0 % tiles == 0 and B_Q0 % tiles == 0 and B_K0 % tiles == 0 and B_K0 - B_Q0 == tiles

    def project():
        return lax.dot_general(x_ref[...], w_ref[...].astype(BF16), NT_DIMS, preferred_element_type=F32)

    def store(acc, n_rope):
        lane = lax.broadcasted_iota(jnp.int32, (SEQ, LANES), 1)
        first = (lane % HEAD_DIM) < HEAD_DIM // 2
        for c in range(n_rope):
            cols = slice(c * LANES, (c + 1) * LANES)
            o_ref[:, cols] = _rope_lanes(acc[:, cols], cos_ref[...], sin_ref[...], HEAD_DIM // 2, first)
        if n_rope < tiles:
            o_ref[:, n_rope * LANES:] = acc[:, n_rope * LANES:]

    @pl.when(all_rope)
    def _():
        store(project(), tiles)

    @pl.when(first_rope)
    def _():
        store(project(), 1)

    @pl.when(jnp.logical_not(all_rope | first_rope))
    def _():
        o_ref[...] = project()


def _in_proj(n, w_in_t, layer, cos, sin):
    T, D = n.shape
    nb = T // SEQ
    return pl.pallas_call(
        _proj_kernel,
        out_shape=jax.ShapeDtypeStruct((T, QKV_WIDTH), F32),
        grid=(nb, QKV_WIDTH // PROJ_TN),
        in_specs=[
            pl.BlockSpec((SEQ, D), lambda b, j: (b, 0)),
            pl.BlockSpec((None, PROJ_TN, D), lambda b, j: (layer, j, 0)),
            pl.BlockSpec((SEQ, LANES), lambda b, j: (0, 0)),
            pl.BlockSpec((SEQ, LANES), lambda b, j: (0, 0)),
        ],
        out_specs=pl.BlockSpec((SEQ, PROJ_TN), lambda b, j: (b, j)),
        compiler_params=_cparams(2),
        name="in_proj_rope",
    )(n, w_in_t, cos, sin)


BAND_UNROLL = 16


def _band_scores(q, kcat, vcat, head, has_prev, strict_prev):
    half = _lane_half()
    q_scale = HEAD_DIM ** -0.5 * LOG2_E
    qh = jnp.where(half if head == 0 else jnp.logical_not(half), q * q_scale, 0.0).astype(BF16)
    s = lax.dot_general(qh, kcat, NT_DIMS, preferred_element_type=F32)
    row = lax.broadcasted_iota(jnp.int32, (BLOCK, 2 * BLOCK), 0)
    col = lax.broadcasted_iota(jnp.int32, (BLOCK, 2 * BLOCK), 1)
    dist = row + BLOCK - col
    max_dist = BLOCK - 1 if strict_prev else BLOCK
    first_col = jnp.where(has_prev, 0, BLOCK)
    valid = (dist >= 0) & (dist <= max_dist) & (col >= first_col)
    s = jnp.where(valid, s, NEG)
    m = jnp.max(s, axis=1, keepdims=True)
    p = jnp.exp2(s - m)
    l = jnp.sum(p, axis=1, keepdims=True)
    acc = jnp.dot(p.astype(BF16), vcat, preferred_element_type=F32)
    return acc, m, l


def _pair(a0, a1):
    return jnp.where(_lane_half(), a0, a1)


def _dilated_kernel(*refs):
    q_refs, k_refs, v_refs = refs[0:3], refs[3:6], refs[6:9]
    o_ref = refs[9]
    acc_sc, m_sc, l_sc = refs[10:13]
    n_blocks = SEQ // BLOCK

    for g, (win, dil) in enumerate(DIL_PATTERNS):
        assert win // dil == BLOCK and SEQ % (dil * BLOCK) == 0
        q_ref, k_ref, v_ref = q_refs[g], k_refs[g], v_refs[g]

        def rows(start, dil=dil):
            return pl.ds(start, BLOCK) if dil == 1 else pl.ds(start, BLOCK, stride=dil)

        def block(it, g=g, dil=dil, q_ref=q_ref, k_ref=k_ref, v_ref=v_ref, rows=rows):
            r = it % dil
            i = it // dil
            base = i * (BLOCK * dil) + r
            prev = jnp.maximum(base - BLOCK * dil, 0)
            q = q_ref[rows(base), :]
            kcat = jnp.concatenate([k_ref[rows(prev), :], k_ref[rows(base), :]], axis=0).astype(BF16)
            vcat = jnp.concatenate([v_ref[rows(prev), :], v_ref[rows(base), :]], axis=0).astype(BF16)
            a0, m0, l0 = _band_scores(q, kcat, vcat, 0, i > 0, False)
            a1, m1, l1 = _band_scores(q, kcat, vcat, 1, i > 0, False)
            acc_sc[g, rows(base), :] = _pair(a0, a1)
            m_sc[g, rows(base), :] = _pair(m0, m1)
            l_sc[g, rows(base), :] = _pair(l0, l1)

        def body(it, carry, block=block):
            for u in range(BAND_UNROLL):
                block(it + u * (n_blocks // BAND_UNROLL))
            return carry

        lax.fori_loop(0, n_blocks // BAND_UNROLL, body, 0)

    def combine(i, carry):
        rows = pl.ds(pl.multiple_of(i * BLOCK, BLOCK), BLOCK)
        m = [m_sc[g, rows, :] for g in range(len(DIL_PATTERNS))]
        m_all = jnp.maximum(jnp.maximum(m[0], m[1]), m[2])
        num = jnp.zeros((BLOCK, LANES), F32)
        den = jnp.zeros((BLOCK, LANES), F32)
        for g in range(len(DIL_PATTERNS)):
            a = jnp.exp2(m[g] - m_all)
            num = num + a * acc_sc[g, rows, :]
            den = den + a * l_sc[g, rows, :]
        o_ref[rows, :] = (num / den).astype(o_ref.dtype)
        return carry

    lax.fori_loop(0, n_blocks, combine, 0)


def _dilated_attention(proj, nb):
    T = proj.shape[0]
    specs = []
    for col0 in (A_Q0, A_K0, A_V0):
        for g in range(len(DIL_PATTERNS)):
            specs.append(pl.BlockSpec((SEQ, LANES), lambda b, hp, c=col0 + 4 * g: (b, c + hp)))
    return pl.pallas_call(
        _dilated_kernel,
        out_shape=jax.ShapeDtypeStruct((T, BRANCH_WIDTH), BF16),
        grid=(nb, BRANCH_WIDTH // LANES),
        in_specs=specs,
        out_specs=pl.BlockSpec((SEQ, LANES), lambda b, hp: (b, hp)),
        scratch_shapes=[pltpu.VMEM((len(DIL_PATTERNS), SEQ, LANES), F32)] * 3,
        compiler_params=_cparams(2),
        name="dilated_attention",
    )(*([proj] * 9))


def _sink_kernel(q_ref, k_ref, v_ref, sink_ref, o_ref, k_sc, v_sc):
    hp = pl.program_id(1)
    kv_head = hp // 2
    k = k_ref[...]
    v = v_ref[...]
    k_sw = pltpu.roll(k, HEAD_DIM, 1)
    v_sw = pltpu.roll(v, HEAD_DIM, 1)
    first = kv_head == 0
    k_sc[0] = jnp.where(first, k, k_sw).astype(BF16)
    k_sc[1] = jnp.where(first, k_sw, k).astype(BF16)
    v_sc[0] = jnp.where(first, v, v_sw).astype(BF16)
    v_sc[1] = jnp.where(first, v_sw, v).astype(BF16)
    sink = _pair(sink_ref[pl.ds(2 * hp, 1), :], sink_ref[pl.ds(2 * hp + 1, 1), :])

    def block(i):
        base = pl.multiple_of(i * BLOCK, BLOCK)
        prev = pl.multiple_of(jnp.maximum(base - BLOCK, 0), BLOCK)
        q = q_ref[pl.ds(base, BLOCK), :]
        out = []
        for a in range(2):
            kcat = jnp.concatenate([k_sc[a, pl.ds(prev, BLOCK), :], k_sc[a, pl.ds(base, BLOCK), :]], axis=0)
            vcat = jnp.concatenate([v_sc[a, pl.ds(prev, BLOCK), :], v_sc[a, pl.ds(base, BLOCK), :]], axis=0)
            out.append(_band_scores(q, kcat, vcat, a, i > 0, True))
        acc = _pair(out[0][0], out[1][0])
        m = _pair(out[0][1], out[1][1])
        l = _pair(out[0][2], out[1][2])
        m_new = jnp.maximum(m, sink)
        scale = jnp.exp2(m - m_new)
        den = l * scale + jnp.exp2(sink - m_new)
        o_ref[pl.ds(base, BLOCK), :] = (acc * scale / den).astype(o_ref.dtype)

    n_blocks = SEQ // BLOCK

    def body(it, carry):
        for u in range(BAND_UNROLL):
            block(it + u * (n_blocks // BAND_UNROLL))
        return carry

    lax.fori_loop(0, n_blocks // BAND_UNROLL, body, 0)


def _sink_attention(proj, sinks_l, nb):
    T = proj.shape[0]
    sink_tab = jnp.broadcast_to(sinks_l.astype(F32)[:, None] * LOG2_E, (8, LANES))
    return pl.pallas_call(
        _sink_kernel,
        out_shape=jax.ShapeDtypeStruct((T, BRANCH_WIDTH), BF16),
        grid=(nb, BRANCH_WIDTH // LANES),
        in_specs=[
            pl.BlockSpec((SEQ, LANES), lambda b, hp: (b, B_Q0 + hp)),
            pl.BlockSpec((SEQ, LANES), lambda b, hp: (b, B_K0)),
            pl.BlockSpec((SEQ, LANES), lambda b, hp: (b, B_V0)),
            pl.BlockSpec((8, LANES), lambda b, hp: (0, 0)),
        ],
        out_specs=pl.BlockSpec((SEQ, LANES), lambda b, hp: (b, hp)),
        scratch_shapes=[pltpu.VMEM((2, SEQ, LANES), BF16)] * 2,
        compiler_params=_cparams(2),
        name="sink_attention",
    )(proj, proj, proj, sink_tab)


STICK_TQ = 256
STICK_SUBS = STICK_TQ // BLOCK
STICK_STAGES = 5


def _stick_tiles():
    n_key_blocks, n_q_tiles = SEQ // BLOCK, SEQ // STICK_TQ
    tiles = []
    for j in reversed(range(n_key_blocks)):
        first = j // STICK_SUBS
        tiles.append((first, j, 1 + j % STICK_SUBS))
        tiles += [(qi, j, 0) for qi in range(first + 1, n_q_tiles)]
    idle = (0, n_key_blocks - 1, STICK_SUBS + 1)
    lag = STICK_STAGES - 1
    stages = [[idle] * s + tiles + [idle] * (lag - s) for s in range(STICK_STAGES)]
    table = np.asarray(stages, np.int32)
    return table.shape[1], [np.ascontiguousarray(table[:, :, c]).reshape(-1) for c in range(3)]


def _stick_masks():
    row = np.arange(STICK_TQ)[:, None]
    col = np.arange(BLOCK)[None, :]
    valid = [np.ones((STICK_TQ, BLOCK), bool)]
    valid += [col + r * BLOCK < row for r in range(STICK_SUBS)]
    valid += [np.zeros((STICK_TQ, BLOCK), bool)]
    valid = np.stack(valid)
    masks = np.stack([valid.astype(np.float32), np.where(valid, 0.0, NEG).astype(np.float32)], axis=1)
    return masks.reshape(-1, STICK_TQ, BLOCK)


def _stick_kernel(n_slots, tq_ref, tk_ref, tm_ref, q_ref, k_ref, v_ref, tri_ref, mask_ref, o_ref,
                  q_sc, kd_sc, vd_sc, carry_sc, acc_sc, z_buf, hl_buf, sums_buf, w_buf):
    half = _lane_half()
    q_sc[...] = (q_ref[...] * (HEAD_DIM ** -0.5 * LOG2_E)).astype(BF16)
    n_key_blocks = SEQ // BLOCK
    k = k_ref[...]
    v = v_ref[...]
    for j in range(n_key_blocks):
        rows = slice(j * BLOCK, (j + 1) * BLOCK)
        kd_sc[j, :BLOCK] = jnp.where(half, k[rows], 0.0).astype(BF16)
        kd_sc[j, BLOCK:] = jnp.where(half, 0.0, k[rows]).astype(BF16)
        vd_sc[j, :BLOCK] = jnp.where(half, v[rows], 0.0).astype(BF16)
        vd_sc[j, BLOCK:] = jnp.where(half, 0.0, v[rows]).astype(BF16)
    carry_sc[...] = jnp.zeros_like(carry_sc)
    acc_sc[...] = jnp.zeros_like(acc_sc)
    z_buf[...] = jnp.zeros_like(z_buf)
    hl_buf[...] = jnp.zeros_like(hl_buf)
    sums_buf[...] = jnp.zeros_like(sums_buf)
    w_buf[...] = jnp.zeros_like(w_buf)
    tri = tri_ref[...]

    def q_rows(qi):
        return pl.ds(pl.multiple_of(qi * STICK_TQ, STICK_TQ), STICK_TQ)

    def trip(t, phase):
        cur, other = phase % 2, (phase + 1) % 2
        i = 4 * n_slots + t
        acc_sc[q_rows(tq_ref[i]), :] += jnp.dot(w_buf[cur], vd_sc[tk_ref[i]], preferred_element_type=F32)
        for a in range(2):
            sums_buf[2 * cur + a] = jnp.dot(hl_buf[2 * cur + a], tri, preferred_element_type=F32)
        z_buf[phase] = lax.dot_general(q_sc[q_rows(tq_ref[t]), :], kd_sc[tk_ref[t]], NT_DIMS,
                                       preferred_element_type=F32)
        i = 3 * n_slots + t
        rows = q_rows(tq_ref[i])
        bias = mask_ref[2 * tm_ref[i] + 1]
        z = z_buf[(phase + 1) % 4]
        w = []
        for a in range(2):
            sums = sums_buf[2 * other + a]
            carry = carry_sc[a, rows, :]
            w.append(jnp.exp2(z[:, a * BLOCK:(a + 1) * BLOCK] + carry + sums[:, :BLOCK] + bias))
            carry_sc[a, rows, :] = carry + sums[:, BLOCK:]
        w_buf[other] = jnp.concatenate(w, axis=1).astype(BF16)
        keep = mask_ref[2 * tm_ref[n_slots + t]]
        z = z_buf[(phase + 3) % 4]
        for a in range(2):
            za = z[:, a * BLOCK:(a + 1) * BLOCK]
            sp = (jnp.maximum(za, 0.0) + jnp.log2(1.0 + jnp.exp2(-jnp.abs(za)))) * keep
            hi = sp.astype(BF16)
            lo = (sp - hi.astype(F32)).astype(BF16)
            hl_buf[2 * other + a] = jnp.concatenate([hi, lo], axis=1)

    def body(i, c):
        for phase in range(4):
            trip(4 * i + phase, phase)
        return c

    assert n_slots % 4 == 0
    lax.fori_loop(0, n_slots // 4, body, 0)
    o_ref[...] = acc_sc[...].astype(o_ref.dtype)


def _stick_attention(proj, nb):
    T = proj.shape[0]
    jp = np.arange(BLOCK)[:, None]
    s = np.arange(BLOCK)[None, :]
    tri = -np.concatenate([(jp >= s), np.ones((BLOCK, BLOCK), bool)], axis=1).astype(np.float32)
    tri = jnp.asarray(np.concatenate([tri, tri], axis=0), BF16)
    n_slots, tables = _stick_tiles()
    masks = _stick_masks()
    n_key_blocks = SEQ // BLOCK
    return pl.pallas_call(
        functools.partial(_stick_kernel, n_slots),
        out_shape=jax.ShapeDtypeStruct((T, BRANCH_WIDTH), BF16),
        grid_spec=pltpu.PrefetchScalarGridSpec(
            num_scalar_prefetch=3,
            grid=(nb, BRANCH_WIDTH // LANES),
            in_specs=[
                pl.BlockSpec((SEQ, LANES), lambda b, hp, *_: (b, C_Q0 + hp)),
                pl.BlockSpec((SEQ, LANES), lambda b, hp, *_: (b, C_K0 + hp)),
                pl.BlockSpec((SEQ, LANES), lambda b, hp, *_: (b, C_V0 + hp)),
                pl.BlockSpec((2 * BLOCK, 2 * BLOCK), lambda b, hp, *_: (0, 0)),
                pl.BlockSpec(masks.shape, lambda b, hp, *_: (0, 0, 0)),
            ],
            out_specs=pl.BlockSpec((SEQ, LANES), lambda b, hp, *_: (b, hp)),
            scratch_shapes=[
                pltpu.VMEM((SEQ, LANES), BF16),
                pltpu.VMEM((n_key_blocks, 2 * BLOCK, LANES), BF16),
                pltpu.VMEM((n_key_blocks, 2 * BLOCK, LANES), BF16),
                pltpu.VMEM((2, SEQ, LANES), F32),
                pltpu.VMEM((SEQ, LANES), F32),
                pltpu.VMEM((4, STICK_TQ, 2 * BLOCK), F32),
                pltpu.VMEM((4, STICK_TQ, 2 * BLOCK), BF16),
                pltpu.VMEM((4, STICK_TQ, 2 * BLOCK), F32),
                pltpu.VMEM((2, STICK_TQ, 2 * BLOCK), BF16),
            ],
        ),
        compiler_params=_cparams(2),
        name="stick_breaking_attention",
    )(*[jnp.asarray(t) for t in tables], proj, proj, proj, tri, jnp.asarray(masks))


MLA_HEADS = 8
MLA_TM = 512


def _mla_prep_kernel(c_ref, gq_ref, gkv_ref, wq_ref, wk_ref, wv_ref, cos_ref, sin_ref, q_ref, k_ref, v_ref):
    c = c_ref[...]
    cq = _rms(c[:, :Q_LORA], gq_ref[...]).astype(BF16)
    ckv = _rms(c[:, Q_LORA:Q_LORA + KV_LORA], gkv_ref[...]).astype(BF16)
    q = jnp.dot(cq, wq_ref[...].astype(BF16), preferred_element_type=F32)
    k = jnp.dot(ckv, wk_ref[...].astype(BF16), preferred_element_type=F32)
    v_ref[...] = jnp.dot(ckv, wv_ref[...].astype(BF16), preferred_element_type=F32).astype(BF16)
    cos, sin = cos_ref[...], sin_ref[...]
    lane = lax.broadcasted_iota(jnp.int32, (MLA_TM, LANES), 1)
    rope_lane = (lane >= D_NOPE) & (lane < D_NOPE + D_ROPE)
    first = lane < D_NOPE + D_ROPE // 2
    kr = pltpu.roll(c[:, Q_LORA + KV_LORA:], D_NOPE, 1)
    kr = jnp.where(rope_lane, _rope_lanes(kr, cos, sin, D_ROPE // 2, first), 0.0)
    scale = (D_NOPE + D_ROPE) ** -0.5 * LOG2_E
    for h in range(MLA_HEADS):
        sl = slice(h * LANES, (h + 1) * LANES)
        q_ref[:, sl] = (_rope_lanes(q[:, sl], cos, sin, D_ROPE // 2, first) * scale).astype(BF16)
        k_ref[:, sl] = (k[:, sl] + kr).astype(BF16)


def _mla_prep(proj, g_qa, g_kva, w_uq, w_ukv, cos, sin):
    T = proj.shape[0]
    wq = jnp.pad(w_uq.reshape(Q_LORA, MLA_HEADS, D_NOPE + D_ROPE), ((0, 0), (0, 0), (0, LANES - D_NOPE - D_ROPE)))
    wq = wq.reshape(Q_LORA, MLA_HEADS * LANES)
    wkv = w_ukv.reshape(KV_LORA, MLA_HEADS, 2 * HEAD_DIM)
    wk = jnp.pad(wkv[:, :, :D_NOPE], ((0, 0), (0, 0), (0, LANES - D_NOPE))).reshape(KV_LORA, MLA_HEADS * LANES)
    wv = wkv[:, :, D_NOPE:].reshape(KV_LORA, MLA_HEADS * HEAD_DIM)
    full = lambda shape: pl.BlockSpec(shape, lambda i: (0, 0))
    per_seq = SEQ // MLA_TM
    return pl.pallas_call(
        _mla_prep_kernel,
        out_shape=[jax.ShapeDtypeStruct((T, MLA_HEADS * LANES), BF16),
                   jax.ShapeDtypeStruct((T, MLA_HEADS * LANES), BF16),
                   jax.ShapeDtypeStruct((T, MLA_HEADS * HEAD_DIM), BF16)],
        grid=(T // MLA_TM,),
        in_specs=[
            pl.BlockSpec((MLA_TM, MLA_BLOCK_W), lambda i: (i, MLA_COL0 // MLA_BLOCK_W)),
            full((1, Q_LORA)), full((1, KV_LORA)),
            full(wq.shape), full(wk.shape), full(wv.shape),
            pl.BlockSpec((MLA_TM, LANES), lambda i: (i % per_seq, 0)),
            pl.BlockSpec((MLA_TM, LANES), lambda i: (i % per_seq, 0)),
        ],
        out_specs=[pl.BlockSpec((MLA_TM, MLA_HEADS * LANES), lambda i: (i, 0)),
                   pl.BlockSpec((MLA_TM, MLA_HEADS * LANES), lambda i: (i, 0)),
                   pl.BlockSpec((MLA_TM, MLA_HEADS * HEAD_DIM), lambda i: (i, 0))],
        compiler_params=_cparams(1),
        name="mla_prep",
    )(proj, g_qa.reshape(1, Q_LORA), g_kva.reshape(1, KV_LORA), wq, wk, wv, cos, sin)


MLA_TQ = 1024
MLA_TK = 1024


def _mla_attn_kernel(q_ref, k_ref, v_ref, o_ref, v_sc, m_sc, acc_sc):
    half = lax.broadcasted_iota(jnp.int32, (SEQ, LANES), 1) < HEAD_DIM
    v = v_ref[...]
    one = jnp.ones_like(v)
    v_sc[0] = jnp.where(half, v, one)
    v_sc[1] = jnp.where(half, one, v)
    lane_half = lax.broadcasted_iota(jnp.int32, (MLA_TQ, LANES), 1) < HEAD_DIM

    def step(qb, kb, diagonal):
        for a in range(2):
            q = q_ref[pl.ds(qb, MLA_TQ), a * LANES:(a + 1) * LANES]
            k = k_ref[pl.ds(kb, MLA_TK), a * LANES:(a + 1) * LANES]
            s = lax.dot_general(q, k, (((1,), (1,)), ((), ())), preferred_element_type=F32)
            if diagonal:
                row = lax.broadcasted_iota(jnp.int32, (MLA_TQ, MLA_TK), 0)
                col = lax.broadcasted_iota(jnp.int32, (MLA_TQ, MLA_TK), 1)
                s = jnp.where(col + (kb - qb) <= row, s, NEG)
            m = m_sc[a]
            m_new = jnp.maximum(m, jnp.max(s, axis=1, keepdims=True))
            alpha = jnp.exp2(m - m_new)
            p = jnp.exp2(s - jnp.concatenate([m_new] * (MLA_TK // LANES), axis=1))
            pv = jnp.dot(p.astype(BF16), v_sc[a, pl.ds(kb, MLA_TK), :], preferred_element_type=F32)
            acc_sc[a] = alpha * acc_sc[a] + pv
            m_sc[a] = m_new

    def q_block(qi, c):
        qb = pl.multiple_of(qi * MLA_TQ, MLA_TQ)
        m_sc[...] = jnp.full(m_sc.shape, NEG, F32)
        acc_sc[...] = jnp.zeros(acc_sc.shape, F32)

        def k_block(kj, c2):
            step(qb, pl.multiple_of(kj * MLA_TK, MLA_TK), False)
            return c2

        lax.fori_loop(0, qi * (MLA_TQ // MLA_TK), k_block, 0)
        for d in range(MLA_TQ // MLA_TK):
            step(qb, qb + d * MLA_TK, True)
        outs = [acc_sc[a] / pltpu.roll(acc_sc[a], HEAD_DIM, 1) for a in range(2)]
        o_ref[pl.ds(qb, MLA_TQ), :] = jnp.where(lane_half, outs[0], outs[1]).astype(o_ref.dtype)
        return c

    lax.fori_loop(0, SEQ // MLA_TQ, q_block, 0)


def _mla_attention(q, k, v, nb):
    T = q.shape[0]
    return pl.pallas_call(
        _mla_attn_kernel,
        out_shape=jax.ShapeDtypeStruct((T, BRANCH_WIDTH), BF16),
        grid=(nb, BRANCH_WIDTH // LANES),
        in_specs=[
            pl.BlockSpec((SEQ, 2 * LANES), lambda b, hp: (b, hp)),
            pl.BlockSpec((SEQ, 2 * LANES), lambda b, hp: (b, hp)),
            pl.BlockSpec((SEQ, LANES), lambda b, hp: (b, hp)),
        ],
        out_specs=pl.BlockSpec((SEQ, LANES), lambda b, hp: (b, hp)),
        scratch_shapes=[pltpu.VMEM((2, SEQ, LANES), BF16),
                        pltpu.VMEM((2, MLA_TQ, LANES), F32),
                        pltpu.VMEM((2, MLA_TQ, LANES), F32)],
        compiler_params=_cparams(2),
        name="mla_attention",
    )(q, k, v)


MERGE_TM = 1024
MERGE_TN = 256


def _merge_kernel(*refs):
    n_ref = refs[0]
    o_refs = refs[1:5]
    wg_refs = refs[5:9]
    wb_refs = refs[9:13]
    out_ref = refs[13]
    n = n_ref[...]
    acc = jnp.zeros((MERGE_TM, MERGE_TN), F32)
    for i in range(N_BRANCH):
        gate = lax.dot_general(n, wg_refs[i][...], NT_DIMS, preferred_element_type=F32)
        br = jnp.dot(o_refs[i][...], wb_refs[i][...].astype(BF16), preferred_element_type=F32)
        acc = acc + br * jax.nn.sigmoid(gate)
    out_ref[...] = acc.astype(out_ref.dtype)


GATE_CAST_ROWS = 256


def _gate_cast_kernel(layer, w_hbm, o_ref, buf, sem):
    j = pl.program_id(0)

    def copy(step, slot):
        rows = pl.ds(pl.multiple_of(GATE_COL0 + step * GATE_CAST_ROWS, 8), GATE_CAST_ROWS)
        return pltpu.make_async_copy(w_hbm.at[layer, rows, :], buf.at[slot], sem.at[slot])

    @pl.when(j == 0)
    def _():
        copy(0, 0).start()

    @pl.when(j + 1 < pl.num_programs(0))
    def _():
        copy(j + 1, (j + 1) % 2).start()

    copy(j, j % 2).wait()
    o_ref[...] = buf[j % 2].astype(o_ref.dtype)


def _gate_weights_t(w_in_t, layer):
    n_gate = N_BRANCH * D_MODEL
    D = w_in_t.shape[2]
    assert GATE_COL0 + n_gate == w_in_t.shape[1] and GATE_COL0 % 8 == 0
    return pl.pallas_call(
        functools.partial(_gate_cast_kernel, layer),
        out_shape=jax.ShapeDtypeStruct((n_gate, D), BF16),
        grid=(n_gate // GATE_CAST_ROWS,),
        in_specs=[pl.BlockSpec(memory_space=pl.ANY)],
        out_specs=pl.BlockSpec((GATE_CAST_ROWS, D), lambda j: (j, 0)),
        scratch_shapes=[pltpu.VMEM((2, GATE_CAST_ROWS, D), F32), pltpu.SemaphoreType.DMA((2,))],
        compiler_params=_cparams(1),
        name="gate_weight_cast",
    )(w_in_t)


def _merge(n, branches, w_gates_t, w_branch, layer):
    T, D = n.shape
    nblk = D // MERGE_TN
    in_specs = [pl.BlockSpec((MERGE_TM, D), lambda m, j: (m, 0))]
    in_specs += [pl.BlockSpec((MERGE_TM, BRANCH_WIDTH), lambda m, j: (m, 0))] * N_BRANCH
    in_specs += [pl.BlockSpec((MERGE_TN, D), lambda m, j, i=i: (i * nblk + j, 0)) for i in range(N_BRANCH)]
    in_specs += [pl.BlockSpec((None, None, BRANCH_WIDTH, MERGE_TN), lambda m, j, i=i: (layer, i, 0, j))
                 for i in range(N_BRANCH)]
    return pl.pallas_call(
        _merge_kernel,
        out_shape=jax.ShapeDtypeStruct((T, D), BF16),
        grid=(T // MERGE_TM, nblk),
        in_specs=in_specs,
        out_specs=pl.BlockSpec((MERGE_TM, MERGE_TN), lambda m, j: (m, j)),
        compiler_params=_cparams(2),
        name="gated_merge",
    )(n, *branches, *([w_gates_t] * N_BRANCH), *([w_branch] * N_BRANCH))


def _matmul_res_kernel(a_ref, w_ref, r_ref, o_ref):
    o_ref[...] = r_ref[...] + jnp.dot(a_ref[...], w_ref[...].astype(BF16), preferred_element_type=F32)


def _matmul_res(a, w, layer, res, tm, tn):
    M, K = a.shape
    N = w.shape[2]
    return pl.pallas_call(
        _matmul_res_kernel,
        out_shape=jax.ShapeDtypeStruct((M, N), F32),
        grid=(M // tm, N // tn),
        in_specs=[pl.BlockSpec((tm, K), lambda m, j: (m, 0)),
                  pl.BlockSpec((None, K, tn), lambda m, j: (layer, 0, j)),
                  pl.BlockSpec((tm, tn), lambda m, j: (m, j))],
        out_specs=pl.BlockSpec((tm, tn), lambda m, j: (m, j)),
        compiler_params=_cparams(2),
        name="matmul_residual",
    )(a, w, res)


def _swiglu_up_kernel(x_ref, wg_ref, wu_ref, o_ref):
    x = x_ref[...]
    g = jnp.dot(x, wg_ref[...].astype(BF16), preferred_element_type=F32)
    u = jnp.dot(x, wu_ref[...].astype(BF16), preferred_element_type=F32)
    o_ref[...] = (jax.nn.silu(g) * u).astype(o_ref.dtype)


def _swiglu_up(x, wg, wu, layer, tm=1024, tn=512):
    M, K = x.shape
    N = wg.shape[2]
    return pl.pallas_call(
        _swiglu_up_kernel,
        out_shape=jax.ShapeDtypeStruct((M, N), BF16),
        grid=(M // tm, N // tn),
        in_specs=[pl.BlockSpec((tm, K), lambda m, j: (m, 0)),
                  pl.BlockSpec((None, K, tn), lambda m, j: (layer, 0, j)),
                  pl.BlockSpec((None, K, tn), lambda m, j: (layer, 0, j))],
        out_specs=pl.BlockSpec((tm, tn), lambda m, j: (m, j)),
        compiler_params=_cparams(2),
        name="swiglu_up",
    )(x, wg, wu)


ROUTER_TM = 512


def _router_kernel(h_ref, g_ref, wr_ref, idx_ref, w_ref):
    n = _rms(h_ref[...], g_ref[...])
    wr = wr_ref[...]
    n_hi = n.astype(BF16)
    n_lo = (n - n_hi.astype(F32)).astype(BF16)
    w_hi = wr.astype(BF16)
    w_lo = (wr - w_hi.astype(F32)).astype(BF16)
    logits = (jnp.dot(n_hi, w_hi, preferred_element_type=F32) + jnp.dot(n_lo, w_hi, preferred_element_type=F32)
              + jnp.dot(n_hi, w_lo, preferred_element_type=F32))
    lane = lax.broadcasted_iota(jnp.int32, logits.shape, 1)
    v1 = jnp.max(logits, axis=1, keepdims=True)
    i1 = jnp.min(jnp.where(logits == v1, lane, N_EXPERTS), axis=1, keepdims=True)
    rest = jnp.where(lane == i1, -jnp.inf, logits)
    v2 = jnp.max(rest, axis=1, keepdims=True)
    i2 = jnp.min(jnp.where(rest == v2, lane, N_EXPERTS), axis=1, keepdims=True)
    e2 = jnp.exp(v2 - v1)
    den = 1.0 + e2
    two = lax.broadcasted_iota(jnp.int32, (ROUTER_TM, TOP_K), 1)
    idx_ref[...] = jnp.where(two == 0, i1, i2)
    w_ref[...] = jnp.where(two == 0, 1.0 / den, e2 / den)


def _router(h, g, w_router):
    T, D = h.shape
    return pl.pallas_call(
        _router_kernel,
        out_shape=[jax.ShapeDtypeStruct((T, TOP_K), jnp.int32), jax.ShapeDtypeStruct((T, TOP_K), F32)],
        grid=(T // ROUTER_TM,),
        in_specs=[pl.BlockSpec((ROUTER_TM, D), lambda i: (i, 0)),
                  pl.BlockSpec((1, D), lambda i: (0, 0)),
                  pl.BlockSpec((D, N_EXPERTS), lambda i: (0, 0))],
        out_specs=[pl.BlockSpec((ROUTER_TM, TOP_K), lambda i: (i, 0)),
                   pl.BlockSpec((ROUTER_TM, TOP_K), lambda i: (i, 0))],
        compiler_params=_cparams(1),
        name="moe_router",
    )(h, g.reshape(1, D), w_router)


def _row_copy(src_hbm, row, dst_vmem, slot, sem):
    return pltpu.make_async_copy(src_hbm.at[pl.ds(row, 1), :], dst_vmem.at[pl.ds(slot, 1), :], sem)


def _expert_kernel(n_f, plan_ref, item_e_ref, item_row_ref, item_sub_ref, src_ref,
                   h_hbm, g_ref, wg_ref, wu_ref, wd_ref, y_hbm, gbuf, x_sc, acc_sc, sem_in, sem_out):
    it = pl.program_id(0)
    f = pl.program_id(1)
    n_it = pl.num_programs(0)
    n_items = plan_ref[0]
    live = it < n_items
    row0 = pl.multiple_of(item_row_ref[it], MOE_SUB)
    n_sub = item_sub_ref[it]
    item_rows = x_sc.shape[0]
    g_rows = gbuf.shape[0]
    chunk = g_rows // n_f

    def gather_wait():
        pltpu.make_async_copy(h_hbm.at[pl.ds(0, g_rows), :], gbuf, sem_in).wait()

    def y_copy(s):
        rows = pl.ds(s * MOE_SUB, MOE_SUB)
        return pltpu.make_async_copy(acc_sc.at[rows, :], y_hbm.at[pl.ds(row0 + s * MOE_SUB, MOE_SUB), :], sem_out)

    @pl.when(live & (it == 0) & (f == 0))
    def _():
        def start(r, c):
            _row_copy(h_hbm, src_ref[row0 + r], gbuf, r, sem_in).start()
            return c
        lax.fori_loop(0, g_rows, start, 0)

    @pl.when(live & (f == 0))
    def _():
        gather_wait()

        def norm(i, c):
            rows = pl.ds(pl.multiple_of(i * MOE_SUB, MOE_SUB), MOE_SUB)
            x_sc[rows, :] = _rms(gbuf[rows, :], g_ref[...]).astype(BF16)
            return c
        lax.fori_loop(0, item_rows // MOE_SUB, norm, 0)
        acc_sc[...] = jnp.zeros(acc_sc.shape, F32)

    @pl.when((it == n_items) & (it > 0) & (f == 0))
    def _():
        gather_wait()

    next_row0 = item_row_ref[jnp.minimum(it + 1, n_it - 1)]

    for n in range(1, MOE_ITEM_SUBS + 1):
        @pl.when(live & (n_sub == n))
        def _(n=n):
            lo = f * chunk
            for r in range(chunk):
                _row_copy(h_hbm, src_ref[next_row0 + lo + r], gbuf, lo + r, sem_in).start()
            rows = n * MOE_SUB
            x = x_sc[:rows, :]
            g = jnp.dot(x, wg_ref[...].astype(BF16), preferred_element_type=F32)
            u = jnp.dot(x, wu_ref[...].astype(BF16), preferred_element_type=F32)
            mid = (jax.nn.silu(g) * u).astype(BF16)
            acc_sc[:rows, :] += jnp.dot(mid, wd_ref[...].astype(BF16), preferred_element_type=F32)

    @pl.when(live & (f == n_f - 1))
    def _():
        for s in range(MOE_ITEM_SUBS):
            @pl.when(s < n_sub)
            def _(s=s):
                y_copy(s).start()
        for s in range(MOE_ITEM_SUBS):
            @pl.when(s < n_sub)
            def _(s=s):
                y_copy(s).wait()

    @pl.when((it == n_it - 1) & (f == n_f - 1))
    def _():
        @pl.when(live)
        def _():
            gather_wait()

        acc_sc[:MOE_SUB, :] = jnp.zeros((MOE_SUB, acc_sc.shape[1]), F32)
        first = plan_ref[1] // MOE_SUB
        n_blocks = y_hbm.shape[0] // MOE_SUB

        def fill(s):
            dst = y_hbm.at[pl.ds(pl.multiple_of(s * MOE_SUB, MOE_SUB), MOE_SUB), :]
            return pltpu.make_async_copy(acc_sc.at[pl.ds(0, MOE_SUB), :], dst, sem_out)

        lax.fori_loop(first, n_blocks, lambda s, c: (fill(s).start(), c)[1], 0)
        lax.fori_loop(first, n_blocks, lambda s, c: (fill(s).wait(), c)[1], 0)


def _gather_rows_per_item(n_f):
    item_rows = MOE_SUB * MOE_ITEM_SUBS
    chunk = -(-item_rows // n_f)
    chunk += -chunk % 8
    return chunk * n_f


def _expert_ffn(h, g, w_gate, w_up, w_down, plan, item_e, item_row, item_sub, src_rows, max_rows, max_items):
    D = h.shape[1]
    n_f = w_gate.shape[3] // MOE_TF
    item_rows = MOE_SUB * MOE_ITEM_SUBS

    def f_of(it, f, plan_ref):
        return jnp.where(it < plan_ref[0], f, n_f - 1)

    return pl.pallas_call(
        functools.partial(_expert_kernel, n_f),
        out_shape=jax.ShapeDtypeStruct((max_rows, D), F32),
        grid_spec=pltpu.PrefetchScalarGridSpec(
            num_scalar_prefetch=5,
            grid=(max_items, n_f),
            in_specs=[
                pl.BlockSpec(memory_space=pl.ANY),
                pl.BlockSpec((1, D), lambda it, f, n, e, r, s, src: (0, 0)),
                pl.BlockSpec((None, None, D, MOE_TF), lambda it, f, n, e, r, s, src: (0, e[it], 0, f_of(it, f, n))),
                pl.BlockSpec((None, None, D, MOE_TF), lambda it, f, n, e, r, s, src: (0, e[it], 0, f_of(it, f, n))),
                pl.BlockSpec((None, None, MOE_TF, D), lambda it, f, n, e, r, s, src: (0, e[it], f_of(it, f, n), 0)),
            ],
            out_specs=pl.BlockSpec(memory_space=pl.ANY),
            scratch_shapes=[pltpu.VMEM((_gather_rows_per_item(n_f), D), F32),
                            pltpu.VMEM((item_rows, D), BF16),
                            pltpu.VMEM((item_rows, D), F32),
                            pltpu.SemaphoreType.DMA(()), pltpu.SemaphoreType.DMA(())],
        ),
        compiler_params=_cparams(2),
        name="moe_expert_ffn",
    )(plan, item_e, item_row, item_sub, src_rows, h, g.reshape(1, D), w_gate, w_up, w_down)


COMBINE_TM = 256


def _combine_kernel(pos_ref, h_ref, w_ref, g_ref, y_hbm, o_ref, buf, sem):
    i = pl.program_id(0)

    def issue(step, slot):
        base = step * COMBINE_TM

        def start(r, c):
            for k in range(TOP_K):
                _row_copy(y_hbm, pos_ref[(base + r) * TOP_K + k], buf.at[slot, k], r, sem.at[slot]).start()
            return c

        lax.fori_loop(0, COMBINE_TM, start, 0, unroll=8)

    @pl.when(i == 0)
    def _():
        issue(0, 0)

    @pl.when(i + 1 < pl.num_programs(0))
    def _():
        issue(i + 1, (i + 1) % 2)

    slot = i % 2
    for k in range(TOP_K):
        pltpu.make_async_copy(y_hbm.at[pl.ds(0, COMBINE_TM), :], buf.at[slot, k], sem.at[slot]).wait()
    w = w_ref[...]
    out = h_ref[...] + w[:, 0:1] * buf[slot, 0] + w[:, 1:2] * buf[slot, 1]
    o_ref[...] = _rms(out, g_ref[...])


def _combine_norm(h, ys, pos, top_w, g):
    T, D = h.shape
    return pl.pallas_call(
        _combine_kernel,
        out_shape=jax.ShapeDtypeStruct((T, D), F32),
        grid_spec=pltpu.PrefetchScalarGridSpec(
            num_scalar_prefetch=1,
            grid=(T // COMBINE_TM,),
            in_specs=[pl.BlockSpec((COMBINE_TM, D), lambda i, p: (i, 0)),
                      pl.BlockSpec((COMBINE_TM, TOP_K), lambda i, p: (i, 0)),
                      pl.BlockSpec((1, D), lambda i, p: (0, 0)),
                      pl.BlockSpec(memory_space=pl.ANY)],
            out_specs=pl.BlockSpec((COMBINE_TM, D), lambda i, p: (i, 0)),
            scratch_shapes=[pltpu.VMEM((2, TOP_K, COMBINE_TM, D), F32), pltpu.SemaphoreType.DMA((2,))],
        ),
        compiler_params=_cparams(1),
        name="moe_combine_norm",
    )(pos.reshape(-1), h, top_w, g.reshape(1, D), ys)


def _moe_plan(top_idx, gather_rows):
    T = top_idx.shape[0]
    n_assign = T * TOP_K
    item_rows = MOE_SUB * MOE_ITEM_SUBS
    max_rows = n_assign + N_EXPERTS * MOE_SUB
    max_items = n_assign // item_rows + N_EXPERTS
    e_flat = top_idx.reshape(-1)
    onehot = (e_flat[:, None] == jnp.arange(N_EXPERTS)[None, :]).astype(jnp.int32)
    ranks = jnp.cumsum(onehot, axis=0) - onehot
    counts = jnp.sum(onehot, axis=0)
    padded = ((counts + MOE_SUB - 1) // MOE_SUB) * MOE_SUB
    group_start = jnp.cumsum(padded) - padded
    rank = jnp.sum(ranks * onehot, axis=1)
    pos = group_start[e_flat] + rank
    src_rows = jnp.zeros((max_rows + gather_rows,), jnp.int32).at[pos].set(
        jnp.arange(n_assign, dtype=jnp.int32) // TOP_K, unique_indices=True)
    items_per_e = (padded + item_rows - 1) // item_rows
    item_start = jnp.cumsum(items_per_e) - items_per_e
    n_items = jnp.sum(items_per_e)
    it = jnp.arange(max_items)
    item_e = jnp.minimum(jnp.sum(it[:, None] >= (item_start + items_per_e)[None, :], axis=1), N_EXPERTS - 1)
    local = it - item_start[item_e]
    item_row = group_start[item_e] + local * item_rows
    item_sub = jnp.clip((padded[item_e] - local * item_rows) // MOE_SUB, 0, MOE_ITEM_SUBS)
    live = it < n_items
    last = jnp.maximum(n_items - 1, 0)
    item_e = jnp.where(live, item_e, item_e[last])
    item_row = jnp.where(live, item_row, 0)
    item_sub = jnp.where(live, item_sub, 0)
    i32 = lambda a: a.astype(jnp.int32)
    plan = jnp.stack([n_items, jnp.sum(padded)])
    return (i32(pos.reshape(T, TOP_K)), src_rows, i32(plan), i32(item_e), i32(item_row),
            i32(item_sub), max_rows, max_items)


def _moe_and_final_norm(h, g_ffn, w_router, w_gate, w_up, w_down, g_final):
    top_idx, top_w = _router(h, g_ffn, w_router)
    gather_rows = _gather_rows_per_item(w_gate.shape[3] // MOE_TF)
    pos, src_rows, plan, item_e, item_row, item_sub, max_rows, max_items = _moe_plan(top_idx, gather_rows)
    ys = _expert_ffn(h, g_ffn, w_gate, w_up, w_down, plan, item_e, item_row, item_sub, src_rows, max_rows,
                     max_items)
    return _combine_norm(h, ys, pos, top_w, g_final)


def kernel(x, w_in, w_branch, w_out, norm_mix, norm_ffn, norm_final, sinks, mla_q_norm, mla_kv_norm, mla_w_uq,
           mla_w_ukv, ffn_w_gate, ffn_w_up, ffn_w_down, router_w, moe_w_gate, moe_w_up, moe_w_down):
    nb, seq, d = x.shape
    assert (seq, d) == (SEQ, D_MODEL)
    depth = w_in.shape[0]
    assert depth == 2, "layer 0 uses the dense FFN, layer 1 the MoE followed by the final norm"
    cos64, sin64 = _rope_tables(HEAD_DIM, 0, HEAD_DIM)
    cos32, sin32 = _rope_tables(D_ROPE, D_NOPE, LANES)
    h = x.reshape(nb * seq, d)
    w_in_t = jnp.swapaxes(w_in, 1, 2)
    out = None
    for layer in range(depth):
        n = _rmsnorm(h, norm_mix[layer], BF16)
        proj = _in_proj(n, w_in_t, layer, cos64, sin64)
        o_a = _dilated_attention(proj, nb)
        o_b = _sink_attention(proj, sinks[layer], nb)
        o_c = _stick_attention(proj, nb)
        q_d, k_d, v_d = _mla_prep(proj, mla_q_norm[layer], mla_kv_norm[layer], mla_w_uq[layer], mla_w_ukv[layer],
                                  cos32, sin32)
        o_d = _mla_attention(q_d, k_d, v_d, nb)
        w_gates_t = _gate_weights_t(w_in_t, layer)
        merged = _merge(n, (o_a, o_b, o_c, o_d), w_gates_t, w_branch, layer)
        h = _matmul_res(merged, w_out, layer, h, 2048, 512)
        if layer == 0:
            n2 = _rmsnorm(h, norm_ffn[layer], BF16)
            mid = _swiglu_up(n2, ffn_w_gate, ffn_w_up, 0)
            h = _matmul_res(mid, ffn_w_down, 0, h, 1024, 256)
        else:
            out = _moe_and_final_norm(h, norm_ffn[layer], router_w[0], moe_w_gate, moe_w_up, moe_w_down, norm_final)
    return out.reshape(nb, seq, d)
```

```python
import functools

import numpy as np
import jax
import jax.numpy as jnp
from jax import lax
from jax.experimental import pallas as pl
from jax.experimental.pallas import tpu as pltpu

F32 = jnp.float32
BF16 = jnp.bfloat16

D_MODEL = 2048
SEQ = 2048
HEAD_DIM = 64
BLOCK = 128
LANES = 128
ROPE_THETA = 10000.0
NORM_EPS = 1e-6
DIL_PATTERNS = ((128, 1), (512, 4), (2048, 16))
B_WINDOW = 128
D_NOPE = 64
D_ROPE = 32
Q_LORA = 384
KV_LORA = 256
N_BRANCH = 4
BRANCH_WIDTH = 512
N_EXPERTS = 8
TOP_K = 2

A_Q0, A_K0, A_V0 = 0, 12, 24
B_Q0, B_K0, B_V0 = 36, 40, 41
C_Q0, C_K0, C_V0 = 42, 46, 50
MLA_COL0 = 6912
MLA_BLOCK_W = 768
QKV_WIDTH = 7680
GATE_COL0 = 7584
PROJ_TN = 512

VMEM_LIMIT = 56 * 1024 * 1024
NEG = -1e30
NT_DIMS = (((1,), (1,)), ((), ()))
LOG2_E = 1.4426950408889634

MOE_SUB = 128
MOE_ITEM_SUBS = 8
MOE_TF = 256


def _cparams(n_axes):
    return pltpu.CompilerParams(dimension_semantics=("arbitrary",) * n_axes, vmem_limit_bytes=VMEM_LIMIT)


def _lane_half():
    return lax.broadcasted_iota(jnp.int32, (BLOCK, LANES), 1) < HEAD_DIM


def _rms(x, g):
    return x * lax.rsqrt(jnp.mean(x * x, axis=-1, keepdims=True) + NORM_EPS) * g


def _rmsnorm_kernel(x_ref, g_ref, o_ref):
    o_ref[...] = _rms(x_ref[...], g_ref[...]).astype(o_ref.dtype)


def _rmsnorm(x, g, out_dtype, tm=512):
    T, D = x.shape
    return pl.pallas_call(
        _rmsnorm_kernel,
        out_shape=jax.ShapeDtypeStruct((T, D), out_dtype),
        grid=(T // tm,),
        in_specs=[pl.BlockSpec((tm, D), lambda i: (i, 0)), pl.BlockSpec((1, D), lambda i: (0, 0))],
        out_specs=pl.BlockSpec((tm, D), lambda i: (i, 0)),
        compiler_params=_cparams(1),
        name="rmsnorm",
    )(x, g.reshape(1, D))


def _rope_tables(dh, lane0, period):
    half = dh // 2
    freqs = ROPE_THETA ** (-2.0 * jnp.arange(half, dtype=F32) / dh)
    ang = jnp.arange(SEQ, dtype=F32)[:, None] * freqs[None, :]
    cos, sin = jnp.cos(ang), jnp.sin(ang)
    cos_h = jnp.concatenate([cos, cos], axis=1)
    sin_h = jnp.concatenate([-sin, sin], axis=1)
    ones = jnp.ones((SEQ, period - dh), F32)
    cos_p = jnp.concatenate([cos_h, ones], axis=1)
    sin_p = jnp.concatenate([sin_h, 0.0 * ones], axis=1)
    reps = LANES // period
    cos_t = jnp.roll(jnp.tile(cos_p, (1, reps)), lane0, axis=1)
    sin_t = jnp.roll(jnp.tile(sin_p, (1, reps)), lane0, axis=1)
    return cos_t, sin_t


def _rope_lanes(x, cos, sin, half, first_half_mask):
    fwd = pltpu.roll(x, LANES - half, 1)
    bwd = pltpu.roll(x, half, 1)
    return x * cos + jnp.where(first_half_mask, fwd, bwd) * sin


def _proj_kernel(x_ref, w_ref, cos_ref, sin_ref, o_ref):
    j = pl.program_id(1)
    tiles = PROJ_TN // LANES
    all_rope = (j < (A_V0 // tiles)) | (j == B_Q0 // tiles)
    first_rope = j == B_K0 // tiles
    assert A_V0 % tiles == 0 and B_Q0 % tiles == 0 and B_K0 % tiles == 0 and B_K0 - B_Q0 == tiles

    def project():
        return lax.dot_general(x_ref[...], w_ref[...].astype(BF16), NT_DIMS, preferred_element_type=F32)

    def store(acc, n_rope):
        lane = lax.broadcasted_iota(jnp.int32, (SEQ, LANES), 1)
        first = (lane % HEAD_DIM) < HEAD_DIM // 2
        for c in range(n_rope):
            cols = slice(c * LANES, (c + 1) * LANES)
            o_ref[:, cols] = _rope_lanes(acc[:, cols], cos_ref[...], sin_ref[...], HEAD_DIM // 2, first)
        if n_rope < tiles:
            o_ref[:, n_rope * LANES:] = acc[:, n_rope * LANES:]

    @pl.when(all_rope)
    def _():
        store(project(), tiles)

    @pl.when(first_rope)
    def _():
        store(project(), 1)

    @pl.when(jnp.logical_not(all_rope | first_rope))
    def _():
        o_ref[...] = project()


def _in_proj(n, w_in_t, layer, cos, sin):
    T, D = n.shape
    nb = T // SEQ
    return pl.pallas_call(
        _proj_kernel,
        out_shape=jax.ShapeDtypeStruct((T, QKV_WIDTH), F32),
        grid=(nb, QKV_WIDTH // PROJ_TN),
        in_specs=[
            pl.BlockSpec((SEQ, D), lambda b, j: (b, 0)),
            pl.BlockSpec((None, PROJ_TN, D), lambda b, j: (layer, j, 0)),
            pl.BlockSpec((SEQ, LANES), lambda b, j: (0, 0)),
            pl.BlockSpec((SEQ, LANES), lambda b, j: (0, 0)),
        ],
        out_specs=pl.BlockSpec((SEQ, PROJ_TN), lambda b, j: (b, j)),
        compiler_params=_cparams(2),
        name="in_proj_rope",
    )(n, w_in_t, cos, sin)


BAND_UNROLL = 16


def _band_scores(q, kcat, vcat, head, has_prev, strict_prev):
    half = _lane_half()
    q_scale = HEAD_DIM ** -0.5 * LOG2_E
    qh = jnp.where(half if head == 0 else jnp.logical_not(half), q * q_scale, 0.0).astype(BF16)
    s = lax.dot_general(qh, kcat, NT_DIMS, preferred_element_type=F32)
    row = lax.broadcasted_iota(jnp.int32, (BLOCK, 2 * BLOCK), 0)
    col = lax.broadcasted_iota(jnp.int32, (BLOCK, 2 * BLOCK), 1)
    dist = row + BLOCK - col
    max_dist = BLOCK - 1 if strict_prev else BLOCK
    first_col = jnp.where(has_prev, 0, BLOCK)
    valid = (dist >= 0) & (dist <= max_dist) & (col >= first_col)
    s = jnp.where(valid, s, NEG)
    m = jnp.max(s, axis=1, keepdims=True)
    p = jnp.exp2(s - m)
    l = jnp.sum(p, axis=1, keepdims=True)
    acc = jnp.dot(p.astype(BF16), vcat, preferred_element_type=F32)
    return acc, m, l


def _pair(a0, a1):
    return jnp.where(_lane_half(), a0, a1)


def _dilated_kernel(*refs):
    q_refs, k_refs, v_refs = refs[0:3], refs[3:6], refs[6:9]
    o_ref = refs[9]
    acc_sc, m_sc, l_sc = refs[10:13]
    n_blocks = SEQ // BLOCK

    for g, (win, dil) in enumerate(DIL_PATTERNS):
        assert win // dil == BLOCK and SEQ % (dil * BLOCK) == 0
        q_ref, k_ref, v_ref = q_refs[g], k_refs[g], v_refs[g]

        def rows(start, dil=dil):
            return pl.ds(start, BLOCK) if dil == 1 else pl.ds(start, BLOCK, stride=dil)

        def block(it, g=g, dil=dil, q_ref=q_ref, k_ref=k_ref, v_ref=v_ref, rows=rows):
            r = it % dil
            i = it // dil
            base = i * (BLOCK * dil) + r
            prev = jnp.maximum(base - BLOCK * dil, 0)
            q = q_ref[rows(base), :]
            kcat = jnp.concatenate([k_ref[rows(prev), :], k_ref[rows(base), :]], axis=0).astype(BF16)
            vcat = jnp.concatenate([v_ref[rows(prev), :], v_ref[rows(base), :]], axis=0).astype(BF16)
            a0, m0, l0 = _band_scores(q, kcat, vcat, 0, i > 0, False)
            a1, m1, l1 = _band_scores(q, kcat, vcat, 1, i > 0, False)
            acc_sc[g, rows(base), :] = _pair(a0, a1)
            m_sc[g, rows(base), :] = _pair(m0, m1)
            l_sc[g, rows(base), :] = _pair(l0, l1)

        def body(it, carry, block=block):
            for u in range(BAND_UNROLL):
                block(it + u * (n_blocks // BAND_UNROLL))
            return carry

        lax.fori_loop(0, n_blocks // BAND_UNROLL, body, 0)

    def combine(i, carry):
        rows = pl.ds(pl.multiple_of(i * BLOCK, BLOCK), BLOCK)
        m = [m_sc[g, rows, :] for g in range(len(DIL_PATTERNS))]
        m_all = jnp.maximum(jnp.maximum(m[0], m[1]), m[2])
        num = jnp.zeros((BLOCK, LANES), F32)
        den = jnp.zeros((BLOCK, LANES), F32)
        for g in range(len(DIL_PATTERNS)):
            a = jnp.exp2(m[g] - m_all)
            num = num + a * acc_sc[g, rows, :]
            den = den + a * l_sc[g, rows, :]
        o_ref[rows, :] = (num / den).astype(o_ref.dtype)
        return carry

    lax.fori_loop(0, n_blocks, combine, 0)


def _dilated_attention(proj, nb):
    T = proj.shape[0]
    specs = []
    for col0 in (A_Q0, A_K0, A_V0):
        for g in range(len(DIL_PATTERNS)):
            specs.append(pl.BlockSpec((SEQ, LANES), lambda b, hp, c=col0 + 4 * g: (b, c + hp)))
    return pl.pallas_call(
        _dilated_kernel,
        out_shape=jax.ShapeDtypeStruct((T, BRANCH_WIDTH), BF16),
        grid=(nb, BRANCH_WIDTH // LANES),
        in_specs=specs,
        out_specs=pl.BlockSpec((SEQ, LANES), lambda b, hp: (b, hp)),
        scratch_shapes=[pltpu.VMEM((len(DIL_PATTERNS), SEQ, LANES), F32)] * 3,
        compiler_params=_cparams(2),
        name="dilated_attention",
    )(*([proj] * 9))


def _sink_kernel(q_ref, k_ref, v_ref, sink_ref, o_ref, k_sc, v_sc):
    hp = pl.program_id(1)
    kv_head = hp // 2
    k = k_ref[...]
    v = v_ref[...]
    k_sw = pltpu.roll(k, HEAD_DIM, 1)
    v_sw = pltpu.roll(v, HEAD_DIM, 1)
    first = kv_head == 0
    k_sc[0] = jnp.where(first, k, k_sw).astype(BF16)
    k_sc[1] = jnp.where(first, k_sw, k).astype(BF16)
    v_sc[0] = jnp.where(first, v, v_sw).astype(BF16)
    v_sc[1] = jnp.where(first, v_sw, v).astype(BF16)
    sink = _pair(sink_ref[pl.ds(2 * hp, 1), :], sink_ref[pl.ds(2 * hp + 1, 1), :])

    def block(i):
        base = pl.multiple_of(i * BLOCK, BLOCK)
        prev = pl.multiple_of(jnp.maximum(base - BLOCK, 0), BLOCK)
        q = q_ref[pl.ds(base, BLOCK), :]
        out = []
        for a in range(2):
            kcat = jnp.concatenate([k_sc[a, pl.ds(prev, BLOCK), :], k_sc[a, pl.ds(base, BLOCK), :]], axis=0)
            vcat = jnp.concatenate([v_sc[a, pl.ds(prev, BLOCK), :], v_sc[a, pl.ds(base, BLOCK), :]], axis=0)
            out.append(_band_scores(q, kcat, vcat, a, i > 0, True))
        acc = _pair(out[0][0], out[1][0])
        m = _pair(out[0][1], out[1][1])
        l = _pair(out[0][2], out[1][2])
        m_new = jnp.maximum(m, sink)
        scale = jnp.exp2(m - m_new)
        den = l * scale + jnp.exp2(sink - m_new)
        o_ref[pl.ds(base, BLOCK), :] = (acc * scale / den).astype(o_ref.dtype)

    n_blocks = SEQ // BLOCK

    def body(it, carry):
        for u in range(BAND_UNROLL):
            block(it + u * (n_blocks // BAND_UNROLL))
        return carry

    lax.fori_loop(0, n_blocks // BAND_UNROLL, body, 0)


def _sink_attention(proj, sinks_l, nb):
    T = proj.shape[0]
    sink_tab = jnp.broadcast_to(sinks_l.astype(F32)[:, None] * LOG2_E, (8, LANES))
    return pl.pallas_call(
        _sink_kernel,
        out_shape=jax.ShapeDtypeStruct((T, BRANCH_WIDTH), BF16),
        grid=(nb, BRANCH_WIDTH // LANES),
        in_specs=[
            pl.BlockSpec((SEQ, LANES), lambda b, hp: (b, B_Q0 + hp)),
            pl.BlockSpec((SEQ, LANES), lambda b, hp: (b, B_K0)),
            pl.BlockSpec((SEQ, LANES), lambda b, hp: (b, B_V0)),
            pl.BlockSpec((8, LANES), lambda b, hp: (0, 0)),
        ],
        out_specs=pl.BlockSpec((SEQ, LANES), lambda b, hp: (b, hp)),
        scratch_shapes=[pltpu.VMEM((2, SEQ, LANES), BF16)] * 2,
        compiler_params=_cparams(2),
        name="sink_attention",
    )(proj, proj, proj, sink_tab)


STICK_TQ = 256
STICK_SUBS = STICK_TQ // BLOCK
STICK_STAGES = 5


def _stick_tiles():
    n_key_blocks, n_q_tiles = SEQ // BLOCK, SEQ // STICK_TQ
    tiles = []
    for j in reversed(range(n_key_blocks)):
        first = j // STICK_SUBS
        tiles.append((first, j, 1 + j % STICK_SUBS))
        tiles += [(qi, j, 0) for qi in range(first + 1, n_q_tiles)]
    idle = (0, n_key_blocks - 1, STICK_SUBS + 1)
    lag = STICK_STAGES - 1
    stages = [[idle] * s + tiles + [idle] * (lag - s) for s in range(STICK_STAGES)]
    table = np.asarray(stages, np.int32)
    return table.shape[1], [np.ascontiguousarray(table[:, :, c]).reshape(-1) for c in range(3)]


def _stick_masks():
    row = np.arange(STICK_TQ)[:, None]
    col = np.arange(BLOCK)[None, :]
    valid = [np.ones((STICK_TQ, BLOCK), bool)]
    valid += [col + r * BLOCK < row for r in range(STICK_SUBS)]
    valid += [np.zeros((STICK_TQ, BLOCK), bool)]
    valid = np.stack(valid)
    masks = np.stack([valid.astype(np.float32), np.where(valid, 0.0, NEG).astype(np.float32)], axis=1)
    return masks.reshape(-1, STICK_TQ, BLOCK)


def _stick_kernel(n_slots, tq_ref, tk_ref, tm_ref, q_ref, k_ref, v_ref, tri_ref, mask_ref, o_ref,
                  q_sc, kd_sc, vd_sc, carry_sc, acc_sc, z_buf, hl_buf, sums_buf, w_buf):
    half = _lane_half()
    q_sc[...] = (q_ref[...] * (HEAD_DIM ** -0.5 * LOG2_E)).astype(BF16)
    n_key_blocks = SEQ // BLOCK
    k = k_ref[...]
    v = v_ref[...]
    for j in range(n_key_blocks):
        rows = slice(j * BLOCK, (j + 1) * BLOCK)
        kd_sc[j, :BLOCK] = jnp.where(half, k[rows], 0.0).astype(BF16)
        kd_sc[j, BLOCK:] = jnp.where(half, 0.0, k[rows]).astype(BF16)
        vd_sc[j, :BLOCK] = jnp.where(half, v[rows], 0.0).astype(BF16)
        vd_sc[j, BLOCK:] = jnp.where(half, 0.0, v[rows]).astype(BF16)
    carry_sc[...] = jnp.zeros_like(carry_sc)
    acc_sc[...] = jnp.zeros_like(acc_sc)
    z_buf[...] = jnp.zeros_like(z_buf)
    hl_buf[...] = jnp.zeros_like(hl_buf)
    sums_buf[...] = jnp.zeros_like(sums_buf)
    w_buf[...] = jnp.zeros_like(w_buf)
    tri = tri_ref[...]

    def q_rows(qi):
        return pl.ds(pl.multiple_of(qi * STICK_TQ, STICK_TQ), STICK_TQ)

    def trip(t, phase):
        cur, other = phase % 2, (phase + 1) % 2
        i = 4 * n_slots + t
        acc_sc[q_rows(tq_ref[i]), :] += jnp.dot(w_buf[cur], vd_sc[tk_ref[i]], preferred_element_type=F32)
        for a in range(2):
            sums_buf[2 * cur + a] = jnp.dot(hl_buf[2 * cur + a], tri, preferred_element_type=F32)
        z_buf[phase] = lax.dot_general(q_sc[q_rows(tq_ref[t]), :], kd_sc[tk_ref[t]], NT_DIMS,
                                       preferred_element_type=F32)
        i = 3 * n_slots + t
        rows = q_rows(tq_ref[i])
        bias = mask_ref[2 * tm_ref[i] + 1]
        z = z_buf[(phase + 1) % 4]
        w = []
        for a in range(2):
            sums = sums_buf[2 * other + a]
            carry = carry_sc[a, rows, :]
            w.append(jnp.exp2(z[:, a * BLOCK:(a + 1) * BLOCK] + carry + sums[:, :BLOCK] + bias))
            carry_sc[a, rows, :] = carry + sums[:, BLOCK:]
        w_buf[other] = jnp.concatenate(w, axis=1).astype(BF16)
        keep = mask_ref[2 * tm_ref[n_slots + t]]
        z = z_buf[(phase + 3) % 4]
        for a in range(2):
            za = z[:, a * BLOCK:(a + 1) * BLOCK]
            sp = (jnp.maximum(za, 0.0) + jnp.log2(1.0 + jnp.exp2(-jnp.abs(za)))) * keep
            hi = sp.astype(BF16)
            lo = (sp - hi.astype(F32)).astype(BF16)
            hl_buf[2 * other + a] = jnp.concatenate([hi, lo], axis=1)

    def body(i, c):
        for phase in range(4):
            trip(4 * i + phase, phase)
        return c

    assert n_slots % 4 == 0
    lax.fori_loop(0, n_slots // 4, body, 0)
    o_ref[...] = acc_sc[...].astype(o_ref.dtype)


def _stick_attention(proj, nb):
    T = proj.shape[0]
    jp = np.arange(BLOCK)[:, None]
    s = np.arange(BLOCK)[None, :]
    tri = -np.concatenate([(jp >= s), np.ones((BLOCK, BLOCK), bool)], axis=1).astype(np.float32)
    tri = jnp.asarray(np.concatenate([tri, tri], axis=0), BF16)
    n_slots, tables = _stick_tiles()
    masks = _stick_masks()
    n_key_blocks = SEQ // BLOCK
    return pl.pallas_call(
        functools.partial(_stick_kernel, n_slots),
        out_shape=jax.ShapeDtypeStruct((T, BRANCH_WIDTH), BF16),
        grid_spec=pltpu.PrefetchScalarGridSpec(
            num_scalar_prefetch=3,
            grid=(nb, BRANCH_WIDTH // LANES),
            in_specs=[
                pl.BlockSpec((SEQ, LANES), lambda b, hp, *_: (b, C_Q0 + hp)),
                pl.BlockSpec((SEQ, LANES), lambda b, hp, *_: (b, C_K0 + hp)),
                pl.BlockSpec((SEQ, LANES), lambda b, hp, *_: (b, C_V0 + hp)),
                pl.BlockSpec((2 * BLOCK, 2 * BLOCK), lambda b, hp, *_: (0, 0)),
                pl.BlockSpec(masks.shape, lambda b, hp, *_: (0, 0, 0)),
            ],
            out_specs=pl.BlockSpec((SEQ, LANES), lambda b, hp, *_: (b, hp)),
            scratch_shapes=[
                pltpu.VMEM((SEQ, LANES), BF16),
                pltpu.VMEM((n_key_blocks, 2 * BLOCK, LANES), BF16),
                pltpu.VMEM((n_key_blocks, 2 * BLOCK, LANES), BF16),
                pltpu.VMEM((2, SEQ, LANES), F32),
                pltpu.VMEM((SEQ, LANES), F32),
                pltpu.VMEM((4, STICK_TQ, 2 * BLOCK), F32),
                pltpu.VMEM((4, STICK_TQ, 2 * BLOCK), BF16),
                pltpu.VMEM((4, STICK_TQ, 2 * BLOCK), F32),
                pltpu.VMEM((2, STICK_TQ, 2 * BLOCK), BF16),
            ],
        ),
        compiler_params=_cparams(2),
        name="stick_breaking_attention",
    )(*[jnp.asarray(t) for t in tables], proj, proj, proj, tri, jnp.asarray(masks))


MLA_HEADS = 8
MLA_TM = 512


def _mla_prep_kernel(c_ref, gq_ref, gkv_ref, wq_ref, wk_ref, wv_ref, cos_ref, sin_ref, q_ref, k_ref, v_ref):
    c = c_ref[...]
    cq = _rms(c[:, :Q_LORA], gq_ref[...]).astype(BF16)
    ckv = _rms(c[:, Q_LORA:Q_LORA + KV_LORA], gkv_ref[...]).astype(BF16)
    q = jnp.dot(cq, wq_ref[...].astype(BF16), preferred_element_type=F32)
    k = jnp.dot(ckv, wk_ref[...].astype(BF16), preferred_element_type=F32)
    v_ref[...] = jnp.dot(ckv, wv_ref[...].astype(BF16), preferred_element_type=F32).astype(BF16)
    cos, sin = cos_ref[...], sin_ref[...]
    lane = lax.broadcasted_iota(jnp.int32, (MLA_TM, LANES), 1)
    rope_lane = (lane >= D_NOPE) & (lane < D_NOPE + D_ROPE)
    first = lane < D_NOPE + D_ROPE // 2
    kr = pltpu.roll(c[:, Q_LORA + KV_LORA:], D_NOPE, 1)
    kr = jnp.where(rope_lane, _rope_lanes(kr, cos, sin, D_ROPE // 2, first), 0.0)
    scale = (D_NOPE + D_ROPE) ** -0.5 * LOG2_E
    for h in range(MLA_HEADS):
        sl = slice(h * LANES, (h + 1) * LANES)
        q_ref[:, sl] = (_rope_lanes(q[:, sl], cos, sin, D_ROPE // 2, first) * scale).astype(BF16)
        k_ref[:, sl] = (k[:, sl] + kr).astype(BF16)


def _mla_prep(proj, g_qa, g_kva, w_uq, w_ukv, cos, sin):
    T = proj.shape[0]
    wq = jnp.pad(w_uq.reshape(Q_LORA, MLA_HEADS, D_NOPE + D_ROPE), ((0, 0), (0, 0), (0, LANES - D_NOPE - D_ROPE)))
    wq = wq.reshape(Q_LORA, MLA_HEADS * LANES)
    wkv = w_ukv.reshape(KV_LORA, MLA_HEADS, 2 * HEAD_DIM)
    wk = jnp.pad(wkv[:, :, :D_NOPE], ((0, 0), (0, 0), (0, LANES - D_NOPE))).reshape(KV_LORA, MLA_HEADS * LANES)
    wv = wkv[:, :, D_NOPE:].reshape(KV_LORA, MLA_HEADS * HEAD_DIM)
    full = lambda shape: pl.BlockSpec(shape, lambda i: (0, 0))
    per_seq = SEQ // MLA_TM
    return pl.pallas_call(
        _mla_prep_kernel,
        out_shape=[jax.ShapeDtypeStruct((T, MLA_HEADS * LANES), BF16),
                   jax.ShapeDtypeStruct((T, MLA_HEADS * LANES), BF16),
                   jax.ShapeDtypeStruct((T, MLA_HEADS * HEAD_DIM), BF16)],
        grid=(T // MLA_TM,),
        in_specs=[
            pl.BlockSpec((MLA_TM, MLA_BLOCK_W), lambda i: (i, MLA_COL0 // MLA_BLOCK_W)),
            full((1, Q_LORA)), full((1, KV_LORA)),
            full(wq.shape), full(wk.shape), full(wv.shape),
            pl.BlockSpec((MLA_TM, LANES), lambda i: (i % per_seq, 0)),
            pl.BlockSpec((MLA_TM, LANES), lambda i: (i % per_seq, 0)),
        ],
        out_specs=[pl.BlockSpec((MLA_TM, MLA_HEADS * LANES), lambda i: (i, 0)),
                   pl.BlockSpec((MLA_TM, MLA_HEADS * LANES), lambda i: (i, 0)),
                   pl.BlockSpec((MLA_TM, MLA_HEADS * HEAD_DIM), lambda i: (i, 0))],
        compiler_params=_cparams(1),
        name="mla_prep",
    )(proj, g_qa.reshape(1, Q_LORA), g_kva.reshape(1, KV_LORA), wq, wk, wv, cos, sin)


MLA_TQ = 1024
MLA_TK = 1024


def _mla_attn_kernel(q_ref, k_ref, v_ref, o_ref, v_sc, m_sc, acc_sc):
    half = lax.broadcasted_iota(jnp.int32, (SEQ, LANES), 1) < HEAD_DIM
    v = v_ref[...]
    one = jnp.ones_like(v)
    v_sc[0] = jnp.where(half, v, one)
    v_sc[1] = jnp.where(half, one, v)
    lane_half = lax.broadcasted_iota(jnp.int32, (MLA_TQ, LANES), 1) < HEAD_DIM

    def step(qb, kb, diagonal):
        for a in range(2):
            q = q_ref[pl.ds(qb, MLA_TQ), a * LANES:(a + 1) * LANES]
            k = k_ref[pl.ds(kb, MLA_TK), a * LANES:(a + 1) * LANES]
            s = lax.dot_general(q, k, (((1,), (1,)), ((), ())), preferred_element_type=F32)
            if diagonal:
                row = lax.broadcasted_iota(jnp.int32, (MLA_TQ, MLA_TK), 0)
                col = lax.broadcasted_iota(jnp.int32, (MLA_TQ, MLA_TK), 1)
                s = jnp.where(col + (kb - qb) <= row, s, NEG)
            m = m_sc[a]
            m_new = jnp.maximum(m, jnp.max(s, axis=1, keepdims=True))
            alpha = jnp.exp2(m - m_new)
            p = jnp.exp2(s - jnp.concatenate([m_new] * (MLA_TK // LANES), axis=1))
            pv = jnp.dot(p.astype(BF16), v_sc[a, pl.ds(kb, MLA_TK), :], preferred_element_type=F32)
            acc_sc[a] = alpha * acc_sc[a] + pv
            m_sc[a] = m_new

    def q_block(qi, c):
        qb = pl.multiple_of(qi * MLA_TQ, MLA_TQ)
        m_sc[...] = jnp.full(m_sc.shape, NEG, F32)
        acc_sc[...] = jnp.zeros(acc_sc.shape, F32)

        def k_block(kj, c2):
            step(qb, pl.multiple_of(kj * MLA_TK, MLA_TK), False)
            return c2

        lax.fori_loop(0, qi * (MLA_TQ // MLA_TK), k_block, 0)
        for d in range(MLA_TQ // MLA_TK):
            step(qb, qb + d * MLA_TK, True)
        outs = [acc_sc[a] / pltpu.roll(acc_sc[a], HEAD_DIM, 1) for a in range(2)]
        o_ref[pl.ds(qb, MLA_TQ), :] = jnp.where(lane_half, outs[0], outs[1]).astype(o_ref.dtype)
        return c

    lax.fori_loop(0, SEQ // MLA_TQ, q_block, 0)


def _mla_attention(q, k, v, nb):
    T = q.shape[0]
    return pl.pallas_call(
        _mla_attn_kernel,
        out_shape=jax.ShapeDtypeStruct((T, BRANCH_WIDTH), BF16),
        grid=(nb, BRANCH_WIDTH // LANES),
        in_specs=[
            pl.BlockSpec((SEQ, 2 * LANES), lambda b, hp: (b, hp)),
            pl.BlockSpec((SEQ, 2 * LANES), lambda b, hp: (b, hp)),
            pl.BlockSpec((SEQ, LANES), lambda b, hp: (b, hp)),
        ],
        out_specs=pl.BlockSpec((SEQ, LANES), lambda b, hp: (b, hp)),
        scratch_shapes=[pltpu.VMEM((2, SEQ, LANES), BF16),
                        pltpu.VMEM((2, MLA_TQ, LANES), F32),
                        pltpu.VMEM((2, MLA_TQ, LANES), F32)],
        compiler_params=_cparams(2),
        name="mla_attention",
    )(q, k, v)


MERGE_TM = 1024
MERGE_TN = 256


def _merge_kernel(*refs):
    n_ref = refs[0]
    o_refs = refs[1:5]
    wg_refs = refs[5:9]
    wb_refs = refs[9:13]
    out_ref = refs[13]
    n = n_ref[...]
    acc = jnp.zeros((MERGE_TM, MERGE_TN), F32)
    for i in range(N_BRANCH):
        gate = lax.dot_general(n, wg_refs[i][...], NT_DIMS, preferred_element_type=F32)
        br = jnp.dot(o_refs[i][...], wb_refs[i][...].astype(BF16), preferred_element_type=F32)
        acc = acc + br * jax.nn.sigmoid(gate)
    out_ref[...] = acc.astype(out_ref.dtype)


GATE_CAST_ROWS = 256


def _gate_cast_kernel(layer, w_hbm, o_ref, buf, sem):
    j = pl.program_id(0)

    def copy(step, slot):
        rows = pl.ds(pl.multiple_of(GATE_COL0 + step * GATE_CAST_ROWS, 8), GATE_CAST_ROWS)
        return pltpu.make_async_copy(w_hbm.at[layer, rows, :], buf.at[slot], sem.at[slot])

    @pl.when(j == 0)
    def _():
        copy(0, 0).start()

    @pl.when(j + 1 < pl.num_programs(0))
    def _():
        copy(j + 1, (j + 1) % 2).start()

    copy(j, j % 2).wait()
    o_ref[...] = buf[j % 2].astype(o_ref.dtype)


def _gate_weights_t(w_in_t, layer):
    n_gate = N_BRANCH * D_MODEL
    D = w_in_t.shape[2]
    assert GATE_COL0 + n_gate == w_in_t.shape[1] and GATE_COL0 % 8 == 0
    return pl.pallas_call(
        functools.partial(_gate_cast_kernel, layer),
        out_shape=jax.ShapeDtypeStruct((n_gate, D), BF16),
        grid=(n_gate // GATE_CAST_ROWS,),
        in_specs=[pl.BlockSpec(memory_space=pl.ANY)],
        out_specs=pl.BlockSpec((GATE_CAST_ROWS, D), lambda j: (j, 0)),
        scratch_shapes=[pltpu.VMEM((2, GATE_CAST_ROWS, D), F32), pltpu.SemaphoreType.DMA((2,))],
        compiler_params=_cparams(1),
        name="gate_weight_cast",
    )(w_in_t)


def _merge(n, branches, w_gates_t, w_branch, layer):
    T, D = n.shape
    nblk = D // MERGE_TN
    in_specs = [pl.BlockSpec((MERGE_TM, D), lambda m, j: (m, 0))]
    in_specs += [pl.BlockSpec((MERGE_TM, BRANCH_WIDTH), lambda m, j: (m, 0))] * N_BRANCH
    in_specs += [pl.BlockSpec((MERGE_TN, D), lambda m, j, i=i: (i * nblk + j, 0)) for i in range(N_BRANCH)]
    in_specs += [pl.BlockSpec((None, None, BRANCH_WIDTH, MERGE_TN), lambda m, j, i=i: (layer, i, 0, j))
                 for i in range(N_BRANCH)]
    return pl.pallas_call(
        _merge_kernel,
        out_shape=jax.ShapeDtypeStruct((T, D), BF16),
        grid=(T // MERGE_TM, nblk),
        in_specs=in_specs,
        out_specs=pl.BlockSpec((MERGE_TM, MERGE_TN), lambda m, j: (m, j)),
        compiler_params=_cparams(2),
        name="gated_merge",
    )(n, *branches, *([w_gates_t] * N_BRANCH), *([w_branch] * N_BRANCH))


def _matmul_res_kernel(a_ref, w_ref, r_ref, o_ref):
    o_ref[...] = r_ref[...] + jnp.dot(a_ref[...], w_ref[...].astype(BF16), preferred_element_type=F32)


def _matmul_res(a, w, layer, res, tm, tn):
    M, K = a.shape
    N = w.shape[2]
    return pl.pallas_call(
        _matmul_res_kernel,
        out_shape=jax.ShapeDtypeStruct((M, N), F32),
        grid=(M // tm, N // tn),
        in_specs=[pl.BlockSpec((tm, K), lambda m, j: (m, 0)),
                  pl.BlockSpec((None, K, tn), lambda m, j: (layer, 0, j)),
                  pl.BlockSpec((tm, tn), lambda m, j: (m, j))],
        out_specs=pl.BlockSpec((tm, tn), lambda m, j: (m, j)),
        compiler_params=_cparams(2),
        name="matmul_residual",
    )(a, w, res)


def _swiglu_up_kernel(x_ref, wg_ref, wu_ref, o_ref):
    x = x_ref[...]
    g = jnp.dot(x, wg_ref[...].astype(BF16), preferred_element_type=F32)
    u = jnp.dot(x, wu_ref[...].astype(BF16), preferred_element_type=F32)
    o_ref[...] = (jax.nn.silu(g) * u).astype(o_ref.dtype)


def _swiglu_up(x, wg, wu, layer, tm=1024, tn=512):
    M, K = x.shape
    N = wg.shape[2]
    return pl.pallas_call(
        _swiglu_up_kernel,
        out_shape=jax.ShapeDtypeStruct((M, N), BF16),
        grid=(M // tm, N // tn),
        in_specs=[pl.BlockSpec((tm, K), lambda m, j: (m, 0)),
                  pl.BlockSpec((None, K, tn), lambda m, j: (layer, 0, j)),
                  pl.BlockSpec((None, K, tn), lambda m, j: (layer, 0, j))],
        out_specs=pl.BlockSpec((tm, tn), lambda m, j: (m, j)),
        compiler_params=_cparams(2),
        name="swiglu_up",
    )(x, wg, wu)


ROUTER_TM = 512


def _router_kernel(h_ref, g_ref, wr_ref, idx_ref, w_ref):
    n = _rms(h_ref[...], g_ref[...])
    wr = wr_ref[...]
    n_hi = n.astype(BF16)
    n_lo = (n - n_hi.astype(F32)).astype(BF16)
    w_hi = wr.astype(BF16)
    w_lo = (wr - w_hi.astype(F32)).astype(BF16)
    logits = (jnp.dot(n_hi, w_hi, preferred_element_type=F32) + jnp.dot(n_lo, w_hi, preferred_element_type=F32)
              + jnp.dot(n_hi, w_lo, preferred_element_type=F32))
    lane = lax.broadcasted_iota(jnp.int32, logits.shape, 1)
    v1 = jnp.max(logits, axis=1, keepdims=True)
    i1 = jnp.min(jnp.where(logits == v1, lane, N_EXPERTS), axis=1, keepdims=True)
    rest = jnp.where(lane == i1, -jnp.inf, logits)
    v2 = jnp.max(rest, axis=1, keepdims=True)
    i2 = jnp.min(jnp.where(rest == v2, lane, N_EXPERTS), axis=1, keepdims=True)
    e2 = jnp.exp(v2 - v1)
    den = 1.0 + e2
    two = lax.broadcasted_iota(jnp.int32, (ROUTER_TM, TOP_K), 1)
    idx_ref[...] = jnp.where(two == 0, i1, i2)
    w_ref[...] = jnp.where(two == 0, 1.0 / den, e2 / den)


def _router(h, g, w_router):
    T, D = h.shape
    return pl.pallas_call(
        _router_kernel,
        out_shape=[jax.ShapeDtypeStruct((T, TOP_K), jnp.int32), jax.ShapeDtypeStruct((T, TOP_K), F32)],
        grid=(T // ROUTER_TM,),
        in_specs=[pl.BlockSpec((ROUTER_TM, D), lambda i: (i, 0)),
                  pl.BlockSpec((1, D), lambda i: (0, 0)),
                  pl.BlockSpec((D, N_EXPERTS), lambda i: (0, 0))],
        out_specs=[pl.BlockSpec((ROUTER_TM, TOP_K), lambda i: (i, 0)),
                   pl.BlockSpec((ROUTER_TM, TOP_K), lambda i: (i, 0))],
        compiler_params=_cparams(1),
        name="moe_router",
    )(h, g.reshape(1, D), w_router)


def _row_copy(src_hbm, row, dst_vmem, slot, sem):
    return pltpu.make_async_copy(src_hbm.at[pl.ds(row, 1), :], dst_vmem.at[pl.ds(slot, 1), :], sem)


def _expert_kernel(n_f, plan_ref, item_e_ref, item_row_ref, item_sub_ref, src_ref,
                   h_hbm, g_ref, wg_ref, wu_ref, wd_ref, y_hbm, gbuf, x_sc, acc_sc, sem_in, sem_out):
    it = pl.program_id(0)
    f = pl.program_id(1)
    n_it = pl.num_programs(0)
    n_items = plan_ref[0]
    live = it < n_items
    row0 = pl.multiple_of(item_row_ref[it], MOE_SUB)
    n_sub = item_sub_ref[it]
    item_rows = x_sc.shape[0]
    g_rows = gbuf.shape[0]
    chunk = g_rows // n_f

    def gather_wait():
        pltpu.make_async_copy(h_hbm.at[pl.ds(0, g_rows), :], gbuf, sem_in).wait()

    def y_copy(s):
        rows = pl.ds(s * MOE_SUB, MOE_SUB)
        return pltpu.make_async_copy(acc_sc.at[rows, :], y_hbm.at[pl.ds(row0 + s * MOE_SUB, MOE_SUB), :], sem_out)

    @pl.when(live & (it == 0) & (f == 0))
    def _():
        def start(r, c):
            _row_copy(h_hbm, src_ref[row0 + r], gbuf, r, sem_in).start()
            return c
        lax.fori_loop(0, g_rows, start, 0)

    @pl.when(live & (f == 0))
    def _():
        gather_wait()

        def norm(i, c):
            rows = pl.ds(pl.multiple_of(i * MOE_SUB, MOE_SUB), MOE_SUB)
            x_sc[rows, :] = _rms(gbuf[rows, :], g_ref[...]).astype(BF16)
            return c
        lax.fori_loop(0, item_rows // MOE_SUB, norm, 0)
        acc_sc[...] = jnp.zeros(acc_sc.shape, F32)

    @pl.when((it == n_items) & (it > 0) & (f == 0))
    def _():
        gather_wait()

    next_row0 = item_row_ref[jnp.minimum(it + 1, n_it - 1)]

    for n in range(1, MOE_ITEM_SUBS + 1):
        @pl.when(live & (n_sub == n))
        def _(n=n):
            lo = f * chunk
            for r in range(chunk):
                _row_copy(h_hbm, src_ref[next_row0 + lo + r], gbuf, lo + r, sem_in).start()
            rows = n * MOE_SUB
            x = x_sc[:rows, :]
            g = jnp.dot(x, wg_ref[...].astype(BF16), preferred_element_type=F32)
            u = jnp.dot(x, wu_ref[...].astype(BF16), preferred_element_type=F32)
            mid = (jax.nn.silu(g) * u).astype(BF16)
            acc_sc[:rows, :] += jnp.dot(mid, wd_ref[...].astype(BF16), preferred_element_type=F32)

    @pl.when(live & (f == n_f - 1))
    def _():
        for s in range(MOE_ITEM_SUBS):
            @pl.when(s < n_sub)
            def _(s=s):
                y_copy(s).start()
        for s in range(MOE_ITEM_SUBS):
            @pl.when(s < n_sub)
            def _(s=s):
                y_copy(s).wait()

    @pl.when((it == n_it - 1) & (f == n_f - 1))
    def _():
        @pl.when(live)
        def _():
            gather_wait()

        acc_sc[:MOE_SUB, :] = jnp.zeros((MOE_SUB, acc_sc.shape[1]), F32)
        first = plan_ref[1] // MOE_SUB
        n_blocks = y_hbm.shape[0] // MOE_SUB

        def fill(s):
            dst = y_hbm.at[pl.ds(pl.multiple_of(s * MOE_SUB, MOE_SUB), MOE_SUB), :]
            return pltpu.make_async_copy(acc_sc.at[pl.ds(0, MOE_SUB), :], dst, sem_out)

        lax.fori_loop(first, n_blocks, lambda s, c: (fill(s).start(), c)[1], 0)
        lax.fori_loop(first, n_blocks, lambda s, c: (fill(s).wait(), c)[1], 0)


def _gather_rows_per_item(n_f):
    item_rows = MOE_SUB * MOE_ITEM_SUBS
    chunk = -(-item_rows // n_f)
    chunk += -chunk % 8
    return chunk * n_f


def _expert_ffn(h, g, w_gate, w_up, w_down, plan, item_e, item_row, item_sub, src_rows, max_rows, max_items):
    D = h.shape[1]
    n_f = w_gate.shape[3] // MOE_TF
    item_rows = MOE_SUB * MOE_ITEM_SUBS

    def f_of(it, f, plan_ref):
        return jnp.where(it < plan_ref[0], f, n_f - 1)

    return pl.pallas_call(
        functools.partial(_expert_kernel, n_f),
        out_shape=jax.ShapeDtypeStruct((max_rows, D), F32),
        grid_spec=pltpu.PrefetchScalarGridSpec(
            num_scalar_prefetch=5,
            grid=(max_items, n_f),
            in_specs=[
                pl.BlockSpec(memory_space=pl.ANY),
                pl.BlockSpec((1, D), lambda it, f, n, e, r, s, src: (0, 0)),
                pl.BlockSpec((None, None, D, MOE_TF), lambda it, f, n, e, r, s, src: (0, e[it], 0, f_of(it, f, n))),
                pl.BlockSpec((None, None, D, MOE_TF), lambda it, f, n, e, r, s, src: (0, e[it], 0, f_of(it, f, n))),
                pl.BlockSpec((None, None, MOE_TF, D), lambda it, f, n, e, r, s, src: (0, e[it], f_of(it, f, n), 0)),
            ],
            out_specs=pl.BlockSpec(memory_space=pl.ANY),
            scratch_shapes=[pltpu.VMEM((_gather_rows_per_item(n_f), D), F32),
                            pltpu.VMEM((item_rows, D), BF16),
                            pltpu.VMEM((item_rows, D), F32),
                            pltpu.SemaphoreType.DMA(()), pltpu.SemaphoreType.DMA(())],
        ),
        compiler_params=_cparams(2),
        name="moe_expert_ffn",
    )(plan, item_e, item_row, item_sub, src_rows, h, g.reshape(1, D), w_gate, w_up, w_down)


COMBINE_TM = 256


def _combine_kernel(pos_ref, h_ref, w_ref, g_ref, y_hbm, o_ref, buf, sem):
    i = pl.program_id(0)

    def issue(step, slot):
        base = step * COMBINE_TM

        def start(r, c):
            for k in range(TOP_K):
                _row_copy(y_hbm, pos_ref[(base + r) * TOP_K + k], buf.at[slot, k], r, sem.at[slot]).start(priority=k)
            return c

        lax.fori_loop(0, COMBINE_TM, start, 0, unroll=8)

    @pl.when(i == 0)
    def _():
        issue(0, 0)

    @pl.when(i + 1 < pl.num_programs(0))
    def _():
        issue(i + 1, (i + 1) % 2)

    slot = i % 2
    for k in range(TOP_K):
        pltpu.make_async_copy(y_hbm.at[pl.ds(0, COMBINE_TM), :], buf.at[slot, k], sem.at[slot]).wait()
    w = w_ref[...]
    out = h_ref[...] + w[:, 0:1] * buf[slot, 0] + w[:, 1:2] * buf[slot, 1]
    o_ref[...] = _rms(out, g_ref[...])


def _combine_norm(h, ys, pos, top_w, g):
    T, D = h.shape
    return pl.pallas_call(
        _combine_kernel,
        out_shape=jax.ShapeDtypeStruct((T, D), F32),
        grid_spec=pltpu.PrefetchScalarGridSpec(
            num_scalar_prefetch=1,
            grid=(T // COMBINE_TM,),
            in_specs=[pl.BlockSpec((COMBINE_TM, D), lambda i, p: (i, 0)),
                      pl.BlockSpec((COMBINE_TM, TOP_K), lambda i, p: (i, 0)),
                      pl.BlockSpec((1, D), lambda i, p: (0, 0)),
                      pl.BlockSpec(memory_space=pl.ANY)],
            out_specs=pl.BlockSpec((COMBINE_TM, D), lambda i, p: (i, 0)),
            scratch_shapes=[pltpu.VMEM((2, TOP_K, COMBINE_TM, D), F32), pltpu.SemaphoreType.DMA((2,))],
        ),
        compiler_params=_cparams(1),
        name="moe_combine_norm",
    )(pos.reshape(-1), h, top_w, g.reshape(1, D), ys)


def _moe_plan(top_idx, gather_rows):
    T = top_idx.shape[0]
    n_assign = T * TOP_K
    item_rows = MOE_SUB * MOE_ITEM_SUBS
    max_rows = n_assign + N_EXPERTS * MOE_SUB
    max_items = n_assign // item_rows + N_EXPERTS
    e_flat = top_idx.reshape(-1)
    onehot = (e_flat[:, None] == jnp.arange(N_EXPERTS)[None, :]).astype(jnp.int32)
    ranks = jnp.cumsum(onehot, axis=0) - onehot
    counts = jnp.sum(onehot, axis=0)
    padded = ((counts + MOE_SUB - 1) // MOE_SUB) * MOE_SUB
    group_start = jnp.cumsum(padded) - padded
    rank = jnp.sum(ranks * onehot, axis=1)
    pos = group_start[e_flat] + rank
    src_rows = jnp.zeros((max_rows + gather_rows,), jnp.int32).at[pos].set(
        jnp.arange(n_assign, dtype=jnp.int32) // TOP_K, unique_indices=True)
    items_per_e = (padded + item_rows - 1) // item_rows
    item_start = jnp.cumsum(items_per_e) - items_per_e
    n_items = jnp.sum(items_per_e)
    it = jnp.arange(max_items)
    item_e = jnp.minimum(jnp.sum(it[:, None] >= (item_start + items_per_e)[None, :], axis=1), N_EXPERTS - 1)
    local = it - item_start[item_e]
    item_row = group_start[item_e] + local * item_rows
    item_sub = jnp.clip((padded[item_e] - local * item_rows) // MOE_SUB, 0, MOE_ITEM_SUBS)
    live = it < n_items
    last = jnp.maximum(n_items - 1, 0)
    item_e = jnp.where(live, item_e, item_e[last])
    item_row = jnp.where(live, item_row, 0)
    item_sub = jnp.where(live, item_sub, 0)
    i32 = lambda a: a.astype(jnp.int32)
    plan = jnp.stack([n_items, jnp.sum(padded)])
    return (i32(pos.reshape(T, TOP_K)), src_rows, i32(plan), i32(item_e), i32(item_row),
            i32(item_sub), max_rows, max_items)


def _moe_and_final_norm(h, g_ffn, w_router, w_gate, w_up, w_down, g_final):
    top_idx, top_w = _router(h, g_ffn, w_router)
    gather_rows = _gather_rows_per_item(w_gate.shape[3] // MOE_TF)
    pos, src_rows, plan, item_e, item_row, item_sub, max_rows, max_items = _moe_plan(top_idx, gather_rows)
    ys = _expert_ffn(h, g_ffn, w_gate, w_up, w_down, plan, item_e, item_row, item_sub, src_rows, max_rows,
                     max_items)
    return _combine_norm(h, ys, pos, top_w, g_final)


def kernel(x, w_in, w_branch, w_out, norm_mix, norm_ffn, norm_final, sinks, mla_q_norm, mla_kv_norm, mla_w_uq,
           mla_w_ukv, ffn_w_gate, ffn_w_up, ffn_w_down, router_w, moe_w_gate, moe_w_up, moe_w_down):
    nb, seq, d = x.shape
    assert (seq, d) == (SEQ, D_MODEL)
    depth = w_in.shape[0]
    assert depth == 2, "layer 0 uses the dense FFN, layer 1 the MoE followed by the final norm"
    cos64, sin64 = _rope_tables(HEAD_DIM, 0, HEAD_DIM)
    cos32, sin32 = _rope_tables(D_ROPE, D_NOPE, LANES)
    h = x.reshape(nb * seq, d)
    w_in_t = jnp.swapaxes(w_in, 1, 2)
    out = None
    for layer in range(depth):
        n = _rmsnorm(h, norm_mix[layer], BF16)
        proj = _in_proj(n, w_in_t, layer, cos64, sin64)
        o_a = _dilated_attention(proj, nb)
        o_b = _sink_attention(proj, sinks[layer], nb)
        o_c = _stick_attention(proj, nb)
        q_d, k_d, v_d = _mla_prep(proj, mla_q_norm[layer], mla_kv_norm[layer], mla_w_uq[layer], mla_w_ukv[layer],
                                  cos32, sin32)
        o_d = _mla_attention(q_d, k_d, v_d, nb)
        w_gates_t = _gate_weights_t(w_in_t, layer)
        merged = _merge(n, (o_a, o_b, o_c, o_d), w_gates_t, w_branch, layer)
        h = _matmul_res(merged, w_out, layer, h, 2048, 512)
        if layer == 0:
            n2 = _rmsnorm(h, norm_ffn[layer], BF16)
            mid = _swiglu_up(n2, ffn_w_gate, ffn_w_up, 0)
            h = _matmul_res(mid, ffn_w_down, 0, h, 1024, 256)
        else:
            out = _moe_and_final_norm(h, norm_ffn[layer], router_w[0], moe_w_gate, moe_w_up, moe_w_down, norm_final)
    return out.reshape(nb, seq, d)
```
